```python
import jax, jax.numpy as jnp
from jax import lax
import numpy as np

D_MODEL = 1024
BATCH = 8
SEQ = 4096
DEPTH = 4

MEM_LEN = 256
LRU_WIDTH = D_MODEL // 2
LRU_BLOCKS = 8
LRU_BLOCK_DIM = LRU_WIDTH // LRU_BLOCKS
LRU_CONV = 4
LRU_C = 8.0
NSA_HEADS = 8
NSA_KV_HEADS = 2
HEAD_DIM = 64
CMP_LEN = 32
CMP_STRIDE = 16
CMP_HIDDEN = 128
SEL_BLOCK = 64
SEL_TOP = 16
WINDOW = 512
NSA_QBLOCK = 64
ROPE_THETA = 500000.0
ROT_DIM = HEAD_DIM // 4
SC_WIDTH = D_MODEL
SC_CONV = 3
XA_HEADS = 4
XA_HEAD_DIM = D_MODEL // XA_HEADS
N_GROUPS = 4
EXPERTS_PER_GROUP = 4
N_EXPERTS = N_GROUPS * EXPERTS_PER_GROUP
TOP_K = 2
D_EXPERT = D_MODEL // 2

EPS = 1e-6
NEG = -1e30
FORCE = 1e9
N_EVEN = (DEPTH + 1) // 2
N_ODD = DEPTH // 2
MIX_WIDTH = LRU_WIDTH + NSA_HEADS * HEAD_DIM
EVEN_SPLITS = (LRU_WIDTH, LRU_WIDTH, NSA_HEADS * HEAD_DIM) + (NSA_KV_HEADS * HEAD_DIM,) * 6 + (NSA_HEADS * 3,)
EVEN_PROJ = int(sum(EVEN_SPLITS))
EVEN_OFFSETS = tuple(int(o) for o in np.cumsum(EVEN_SPLITS)[:-1])

kernel_name = 'hybrid_rglru_nsa_shortconv_hmoe'


def rms_norm(x, g):
    x32 = x.astype(jnp.float32)
    y = x32 * lax.rsqrt(jnp.mean(x32 * x32, axis=-1, keepdims=True) + EPS)
    return (y * g.astype(jnp.float32)).astype(x.dtype)


def rope_tables(positions):
    inv = ROPE_THETA ** (-jnp.arange(0, ROT_DIM, 2, dtype=jnp.float32) / ROT_DIM)
    ang = positions.astype(jnp.float32)[..., None] * inv
    return jnp.cos(ang), jnp.sin(ang)


def apply_partial_rope(x, cos, sin):
    half = ROT_DIM // 2
    c = cos[:, :, None, :].astype(x.dtype)
    s = sin[:, :, None, :].astype(x.dtype)
    x1 = x[..., :half]
    x2 = x[..., half:ROT_DIM]
    return jnp.concatenate([x1 * c - x2 * s, x2 * c + x1 * s, x[..., ROT_DIM:]], axis=-1)


def causal_dwconv(x, w, b=None):
    width, ch = w.shape
    y = lax.conv_general_dilated(x, w[:, None, :].astype(x.dtype), window_strides=(1,),
                                 padding=[(width - 1, 0)],
                                 dimension_numbers=('NWC', 'WIO', 'NWC'),
                                 feature_group_count=ch)
    if b is not None:
        y = y + b.astype(x.dtype)
    return y


def masked_softmax(s, mask):
    p = jax.nn.softmax(jnp.where(mask, s, NEG), axis=-1)
    return jnp.where(mask, p, 0.0)


def rg_lru(x, gate_branch, conv_w, conv_b, w_r, b_r, w_i, b_i, lam):
    xc = causal_dwconv(x, conv_w, conv_b)
    bb, t, w = xc.shape
    xb = xc.reshape(bb, t, LRU_BLOCKS, LRU_BLOCK_DIM)
    r = jax.nn.sigmoid(jnp.einsum('btki,kij->btkj', xb, w_r).reshape(bb, t, w) + b_r)
    i = jax.nn.sigmoid(jnp.einsum('btki,kij->btkj', xb, w_i).reshape(bb, t, w) + b_i)
    log_a = -LRU_C * r.astype(jnp.float32) * jax.nn.softplus(-lam.astype(jnp.float32))
    a = jnp.exp(log_a)
    u = jnp.sqrt(-jnp.expm1(2.0 * log_a)) * (i * xc).astype(jnp.float32)

    def combine(left, right):
        a1, b1 = left
        a2, b2 = right
        return a1 * a2, a2 * b1 + b2

    _, h = lax.associative_scan(combine, (a, u), axis=1)
    return h.astype(x.dtype) * jax.nn.gelu(gate_branch)


def compress_kv(kv, pos_emb, w1, w2):
    bb, t, g, dh = kv.shape
    ch = kv.reshape(bb, t // CMP_STRIDE, CMP_STRIDE, g, dh)
    blocks = jnp.concatenate([ch[:, :-1], ch[:, 1:]], axis=2)
    blocks = blocks + pos_emb[None, None, :, None, :]
    ncmp = blocks.shape[1]
    flat = blocks.transpose(0, 3, 1, 2, 4).reshape(bb, g, ncmp, CMP_LEN * dh)
    return jax.nn.gelu(flat @ w1) @ w2


def nsa(q, k_cmp, v_cmp, k_sel, v_sel, k_win, v_win, gate_logits, cos, sin,
        pos_k, w1_k, w2_k, pos_v, w1_v, w2_v):
    bb, t, nh, dh = q.shape
    g = k_cmp.shape[2]
    hg = nh // g
    scale = dh ** -0.5
    kc = compress_kv(k_cmp, pos_k, w1_k, w2_k)
    vc = compress_kv(v_cmp, pos_v, w1_v, w2_v)
    n_cmp = kc.shape[2]
    n_sel = t // SEL_BLOCK
    n_top = min(SEL_TOP, n_sel)
    cmp_start = jnp.arange(n_cmp) * CMP_STRIDE
    cmp_end = cmp_start + CMP_LEN - 1
    sel_start = jnp.arange(n_sel) * SEL_BLOCK
    cover = ((cmp_start[:, None] <= sel_start[None, :] + SEL_BLOCK - 1)
             & (cmp_end[:, None] >= sel_start[None, :])).astype(jnp.float32)

    def kv_major(z):
        return z.transpose(0, 2, 1, 3)

    ks = kv_major(apply_partial_rope(k_sel, cos, sin)).reshape(bb, g, n_sel, SEL_BLOCK, dh)
    vs = kv_major(v_sel).reshape(bb, g, n_sel, SEL_BLOCK, dh)
    pad = ((0, 0), (0, 0), (WINDOW, 0), (0, 0))
    kw = jnp.pad(kv_major(apply_partial_rope(k_win, cos, sin)), pad)
    vw = jnp.pad(kv_major(v_win), pad)

    def by_query_block(z):
        c = z.shape[-1]
        z = z.reshape(bb, t // NSA_QBLOCK, NSA_QBLOCK, g, hg, c)
        return z.transpose(1, 0, 3, 4, 2, 5)

    qn = by_query_block(q)
    qr = by_query_block(apply_partial_rope(q, cos, sin))
    gt = by_query_block(jax.nn.sigmoid(gate_logits.astype(jnp.float32)).astype(q.dtype))
    b_ix = jnp.arange(bb)[:, None, None, None]
    g_ix = jnp.arange(g)[None, :, None, None]
    blk = jnp.arange(n_sel)
    ar_sel = jnp.arange(SEL_BLOCK)
    ar_win = jnp.arange(NSA_QBLOCK + WINDOW)

    def block(args):
        c, qn_c, qr_c, g_c = args
        tq = c * NSA_QBLOCK + jnp.arange(NSA_QBLOCK)
        s = jnp.einsum('bghqd,bgnd->bghqn', qn_c, kc).astype(jnp.float32) * scale
        p_cmp = masked_softmax(s, cmp_end[None, :] <= tq[:, None])
        o_cmp = jnp.einsum('bghqn,bgnd->bghqd', p_cmp.astype(vc.dtype), vc)
        imp = jnp.einsum('bghqn,ns->bgqs', p_cmp, cover)
        cur = tq // SEL_BLOCK
        forced = (blk[None, :] == 0) | (blk[None, :] == cur[:, None]) | (blk[None, :] == cur[:, None] - 1)
        causal = sel_start[None, :] <= tq[:, None]
        score = jnp.where(causal, jnp.where(forced, FORCE, imp), NEG)
        _, idx = lax.top_k(score, n_top)
        ksg = ks[b_ix, g_ix, idx]
        vsg = vs[b_ix, g_ix, idx]
        key_pos = idx[..., None] * SEL_BLOCK + ar_sel
        m_sel = (key_pos <= tq[:, None, None]).reshape(bb, g, 1, NSA_QBLOCK, n_top * SEL_BLOCK)
        s_sel = jnp.einsum('bghqd,bgqnkd->bghqnk', qr_c, ksg).astype(jnp.float32) * scale
        shp = s_sel.shape
        p_sel = masked_softmax(s_sel.reshape(bb, g, hg, NSA_QBLOCK, n_top * SEL_BLOCK), m_sel).reshape(shp)
        o_sel = jnp.einsum('bghqnk,bgqnkd->bghqd', p_sel.astype(vsg.dtype), vsg)
        start = c * NSA_QBLOCK
        kwc = lax.dynamic_slice_in_dim(kw, start, NSA_QBLOCK + WINDOW, axis=2)
        vwc = lax.dynamic_slice_in_dim(vw, start, NSA_QBLOCK + WINDOW, axis=2)
        kpos = start - WINDOW + ar_win
        m_w = ((kpos[None, :] >= 0) & (kpos[None, :] <= tq[:, None])
               & (kpos[None, :] > tq[:, None] - WINDOW))
        s_w = jnp.einsum('bghqd,bgkd->bghqk', qr_c, kwc).astype(jnp.float32) * scale
        p_w = masked_softmax(s_w, m_w)
        o_w = jnp.einsum('bghqk,bgkd->bghqd', p_w.astype(vwc.dtype), vwc)
        return g_c[..., 0:1] * o_cmp + g_c[..., 1:2] * o_sel + g_c[..., 2:3] * o_w

    out = lax.map(block, (jnp.arange(t // NSA_QBLOCK), qn, qr, gt))
    return out.transpose(1, 0, 4, 2, 3, 5).reshape(bb, t, nh * dh)


def even_mixer(xn, w_in, w_out, conv_w, conv_b, w_r, b_r, w_i, b_i, lam,
               pos_k, w1_k, w2_k, pos_v, w1_v, w2_v, cos, sin):
    bb, t, _ = xn.shape
    parts = jnp.split(xn @ w_in, EVEN_OFFSETS, axis=-1)
    x_lru, g_lru, q, kc, vc, ks, vs, kw, vw, gl = parts
    y_lru = rg_lru(x_lru, g_lru, conv_w, conv_b, w_r, b_r, w_i, b_i, lam)

    def kvh(z):
        return z.reshape(bb, t, NSA_KV_HEADS, HEAD_DIM)

    y_nsa = nsa(q.reshape(bb, t, NSA_HEADS, HEAD_DIM), kvh(kc), kvh(vc), kvh(ks), kvh(vs),
                kvh(kw), kvh(vw), gl.reshape(bb, t, NSA_HEADS, 3), cos, sin,
                pos_k, w1_k, w2_k, pos_v, w1_v, w2_v)
    return jnp.concatenate([y_lru, y_nsa], axis=-1) @ w_out


def short_conv_mixer(xn, w_in, conv_w, w_out):
    b_g, c_g, v = jnp.split(xn @ w_in, 3, axis=-1)
    return (b_g * causal_dwconv(c_g * v, conv_w)) @ w_out


def memory_cross_attn(xn, mem, g_mem, wq, wk, wv, wo):
    bb, t, _ = xn.shape
    mn = rms_norm(mem, g_mem)
    q = (xn @ wq).reshape(bb, t, XA_HEADS, XA_HEAD_DIM)
    k = (mn @ wk).reshape(bb, -1, XA_HEADS, XA_HEAD_DIM)
    v = (mn @ wv).reshape(bb, -1, XA_HEADS, XA_HEAD_DIM)
    s = jnp.einsum('bthd,bmhd->bhtm', q, k).astype(jnp.float32) * (XA_HEAD_DIM ** -0.5)
    p = jax.nn.softmax(s, axis=-1).astype(v.dtype)
    o = jnp.einsum('bhtm,bmhd->bthd', p, v).reshape(bb, t, XA_HEADS * XA_HEAD_DIM)
    return o @ wo


def hier_moe(xn, w_group, b_group, w_expert, b_expert, w_gate, w_up, w_down):
    bb, t, d = xn.shape
    xt = xn.reshape(-1, d)
    n = xt.shape[0]
    g_prob = jax.nn.softmax((xt @ w_group).astype(jnp.float32) + b_group.astype(jnp.float32), axis=-1)
    g_top, g_idx = lax.top_k(g_prob, 1)
    e_logits = ((xt @ w_expert).astype(jnp.float32) + b_expert.astype(jnp.float32)).reshape(n, N_GROUPS, EXPERTS_PER_GROUP)
    e_in = e_logits[jnp.arange(n), g_idx[:, 0]]
    e_top, e_idx = lax.top_k(e_in, TOP_K)
    w = jax.nn.softmax(e_top, axis=-1) * g_top
    eid = g_idx * EXPERTS_PER_GROUP + e_idx
    gate = jnp.sum(jax.nn.one_hot(eid, N_EXPERTS, dtype=jnp.float32) * w[..., None], axis=1).astype(xn.dtype)
    y = jnp.zeros_like(xt)
    for e in range(N_EXPERTS):
        h = jax.nn.silu(xt @ w_gate[e]) * (xt @ w_up[e])
        y = y + gate[:, e:e + 1] * (h @ w_down[e])
    return y.reshape(bb, t, d)


def setup_inputs(seed: int = 0) -> dict:
    key = jax.random.key(seed)
    keys = iter(jax.random.split(key, 64))
    res = (3 * DEPTH) ** -0.5

    def nrm(shape, scale):
        return jax.random.normal(next(keys), shape, jnp.float32) * scale

    def gain(shape):
        return 1.0 + nrm(shape, 0.02)

    def lru_lambda():
        u = jax.random.uniform(next(keys), (N_EVEN, LRU_WIDTH), jnp.float32, 0.9, 0.999)
        a = u ** (1.0 / LRU_C)
        return jnp.log(a) - jnp.log1p(-a)

    offset = jax.random.randint(next(keys), (BATCH, 1), 0, 1024)
    positions = (offset + jnp.arange(SEQ)[None, :]).astype(jnp.int32)
    return {
        'x': nrm((BATCH, SEQ, D_MODEL), 1.0),
        'mem': nrm((BATCH, MEM_LEN, D_MODEL), 1.0),
        'positions': positions,
        'norm_mix': gain((DEPTH, D_MODEL)),
        'norm_xattn': gain((DEPTH, D_MODEL)),
        'norm_mem': gain((DEPTH, D_MODEL)),
        'norm_ffn': gain((DEPTH, D_MODEL)),
        'norm_final': gain((D_MODEL,)),
        'even_w_in': nrm((N_EVEN, D_MODEL, EVEN_PROJ), D_MODEL ** -0.5),
        'even_w_out': nrm((N_EVEN, MIX_WIDTH, D_MODEL), MIX_WIDTH ** -0.5 * res),
        'lru_conv_w': nrm((N_EVEN, LRU_CONV, LRU_WIDTH), LRU_CONV ** -0.5),
        'lru_conv_b': nrm((N_EVEN, LRU_WIDTH), 0.01),
        'lru_w_r': nrm((N_EVEN, LRU_BLOCKS, LRU_BLOCK_DIM, LRU_BLOCK_DIM), LRU_BLOCK_DIM ** -0.5),
        'lru_b_r': nrm((N_EVEN, LRU_WIDTH), 0.01),
        'lru_w_i': nrm((N_EVEN, LRU_BLOCKS, LRU_BLOCK_DIM, LRU_BLOCK_DIM), LRU_BLOCK_DIM ** -0.5),
        'lru_b_i': nrm((N_EVEN, LRU_WIDTH), 0.01),
        'lru_lambda': lru_lambda(),
        'nsa_cmp_pos_k': nrm((N_EVEN, CMP_LEN, HEAD_DIM), 0.02),
        'nsa_cmp_w1_k': nrm((N_EVEN, CMP_LEN * HEAD_DIM, CMP_HIDDEN), (CMP_LEN * HEAD_DIM) ** -0.5),
        'nsa_cmp_w2_k': nrm((N_EVEN, CMP_HIDDEN, HEAD_DIM), CMP_HIDDEN ** -0.5),
        'nsa_cmp_pos_v': nrm((N_EVEN, CMP_LEN, HEAD_DIM), 0.02),
        'nsa_cmp_w1_v': nrm((N_EVEN, CMP_LEN * HEAD_DIM, CMP_HIDDEN), (CMP_LEN * HEAD_DIM) ** -0.5),
        'nsa_cmp_w2_v': nrm((N_EVEN, CMP_HIDDEN, HEAD_DIM), CMP_HIDDEN ** -0.5),
        'odd_w_in': nrm((N_ODD, D_MODEL, 3 * SC_WIDTH), D_MODEL ** -0.5),
        'odd_conv_w': nrm((N_ODD, SC_CONV, SC_WIDTH), SC_CONV ** -0.5),
        'odd_w_out': nrm((N_ODD, SC_WIDTH, D_MODEL), SC_WIDTH ** -0.5 * res),
        'xa_wq': nrm((DEPTH, D_MODEL, XA_HEADS * XA_HEAD_DIM), D_MODEL ** -0.5),
        'xa_wk': nrm((DEPTH, D_MODEL, XA_HEADS * XA_HEAD_DIM), D_MODEL ** -0.5),
        'xa_wv': nrm((DEPTH, D_MODEL, XA_HEADS * XA_HEAD_DIM), D_MODEL ** -0.5),
        'xa_wo': nrm((DEPTH, XA_HEADS * XA_HEAD_DIM, D_MODEL), (XA_HEADS * XA_HEAD_DIM) ** -0.5 * res),
        'moe_w_group': nrm((DEPTH, D_MODEL, N_GROUPS), D_MODEL ** -0.5),
        'moe_b_group': nrm((DEPTH, N_GROUPS), 0.01),
        'moe_w_expert': nrm((DEPTH, D_MODEL, N_EXPERTS), D_MODEL ** -0.5),
        'moe_b_expert': nrm((DEPTH, N_EXPERTS), 0.01),
        'moe_w_gate': nrm((DEPTH, N_EXPERTS, D_MODEL, D_EXPERT), D_MODEL ** -0.5),
        'moe_w_up': nrm((DEPTH, N_EXPERTS, D_MODEL, D_EXPERT), D_MODEL ** -0.5),
        'moe_w_down': nrm((DEPTH, N_EXPERTS, D_EXPERT, D_MODEL), D_EXPERT ** -0.5 * res),
    }


def reference(x, mem, positions, norm_mix, norm_xattn, norm_mem, norm_ffn, norm_final,
              even_w_in, even_w_out, lru_conv_w, lru_conv_b, lru_w_r, lru_b_r, lru_w_i, lru_b_i,
              lru_lambda, nsa_cmp_pos_k, nsa_cmp_w1_k, nsa_cmp_w2_k, nsa_cmp_pos_v, nsa_cmp_w1_v,
              nsa_cmp_w2_v, odd_w_in, odd_conv_w, odd_w_out, xa_wq, xa_wk, xa_wv, xa_wo,
              moe_w_group, moe_b_group, moe_w_expert, moe_b_expert, moe_w_gate, moe_w_up, moe_w_down):
    cos, sin = rope_tables(positions)
    h = x
    for layer in range(DEPTH):
        xn = rms_norm(h, norm_mix[layer])
        if layer % 2 == 0:
            e = layer // 2
            h = h + even_mixer(xn, even_w_in[e], even_w_out[e], lru_conv_w[e], lru_conv_b[e],
                               lru_w_r[e], lru_b_r[e], lru_w_i[e], lru_b_i[e], lru_lambda[e],
                               nsa_cmp_pos_k[e], nsa_cmp_w1_k[e], nsa_cmp_w2_k[e],
                               nsa_cmp_pos_v[e], nsa_cmp_w1_v[e], nsa_cmp_w2_v[e], cos, sin)
        else:
            o = layer // 2
            h = h + short_conv_mixer(xn, odd_w_in[o], odd_conv_w[o], odd_w_out[o])
        h = h + memory_cross_attn(rms_norm(h, norm_xattn[layer]), mem, norm_mem[layer],
                                  xa_wq[layer], xa_wk[layer], xa_wv[layer], xa_wo[layer])
        h = h + hier_moe(rms_norm(h, norm_ffn[layer]), moe_w_group[layer], moe_b_group[layer],
                         moe_w_expert[layer], moe_b_expert[layer], moe_w_gate[layer],
                         moe_w_up[layer], moe_w_down[layer])
    return rms_norm(h, norm_final)
```

```python
import functools

import jax
import jax.numpy as jnp
import numpy as np
from jax import lax
from jax.experimental import pallas as pl
from jax.experimental.pallas import tpu as pltpu

F32 = jnp.float32
MXU_DTYPE = jnp.bfloat16

D_MODEL = 1024
LRU_WIDTH = 512
LRU_CONV = 4
LRU_C = 8.0
NSA_HEADS = 8
NSA_KV_HEADS = 2
HEADS_PER_GROUP = NSA_HEADS // NSA_KV_HEADS
HEAD_DIM = 64
CMP_STRIDE = 16
CMP_LEN = 32
CMP_HIDDEN = 128
SEL_BLOCK = 64
SEL_TOP = 16
WINDOW = 512
ROT_HALF = 8
ROPE_THETA = 500000.0
SC_CONV = 3
XA_HEADS = 4
XA_HEAD_DIM = 256
N_GROUPS = 4
EXPERTS_PER_GROUP = 4
N_EXPERTS = 16
D_EXPERT = 512
EPS = 1e-6
NEG = -1e30
FORCE = 1e9

LANES = 128
SUBLANES = 8
VMEM_LIMIT = 56 * 1024 * 1024

C_XL, C_GL, C_Q, C_KCV, C_KS, C_VS, C_KW, C_VW, C_GATE, C_END = (
    0, 512, 1024, 1536, 1792, 1920, 2048, 2176, 2304, 2560)

ROW_TILE = 512
NSA_TQ = 128
NSA_TK = 256
EXPERT_TILE = 256


def _cparams(n_axes):
    return pltpu.CompilerParams(dimension_semantics=("arbitrary",) * n_axes,
                                vmem_limit_bytes=VMEM_LIMIT)


def _mm(a, b):
    return jnp.dot(a, b, preferred_element_type=F32)


def _mm_nt(a, b):
    return lax.dot_general(a, b, (((1,), (1,)), ((), ())), preferred_element_type=F32)


def _rms(x, g):
    return x * lax.rsqrt(jnp.mean(x * x, axis=-1, keepdims=True) + EPS) * g


def _shift_rows(xx, s, rows):
    if s == 0:
        return xx[SUBLANES:SUBLANES + rows]
    return pltpu.roll(xx, s, 0)[SUBLANES:SUBLANES + rows]


def _rope(x, cos_t, sin_t):
    width = x.shape[-1]
    lane = lax.broadcasted_iota(jnp.int32, x.shape, 1) % HEAD_DIM
    partner = jnp.where(lane < ROT_HALF, pltpu.roll(x, width - ROT_HALF, 1), pltpu.roll(x, ROT_HALF, 1))
    return x * cos_t + partner * sin_t


def _even_proj_kernel(h_ref, g_ref, w_ref, cos_ref, sin_ref,
                      xl_ref, gg_ref, qn_ref, qr_ref, kcv_ref, ksr_ref, vs_ref, kwr_ref, vw_ref,
                      gate_ref):
    xn = _rms(h_ref[...], g_ref[...]).astype(MXU_DTYPE)

    def proj(c0, c1):
        return _mm(xn, w_ref[:, c0:c1])

    cos_t = cos_ref[...]
    sin_t = sin_ref[...]
    xl_ref[...] = proj(C_XL, C_GL)
    gg_ref[...] = jax.nn.gelu(proj(C_GL, C_Q))
    q = proj(C_Q, C_KCV) * (HEAD_DIM ** -0.5)
    reps = (C_KCV - C_Q) // LANES
    cos_q = jnp.concatenate([cos_t] * reps, axis=1)
    sin_q = jnp.concatenate([sin_t] * reps, axis=1)
    qn_ref[...] = q.astype(qn_ref.dtype)
    qr_ref[...] = _rope(q, cos_q, sin_q).astype(qr_ref.dtype)
    kcv_ref[...] = proj(C_KCV, C_KS)
    ksr_ref[...] = _rope(proj(C_KS, C_VS), cos_t, sin_t).astype(ksr_ref.dtype)
    vs_ref[...] = proj(C_VS, C_KW).astype(vs_ref.dtype)
    kwr_ref[...] = _rope(proj(C_KW, C_VW), cos_t, sin_t).astype(kwr_ref.dtype)
    vw_ref[...] = proj(C_VW, C_GATE).astype(vw_ref.dtype)
    gate_ref[...] = jax.nn.sigmoid(proj(C_GATE, C_END))


def _even_proj(h, g, w, cos_t, sin_t):
    n = h.shape[0]
    tm = min(ROW_TILE, n)
    row = lambda c: pl.BlockSpec((tm, c), lambda i: (i, 0))
    full = lambda a: pl.BlockSpec(a.shape, lambda i: (0,) * a.ndim)
    widths = [(512, F32), (512, F32), (512, MXU_DTYPE), (512, MXU_DTYPE), (256, F32),
              (128, MXU_DTYPE), (128, MXU_DTYPE), (128, MXU_DTYPE), (128, MXU_DTYPE), (256, F32)]
    return pl.pallas_call(
        _even_proj_kernel,
        grid=(n // tm,),
        in_specs=[row(D_MODEL), full(g), full(w), row(LANES), row(LANES)],
        out_specs=[row(c) for c, _ in widths],
        out_shape=[jax.ShapeDtypeStruct((n, c), dt) for c, dt in widths],
        compiler_params=_cparams(1),
        name="even_proj",
    )(h, g, w, cos_t, sin_t)


def _lru_kernel(xl_ref, gg_ref, cw_ref, cb_ref, wr_ref, wi_ref, br_ref, bi_ref, sp_ref,
                y_ref, tail_ref, h_ref, a_scr, u_scr):
    tt = xl_ref.shape[0]

    @pl.when(pl.program_id(1) == 0)
    def _():
        tail_ref[...] = jnp.zeros_like(tail_ref)
        h_ref[...] = jnp.zeros_like(h_ref)

    x = xl_ref[...]
    xx = jnp.concatenate([tail_ref[...], x], axis=0)
    tail_ref[...] = x[tt - SUBLANES:tt]
    xc = cb_ref[...] + sum(cw_ref[k:k + 1, :] * _shift_rows(xx, LRU_CONV - 1 - k, tt)
                           for k in range(LRU_CONV))
    xcb = xc.astype(MXU_DTYPE)
    half = LRU_WIDTH // 2
    r_lin = jnp.concatenate([_mm(xcb[:, j * half:(j + 1) * half], wr_ref[j]) for j in range(2)], axis=1)
    i_lin = jnp.concatenate([_mm(xcb[:, j * half:(j + 1) * half], wi_ref[j]) for j in range(2)], axis=1)
    r = jax.nn.sigmoid(r_lin + br_ref[...])
    i = jax.nn.sigmoid(i_lin + bi_ref[...])
    log_a = -LRU_C * r * sp_ref[...]
    a = jnp.exp(log_a)
    u = jnp.sqrt(jnp.tanh(-log_a) * (1.0 + a * a)) * (i * xc)

    r8 = lax.broadcasted_iota(jnp.int32, a.shape, 0) % SUBLANES
    for s in (1, 2, 4):
        keep = r8 >= s
        u = jnp.where(keep, a * pltpu.roll(u, s, 0) + u, u)
        a = jnp.where(keep, a * pltpu.roll(a, s, 0), a)
    a_scr[...] = a
    u_scr[...] = u

    def body(gidx, h):
        r0 = pl.multiple_of(gidx * SUBLANES, SUBLANES)
        out = a_scr[pl.ds(r0, SUBLANES), :] * h + u_scr[pl.ds(r0, SUBLANES), :]
        u_scr[pl.ds(r0, SUBLANES), :] = out
        return out[SUBLANES - 1:SUBLANES, :]

    h_ref[...] = lax.fori_loop(0, tt // SUBLANES, body, h_ref[...])
    y_ref[...] = (u_scr[...] * gg_ref[...]).astype(y_ref.dtype)


def _lru(xl, gg, cw, cb, wr, wi, br, bi, sp, batch):
    n = xl.shape[0]
    t = n // batch
    tt = min(ROW_TILE, t)
    nt = t // tt
    row = pl.BlockSpec((tt, LRU_WIDTH), lambda b, i: (b * nt + i, 0))
    full = lambda a: pl.BlockSpec(a.shape, lambda b, i: (0,) * a.ndim)
    return pl.pallas_call(
        _lru_kernel,
        grid=(batch, nt),
        in_specs=[row, row] + [full(a) for a in (cw, cb, wr, wi, br, bi, sp)],
        out_specs=row,
        out_shape=jax.ShapeDtypeStruct((n, LRU_WIDTH), MXU_DTYPE),
        scratch_shapes=[pltpu.VMEM((SUBLANES, LRU_WIDTH), F32), pltpu.VMEM((1, LRU_WIDTH), F32),
                        pltpu.VMEM((tt, LRU_WIDTH), F32), pltpu.VMEM((tt, LRU_WIDTH), F32)],
        compiler_params=_cparams(2),
        name="lru",
    )(xl, gg, cw, cb, wr, wi, br, bi, sp)


def _compress_kernel(x_ref, pos_ref, w1_ref, w2_ref, o_ref):
    x = x_ref[0, 0]
    nchunk = x.shape[0]
    lo = (x + pos_ref[0, 0]).astype(MXU_DTYPE)
    hi = (x + pos_ref[0, 1]).astype(MXU_DTYPE)
    p_lo = _mm(lo, w1_ref[0, 0])
    p_hi = _mm(hi, w1_ref[0, 1])
    a = p_lo + pltpu.roll(p_hi, nchunk - 1, 0)
    o_ref[0, 0] = _mm(jax.nn.gelu(a).astype(MXU_DTYPE), w2_ref[0]).astype(o_ref.dtype)


def _compress(x16, pos, w1, w2):
    b, _, nchunk, width = x16.shape
    return pl.pallas_call(
        _compress_kernel,
        grid=(b, 2 * NSA_KV_HEADS),
        in_specs=[pl.BlockSpec((1, 1, nchunk, width), lambda i, j: (i, j, 0, 0)),
                  pl.BlockSpec((1, 2, 1, width), lambda i, j: (j // NSA_KV_HEADS, 0, 0, 0)),
                  pl.BlockSpec((1, 2, width, CMP_HIDDEN), lambda i, j: (j // NSA_KV_HEADS, 0, 0, 0)),
                  pl.BlockSpec((1, CMP_HIDDEN, HEAD_DIM), lambda i, j: (j // NSA_KV_HEADS, 0, 0))],
        out_specs=pl.BlockSpec((1, 1, nchunk, HEAD_DIM), lambda i, j: (i, j, 0, 0)),
        out_shape=jax.ShapeDtypeStruct((b, 2 * NSA_KV_HEADS, nchunk, HEAD_DIM), MXU_DTYPE),
        compiler_params=_cparams(2),
        name="compress",
    )(x16, pos, w1, w2)


def _nsa_kernel(qn_ref, qr_ref, kc_ref, vc_ref, ks_ref, vs_ref, kw_ref, vw_ref, gate_ref,
                y_ref, m_scr, l_scr, acc_scr):
    tq = qn_ref.shape[2]
    rows = HEADS_PER_GROUP * tq
    t = ks_ref.shape[2]
    nc = kc_ref.shape[2]
    nsel = t // SEL_BLOCK
    n_top = min(SEL_TOP, nsel)
    tk = min(NSA_TK, t)
    t0 = pl.program_id(2) * tq

    qn = qn_ref[0].reshape(rows, HEAD_DIM)
    qr = qr_ref[0].reshape(rows, HEAD_DIM)

    s = _mm_nt(qn, kc_ref[0, 0])
    tq_row = t0 + lax.broadcasted_iota(jnp.int32, (rows, nc), 0) % tq
    cmp_end = lax.broadcasted_iota(jnp.int32, (rows, nc), 1) * CMP_STRIDE + (CMP_LEN - 1)
    valid = cmp_end <= tq_row
    s = jnp.where(valid, s, NEG)
    e = jnp.where(valid, jnp.exp(s - jnp.max(s, axis=-1, keepdims=True)), 0.0)
    den = jnp.sum(e, axis=-1, keepdims=True)
    p = e / jnp.where(den > 0.0, den, 1.0)
    o_cmp = _mm(p.astype(MXU_DTYPE), vc_ref[0, 0])

    psum = p[0:tq]
    for hh in range(1, HEADS_PER_GROUP):
        psum = psum + p[hh * tq:(hh + 1) * tq]
    p_hi = psum.astype(MXU_DTYPE)
    p_lo = (psum - p_hi.astype(F32)).astype(MXU_DTYPE)
    cj = lax.broadcasted_iota(jnp.int32, (nsel, nc), 0)
    cn = lax.broadcasted_iota(jnp.int32, (nsel, nc), 1)
    ratio = SEL_BLOCK // CMP_STRIDE
    cover = jnp.where((cn >= ratio * cj - (CMP_LEN // CMP_STRIDE - 1)) & (cn <= ratio * cj + ratio - 1)
                      & (cn < nc - 1), 1.0, 0.0).astype(MXU_DTYPE)
    imp = _mm_nt(cover, p_hi) + _mm_nt(cover, p_lo)

    blk = lax.broadcasted_iota(jnp.int32, (nsel, tq), 0)
    tq_col = t0 + lax.broadcasted_iota(jnp.int32, (nsel, tq), 1)
    cur = tq_col // SEL_BLOCK
    forced = (blk == 0) | (blk == cur) | (blk == cur - 1)
    causal = blk * SEL_BLOCK <= tq_col
    score0 = jnp.where(causal, jnp.where(forced, FORCE, imp), NEG)
    blk_f = blk.astype(F32)

    def pick_one(_, carry):
        score, chosen = carry
        best = jnp.max(score, axis=0, keepdims=True)
        first = jnp.min(jnp.where(score == best, blk_f, float(nsel)), axis=0, keepdims=True)
        hit = blk_f == first
        return jnp.where(hit, -jnp.inf, score), jnp.where(hit, 1.0, chosen)

    _, chosen_t = lax.fori_loop(0, n_top, pick_one, (score0, jnp.zeros((nsel, tq), F32)))
    chosen = chosen_t.T.astype(MXU_DTYPE)

    def sweep(k_ref, v_ref, kt_lo, kt_hi, allowed_fn):
        m_scr[...] = jnp.full_like(m_scr, NEG)
        l_scr[...] = jnp.zeros_like(l_scr)
        acc_scr[...] = jnp.zeros_like(acc_scr)

        def body(kt, _):
            k0 = pl.multiple_of(kt * tk, tk)
            kpos = k0 + lax.broadcasted_iota(jnp.int32, (tq, tk), 1)
            tqm = t0 + lax.broadcasted_iota(jnp.int32, (tq, tk), 0)
            bias = jnp.where(allowed_fn(k0, kpos, tqm), 0.0, NEG)
            bias = jnp.concatenate([bias] * HEADS_PER_GROUP, axis=0)
            sc = _mm_nt(qr, k_ref[0, 0, pl.ds(k0, tk), :]) + bias
            m_old = m_scr[...]
            m_new = jnp.maximum(m_old, jnp.max(sc, axis=-1, keepdims=True))
            alpha = jnp.exp(m_old - m_new)
            pe = jnp.exp(sc - m_new)
            l_scr[...] = alpha * l_scr[...] + jnp.sum(pe, axis=-1, keepdims=True)
            acc_scr[...] = alpha * acc_scr[...] + _mm(pe.astype(MXU_DTYPE), v_ref[0, 0, pl.ds(k0, tk), :])
            m_scr[...] = m_new
            return 0

        lax.fori_loop(kt_lo, kt_hi, body, 0)
        return acc_scr[...] / l_scr[...]

    kt_hi = (t0 + tq - 1) // tk + 1

    def sel_allowed(k0, kpos, tqm):
        ej = lax.broadcasted_iota(jnp.int32, (nsel, tk), 0)
        ek = k0 + lax.broadcasted_iota(jnp.int32, (nsel, tk), 1)
        expand = jnp.where(ej == ek // SEL_BLOCK, 1.0, 0.0).astype(MXU_DTYPE)
        return (_mm(chosen, expand) > 0.5) & (kpos <= tqm)

    def win_allowed(k0, kpos, tqm):
        return (kpos <= tqm) & (kpos > tqm - WINDOW)

    o_sel = sweep(ks_ref, vs_ref, 0, kt_hi, sel_allowed)
    o_win = sweep(kw_ref, vw_ref, jnp.maximum(t0 - (WINDOW - 1), 0) // tk, kt_hi, win_allowed)

    gate = gate_ref[0]
    outs = []
    for hh in range(HEADS_PER_GROUP):
        sl = slice(hh * tq, (hh + 1) * tq)
        g_cmp = gate[:, hh:hh + 1]
        g_sel = gate[:, HEADS_PER_GROUP + hh:HEADS_PER_GROUP + hh + 1]
        g_win = gate[:, 2 * HEADS_PER_GROUP + hh:2 * HEADS_PER_GROUP + hh + 1]
        outs.append(g_cmp * o_cmp[sl] + g_sel * o_sel[sl] + g_win * o_win[sl])
    y_ref[0] = jnp.concatenate(outs, axis=1).astype(y_ref.dtype)


def _nsa(qn, qr, kvc, ks, vs, kw, vw, gates):
    b, _, t, _ = qn.shape
    nc = kvc.shape[2]
    tq = min(NSA_TQ, t)
    rows = HEADS_PER_GROUP * tq
    q_spec = pl.BlockSpec((1, HEADS_PER_GROUP, tq, HEAD_DIM), lambda i, g, j: (i, g, j, 0))
    kv_spec = pl.BlockSpec((1, 1, t, HEAD_DIM), lambda i, g, j: (i, g, 0, 0))
    return pl.pallas_call(
        _nsa_kernel,
        grid=(b, NSA_KV_HEADS, t // tq),
        in_specs=[q_spec, q_spec,
                  pl.BlockSpec((1, 1, nc, HEAD_DIM), lambda i, g, j: (i, g, 0, 0)),
                  pl.BlockSpec((1, 1, nc, HEAD_DIM), lambda i, g, j: (i, NSA_KV_HEADS + g, 0, 0)),
                  kv_spec, kv_spec, kv_spec, kv_spec,
                  pl.BlockSpec((1, tq, LANES), lambda i, g, j: (i, j, g))],
        out_specs=pl.BlockSpec((1, tq, HEADS_PER_GROUP * HEAD_DIM), lambda i, g, j: (i, j, g)),
        out_shape=jax.ShapeDtypeStruct((b, t, NSA_HEADS * HEAD_DIM), MXU_DTYPE),
        scratch_shapes=[pltpu.VMEM((rows, 1), F32), pltpu.VMEM((rows, 1), F32),
                        pltpu.VMEM((rows, HEAD_DIM), F32)],
        compiler_params=_cparams(3),
        name="nsa",
    )(qn, qr, kvc, kvc, ks, vs, kw, vw, gates)


def _out_proj_kernel(h_ref, a_ref, b_ref, wa_ref, wb_ref, o_ref):
    o_ref[...] = h_ref[...] + _mm(a_ref[...], wa_ref[...]) + _mm(b_ref[...], wb_ref[...])


def _out_proj(h, a, b, wa, wb):
    n = h.shape[0]
    tm = min(ROW_TILE, n)
    full = lambda x: pl.BlockSpec(x.shape, lambda i: (0,) * x.ndim)
    return pl.pallas_call(
        _out_proj_kernel,
        grid=(n // tm,),
        in_specs=[pl.BlockSpec((tm, D_MODEL), lambda i: (i, 0)),
                  pl.BlockSpec((tm, a.shape[1]), lambda i: (i, 0)),
                  pl.BlockSpec((tm, b.shape[1]), lambda i: (i, 0)), full(wa), full(wb)],
        out_specs=pl.BlockSpec((tm, D_MODEL), lambda i: (i, 0)),
        out_shape=jax.ShapeDtypeStruct((n, D_MODEL), F32),
        compiler_params=_cparams(1),
        name="out_proj",
    )(h, a, b, wa, wb)


def _short_conv_kernel(h_ref, g_ref, win_ref, cw_ref, wout_ref, o_ref, tail_ref):
    tt = h_ref.shape[0]

    @pl.when(pl.program_id(1) == 0)
    def _():
        tail_ref[...] = jnp.zeros_like(tail_ref)

    h = h_ref[...]
    xn = _rms(h, g_ref[...]).astype(MXU_DTYPE)
    b_g = _mm(xn, win_ref[:, 0:D_MODEL])
    cv = _mm(xn, win_ref[:, D_MODEL:2 * D_MODEL]) * _mm(xn, win_ref[:, 2 * D_MODEL:3 * D_MODEL])
    xx = jnp.concatenate([tail_ref[...], cv], axis=0)
    tail_ref[...] = cv[tt - SUBLANES:tt]
    conv = sum(cw_ref[k:k + 1, :] * _shift_rows(xx, SC_CONV - 1 - k, tt) for k in range(SC_CONV))
    o_ref[...] = h + _mm((b_g * conv).astype(MXU_DTYPE), wout_ref[...])


def _short_conv(h, g, w_in, cw, w_out, batch):
    n = h.shape[0]
    t = n // batch
    tt = min(ROW_TILE, t)
    nt = t // tt
    row = pl.BlockSpec((tt, D_MODEL), lambda b, i: (b * nt + i, 0))
    full = lambda a: pl.BlockSpec(a.shape, lambda b, i: (0,) * a.ndim)
    return pl.pallas_call(
        _short_conv_kernel,
        grid=(batch, nt),
        in_specs=[row, full(g), full(w_in), full(cw), full(w_out)],
        out_specs=row,
        out_shape=jax.ShapeDtypeStruct((n, D_MODEL), F32),
        scratch_shapes=[pltpu.VMEM((SUBLANES, D_MODEL), F32)],
        compiler_params=_cparams(2),
        name="short_conv",
    )(h, g, w_in, cw, w_out)


def _mem_kv_kernel(m_ref, g_ref, w_ref, o_ref):
    o_ref[...] = _mm(_rms(m_ref[...], g_ref[...]).astype(MXU_DTYPE), w_ref[...]).astype(o_ref.dtype)


def _mem_kv(mem, g, wkv):
    n = mem.shape[0]
    tm = min(ROW_TILE, n)
    tn = 1024
    return pl.pallas_call(
        _mem_kv_kernel,
        grid=(n // tm, wkv.shape[1] // tn),
        in_specs=[pl.BlockSpec((tm, D_MODEL), lambda i, j: (i, 0)),
                  pl.BlockSpec(g.shape, lambda i, j: (0, 0)),
                  pl.BlockSpec((D_MODEL, tn), lambda i, j: (0, j))],
        out_specs=pl.BlockSpec((tm, tn), lambda i, j: (i, j)),
        out_shape=jax.ShapeDtypeStruct((n, wkv.shape[1]), MXU_DTYPE),
        compiler_params=_cparams(2),
        name="mem_kv",
    )(mem, g, wkv)


def _xattn_kernel(h_ref, g_ref, wq_ref, kv_ref, wo_ref, o_ref):
    h = h_ref[...]
    xn = _rms(h, g_ref[...]).astype(MXU_DTYPE)
    q = (_mm(xn, wq_ref[...]) * (XA_HEAD_DIM ** -0.5)).astype(MXU_DTYPE)
    width = XA_HEADS * XA_HEAD_DIM
    outs = []
    for hd in range(XA_HEADS):
        sl = slice(hd * XA_HEAD_DIM, (hd + 1) * XA_HEAD_DIM)
        s = _mm_nt(q[:, sl], kv_ref[:, sl])
        e = jnp.exp(s - jnp.max(s, axis=-1, keepdims=True))
        p = e / jnp.sum(e, axis=-1, keepdims=True)
        outs.append(_mm(p.astype(MXU_DTYPE), kv_ref[:, width + hd * XA_HEAD_DIM:width + (hd + 1) * XA_HEAD_DIM]))
    o = jnp.concatenate(outs, axis=1).astype(MXU_DTYPE)
    o_ref[...] = h + _mm(o, wo_ref[...])


def _xattn(h, g, wq, kv, wo, batch):
    n = h.shape[0]
    t = n // batch
    tm = min(ROW_TILE, t)
    nt = t // tm
    mlen = kv.shape[0] // batch
    full = lambda a: pl.BlockSpec(a.shape, lambda b, i: (0,) * a.ndim)
    row = pl.BlockSpec((tm, D_MODEL), lambda b, i: (b * nt + i, 0))
    return pl.pallas_call(
        _xattn_kernel,
        grid=(batch, nt),
        in_specs=[row, full(g), full(wq), pl.BlockSpec((mlen, kv.shape[1]), lambda b, i: (b, 0)), full(wo)],
        out_specs=row,
        out_shape=jax.ShapeDtypeStruct((n, D_MODEL), F32),
        compiler_params=_cparams(2),
        name="xattn",
    )(h, g, wq, kv, wo)


def _router_kernel(h_ref, g_ref, whi_ref, wlo_ref, b_ref, ri_ref, rw_ref, cnt_ref, carry_ref):
    tm = h_ref.shape[0]

    @pl.when(pl.program_id(0) == 0)
    def _():
        carry_ref[...] = jnp.zeros_like(carry_ref)

    xn = _rms(h_ref[...], g_ref[...])
    x_hi = xn.astype(MXU_DTYPE)
    x_lo = (xn - x_hi.astype(F32)).astype(MXU_DTYPE)
    logits = _mm(x_hi, whi_ref[...]) + (_mm(x_lo, whi_ref[...]) + _mm(x_hi, wlo_ref[...])) + b_ref[...]

    lane = lax.broadcasted_iota(jnp.int32, logits.shape, 1)
    lane_f = lane.astype(F32)
    none = float(LANES)
    is_g = lane < N_GROUPS
    g_max = jnp.max(jnp.where(is_g, logits, -jnp.inf), axis=-1, keepdims=True)
    g_sum = jnp.sum(jnp.where(is_g, jnp.exp(logits - g_max), 0.0), axis=-1, keepdims=True)
    g_top = 1.0 / g_sum
    g_idx = jnp.min(jnp.where(is_g & (logits == g_max), lane_f, none), axis=-1, keepdims=True)
    first = N_GROUPS + EXPERTS_PER_GROUP * g_idx
    in_g = (lane_f >= first) & (lane_f < first + EXPERTS_PER_GROUP)
    e1 = jnp.max(jnp.where(in_g, logits, -jnp.inf), axis=-1, keepdims=True)
    i1 = jnp.min(jnp.where(in_g & (logits == e1), lane_f, none), axis=-1, keepdims=True)
    rest = in_g & (lane_f != i1)
    e2 = jnp.max(jnp.where(rest, logits, -jnp.inf), axis=-1, keepdims=True)
    i2 = jnp.min(jnp.where(rest & (logits == e2), lane_f, none), axis=-1, keepdims=True)
    ratio = jnp.exp(e2 - e1)
    w1 = g_top / (1.0 + ratio)
    w2 = g_top * ratio / (1.0 + ratio)

    onehot = jnp.where((lane_f == i1) | (lane_f == i2), 1.0, 0.0)
    tri = jnp.where(lax.broadcasted_iota(jnp.int32, (tm, tm), 0) >= lax.broadcasted_iota(jnp.int32, (tm, tm), 1),
                    1.0, 0.0).astype(MXU_DTYPE)
    incl = _mm(tri, onehot.astype(MXU_DTYPE))
    before = incl - onehot + carry_ref[...]
    carry_ref[...] = carry_ref[...] + incl[tm - 1:tm, :]
    rank1 = jnp.sum(jnp.where(lane_f == i1, before, 0.0), axis=-1, keepdims=True)
    rank2 = jnp.sum(jnp.where(lane_f == i2, before, 0.0), axis=-1, keepdims=True)

    ri = jnp.where(lane == 0, i1 - N_GROUPS,
                   jnp.where(lane == 1, i2 - N_GROUPS, jnp.where(lane == 2, rank1, jnp.where(lane == 3, rank2, 0.0))))
    ri_ref[...] = ri.astype(jnp.int32)
    rw_ref[...] = jnp.where(lane == 0, w1, jnp.where(lane == 1, w2, 0.0))
    cnt_ref[...] = jnp.broadcast_to(carry_ref[...], cnt_ref.shape).astype(jnp.int32)


def _router(h, g, w_hi, w_lo, bias):
    n = h.shape[0]
    tm = min(ROW_TILE, n)
    full = lambda a: pl.BlockSpec(a.shape, lambda i: (0,) * a.ndim)
    return pl.pallas_call(
        _router_kernel,
        grid=(n // tm,),
        in_specs=[pl.BlockSpec((tm, D_MODEL), lambda i: (i, 0)), full(g), full(w_hi), full(w_lo), full(bias)],
        out_specs=[pl.BlockSpec((tm, LANES), lambda i: (i, 0)), pl.BlockSpec((tm, LANES), lambda i: (i, 0)),
                   pl.BlockSpec((SUBLANES, LANES), lambda i: (i, 0))],
        out_shape=[jax.ShapeDtypeStruct((n, LANES), jnp.int32), jax.ShapeDtypeStruct((n, LANES), F32),
                   jax.ShapeDtypeStruct((n // tm * SUBLANES, LANES), jnp.int32)],
        scratch_shapes=[pltpu.VMEM((1, LANES), F32)],
        compiler_params=_cparams(1),
        name="router",
    )(h, g, w_hi, w_lo, bias)


def _row_copy(src_ref, src_row, dst_ref, dst_row, sem):
    return pltpu.make_async_copy(src_ref.at[pl.ds(src_row, 1)], dst_ref.at[pl.ds(dst_row, 1)], sem)


def _dispatch_kernel(pos_ref, h_ref, g_ref, xs_in_ref, xs_ref, xn_scr, sem):
    del xs_in_ref
    tm = h_ref.shape[0]
    base = pl.program_id(0) * (2 * tm)
    xn_scr[...] = _rms(h_ref[...], g_ref[...])

    def issue(r, _):
        _row_copy(xn_scr, r, xs_ref, pos_ref[base + 2 * r], sem).start()
        _row_copy(xn_scr, r, xs_ref, pos_ref[base + 2 * r + 1], sem).start()
        return 0

    def drain(r, _):
        _row_copy(xn_scr, r, xs_ref, pos_ref[base + 2 * r], sem).wait()
        _row_copy(xn_scr, r, xs_ref, pos_ref[base + 2 * r + 1], sem).wait()
        return 0

    lax.fori_loop(0, tm, issue, 0)
    lax.fori_loop(0, tm, drain, 0)


def _dispatch(pos, h, g, xs_zero):
    n = h.shape[0]
    tm = min(ROW_TILE, n)
    grid_spec = pltpu.PrefetchScalarGridSpec(
        num_scalar_prefetch=1,
        grid=(n // tm,),
        in_specs=[pl.BlockSpec((tm, D_MODEL), lambda i, pos: (i, 0)),
                  pl.BlockSpec(g.shape, lambda i, pos: (0, 0)),
                  pl.BlockSpec(memory_space=pl.ANY)],
        out_specs=pl.BlockSpec(memory_space=pl.ANY),
        scratch_shapes=[pltpu.VMEM((tm, D_MODEL), F32), pltpu.SemaphoreType.DMA(())],
    )
    return pl.pallas_call(
        _dispatch_kernel,
        grid_spec=grid_spec,
        out_shape=jax.ShapeDtypeStruct(xs_zero.shape, F32),
        input_output_aliases={3: 0},
        compiler_params=_cparams(1),
        name="dispatch",
    )(pos, h, g, xs_zero)


def _expert_kernel(te_ref, ta_ref, x_ref, wg_ref, wu_ref, wd_ref, y_ref):
    j = pl.program_id(0)

    @pl.when(ta_ref[j] == 1)
    def _():
        x = x_ref[...].astype(MXU_DTYPE)
        hid = jax.nn.silu(_mm(x, wg_ref[0])) * _mm(x, wu_ref[0])
        y_ref[...] = _mm(hid.astype(MXU_DTYPE), wd_ref[0])

    @pl.when(ta_ref[j] == 0)
    def _():
        y_ref[...] = jnp.zeros_like(y_ref)


def _experts(tile_expert, tile_active, xs, wg, wu, wd):
    p = xs.shape[0]
    grid_spec = pltpu.PrefetchScalarGridSpec(
        num_scalar_prefetch=2,
        grid=(p // EXPERT_TILE,),
        in_specs=[pl.BlockSpec((EXPERT_TILE, D_MODEL), lambda j, te, ta: (j, 0)),
                  pl.BlockSpec((1, D_MODEL, D_EXPERT), lambda j, te, ta: (te[j], 0, 0)),
                  pl.BlockSpec((1, D_MODEL, D_EXPERT), lambda j, te, ta: (te[j], 0, 0)),
                  pl.BlockSpec((1, D_EXPERT, D_MODEL), lambda j, te, ta: (te[j], 0, 0))],
        out_specs=pl.BlockSpec((EXPERT_TILE, D_MODEL), lambda j, te, ta: (j, 0)),
    )
    return pl.pallas_call(
        _expert_kernel,
        grid_spec=grid_spec,
        out_shape=jax.ShapeDtypeStruct((p, D_MODEL), F32),
        compiler_params=_cparams(1),
        name="experts",
    )(tile_expert, tile_active, xs, wg, wu, wd)


def _combine_kernel(pos_ref, h_ref, rw_ref, ys_ref, o_ref, y1_scr, y2_scr, sem):
    tm = h_ref.shape[0]
    base = pl.program_id(0) * (2 * tm)

    def issue(r, _):
        _row_copy(ys_ref, pos_ref[base + 2 * r], y1_scr, r, sem).start()
        _row_copy(ys_ref, pos_ref[base + 2 * r + 1], y2_scr, r, sem).start()
        return 0

    def drain(r, _):
        _row_copy(ys_ref, pos_ref[base + 2 * r], y1_scr, r, sem).wait()
        _row_copy(ys_ref, pos_ref[base + 2 * r + 1], y2_scr, r, sem).wait()
        return 0

    lax.fori_loop(0, tm, issue, 0)
    lax.fori_loop(0, tm, drain, 0)
    rw = rw_ref[...]
    o_ref[...] = h_ref[...] + (rw[:, 0:1] * y1_scr[...] + rw[:, 1:2] * y2_scr[...])


def _combine(pos, h, rw, ys):
    n = h.shape[0]
    tm = min(ROW_TILE, n)
    grid_spec = pltpu.PrefetchScalarGridSpec(
        num_scalar_prefetch=1,
        grid=(n // tm,),
        in_specs=[pl.BlockSpec((tm, D_MODEL), lambda i, pos: (i, 0)),
                  pl.BlockSpec((tm, LANES), lambda i, pos: (i, 0)),
                  pl.BlockSpec(memory_space=pl.ANY)],
        out_specs=pl.BlockSpec((tm, D_MODEL), lambda i, pos: (i, 0)),
        scratch_shapes=[pltpu.VMEM((tm, D_MODEL), F32), pltpu.VMEM((tm, D_MODEL), F32),
                        pltpu.SemaphoreType.DMA(())],
    )
    return pl.pallas_call(
        _combine_kernel,
        grid_spec=grid_spec,
        out_shape=jax.ShapeDtypeStruct((n, D_MODEL), F32),
        compiler_params=_cparams(1),
        name="combine",
    )(pos, h, rw, ys)


def _final_norm_kernel(h_ref, g_ref, o_ref):
    o_ref[...] = _rms(h_ref[...], g_ref[...])


def _final_norm(h, g):
    n = h.shape[0]
    tm = min(ROW_TILE, n)
    return pl.pallas_call(
        _final_norm_kernel,
        grid=(n // tm,),
        in_specs=[pl.BlockSpec((tm, D_MODEL), lambda i: (i, 0)), pl.BlockSpec(g.shape, lambda i: (0, 0))],
        out_specs=pl.BlockSpec((tm, D_MODEL), lambda i: (i, 0)),
        out_shape=jax.ShapeDtypeStruct((n, D_MODEL), F32),
        compiler_params=_cparams(1),
        name="final_norm",
    )(h, g)


def _moe(h, g, w_group, b_group, w_expert, b_expert, w_gate, w_up, w_down):
    n = h.shape[0]
    w_r = jnp.zeros((D_MODEL, LANES), F32).at[:, :N_GROUPS].set(w_group)
    w_r = w_r.at[:, N_GROUPS:N_GROUPS + N_EXPERTS].set(w_expert)
    b_r = jnp.zeros((1, LANES), F32).at[0, :N_GROUPS].set(b_group)
    b_r = b_r.at[0, N_GROUPS:N_GROUPS + N_EXPERTS].set(b_expert)
    w_hi = w_r.astype(MXU_DTYPE)
    w_lo = (w_r - w_hi.astype(F32)).astype(MXU_DTYPE)
    ri, rw, cnt = _router(h, g, w_hi, w_lo, b_r)

    counts = cnt[-1, N_GROUPS:N_GROUPS + N_EXPERTS]
    padded = (counts + EXPERT_TILE - 1) // EXPERT_TILE * EXPERT_TILE
    ends = jnp.cumsum(padded)
    starts = ends - padded
    pos = (starts[ri[:, 0:2]] + ri[:, 2:4]).reshape(-1).astype(jnp.int32)
    p_rows = 2 * n + N_EXPERTS * EXPERT_TILE
    tile_start = jnp.arange(p_rows // EXPERT_TILE, dtype=jnp.int32) * EXPERT_TILE
    tile_expert = jnp.minimum(jnp.searchsorted(ends, tile_start, side="right"), N_EXPERTS - 1).astype(jnp.int32)
    tile_active = (tile_start < ends[-1]).astype(jnp.int32)

    xs = _dispatch(pos, h, g, jnp.zeros((p_rows, D_MODEL), F32))
    ys = _experts(tile_expert, tile_active, xs, w_gate.astype(MXU_DTYPE), w_up.astype(MXU_DTYPE),
                  w_down.astype(MXU_DTYPE))
    return _combine(pos, h, rw, ys)


def _even_weights(w_in):
    gate_cols = w_in[:, C_GATE:C_GATE + 3 * NSA_HEADS].reshape(D_MODEL, NSA_KV_HEADS, HEADS_PER_GROUP, 3)
    gate_cols = gate_cols.transpose(0, 1, 3, 2).reshape(D_MODEL, NSA_KV_HEADS, 3 * HEADS_PER_GROUP)
    gate_cols = jnp.pad(gate_cols, ((0, 0), (0, 0), (0, LANES - 3 * HEADS_PER_GROUP)))
    return jnp.concatenate([w_in[:, :C_GATE], gate_cols.reshape(D_MODEL, NSA_KV_HEADS * LANES)], axis=1)


def _block_diag_halves(w):
    blocks = w.shape[0] // 2
    out = jnp.zeros((2, blocks * w.shape[1], blocks * w.shape[2]), w.dtype)
    for j in range(2):
        for k in range(blocks):
            out = out.at[j, k * w.shape[1]:(k + 1) * w.shape[1], k * w.shape[2]:(k + 1) * w.shape[2]].set(
                w[j * blocks + k])
    return out


def _even_mixer(h, g, batch, cos_t, sin_t, w_in, w_out, conv_w, conv_b, w_r, b_r, w_i, b_i, lam,
                pos_k, w1_k, w2_k, pos_v, w1_v, w2_v):
    n = h.shape[0]
    t = n // batch
    xl, gg, qn, qr, kcv, ksr, vs, kwr, vw, gates = _even_proj(
        h, g, _even_weights(w_in).astype(MXU_DTYPE), cos_t, sin_t)

    y_lru = _lru(xl, gg, conv_w, conv_b[None, :], _block_diag_halves(w_r).astype(MXU_DTYPE),
                 _block_diag_halves(w_i).astype(MXU_DTYPE), b_r[None, :], b_i[None, :],
                 jax.nn.softplus(-lam)[None, :], batch)

    nchunk = t // CMP_STRIDE
    x16 = kcv.reshape(batch, nchunk, CMP_STRIDE, 2 * NSA_KV_HEADS, HEAD_DIM).transpose(0, 3, 1, 2, 4)
    x16 = x16.reshape(batch, 2 * NSA_KV_HEADS, nchunk, CMP_STRIDE * HEAD_DIM)
    half = CMP_STRIDE * HEAD_DIM
    pos = jnp.stack([pos_k.reshape(2, 1, half), pos_v.reshape(2, 1, half)])
    w1 = jnp.stack([w1_k.reshape(2, half, CMP_HIDDEN), w1_v.reshape(2, half, CMP_HIDDEN)]).astype(MXU_DTYPE)
    w2 = jnp.stack([w2_k, w2_v]).astype(MXU_DTYPE)
    kvc = _compress(x16, pos, w1, w2)

    def heads(z, nh):
        return z.reshape(batch, t, nh, HEAD_DIM).transpose(0, 2, 1, 3)

    y_nsa = _nsa(heads(qn, NSA_HEADS), heads(qr, NSA_HEADS), kvc, heads(ksr, NSA_KV_HEADS),
                 heads(vs, NSA_KV_HEADS), heads(kwr, NSA_KV_HEADS), heads(vw, NSA_KV_HEADS),
                 gates.reshape(batch, t, NSA_KV_HEADS * LANES))
    w_out = w_out.astype(MXU_DTYPE)
    return _out_proj(h, y_lru, y_nsa.reshape(n, NSA_HEADS * HEAD_DIM), w_out[:LRU_WIDTH], w_out[LRU_WIDTH:])


def _rope_tables(positions):
    inv = ROPE_THETA ** (-jnp.arange(0, 2 * ROT_HALF, 2, dtype=F32) / (2 * ROT_HALF))
    ang = positions.reshape(-1).astype(F32)[:, None] * inv
    cos, sin = jnp.cos(ang), jnp.sin(ang)
    rest = HEAD_DIM - 2 * ROT_HALF
    cos_h = jnp.concatenate([cos, cos, jnp.ones((cos.shape[0], rest), F32)], axis=1)
    sin_h = jnp.concatenate([-sin, sin, jnp.zeros((cos.shape[0], rest), F32)], axis=1)
    reps = LANES // HEAD_DIM
    return jnp.tile(cos_h, (1, reps)), jnp.tile(sin_h, (1, reps))


def kernel(x, mem, positions, norm_mix, norm_xattn, norm_mem, norm_ffn, norm_final, even_w_in, even_w_out, lru_conv_w, lru_conv_b, lru_w_r, lru_b_r, lru_w_i, lru_b_i, lru_lambda, nsa_cmp_pos_k, nsa_cmp_w1_k, nsa_cmp_w2_k, nsa_cmp_pos_v, nsa_cmp_w1_v, nsa_cmp_w2_v, odd_w_in, odd_conv_w, odd_w_out, xa_wq, xa_wk, xa_wv, xa_wo, moe_w_group, moe_b_group, moe_w_expert, moe_b_expert, moe_w_gate, moe_w_up, moe_w_down):
    batch, t, d = x.shape
    n = batch * t
    depth = norm_mix.shape[0]
    cos_t, sin_t = _rope_tables(positions)
    h = x.reshape(n, d)
    mem2 = mem.reshape(-1, d)
    for layer in range(depth):
        g_mix = norm_mix[layer][None, :]
        if layer % 2 == 0:
            e = layer // 2
            h = _even_mixer(h, g_mix, batch, cos_t, sin_t, even_w_in[e], even_w_out[e], lru_conv_w[e],
                            lru_conv_b[e], lru_w_r[e], lru_b_r[e], lru_w_i[e], lru_b_i[e], lru_lambda[e],
                            nsa_cmp_pos_k[e], nsa_cmp_w1_k[e], nsa_cmp_w2_k[e],
                            nsa_cmp_pos_v[e], nsa_cmp_w1_v[e], nsa_cmp_w2_v[e])
        else:
            o = layer // 2
            h = _short_conv(h, g_mix, odd_w_in[o].astype(MXU_DTYPE), odd_conv_w[o],
                            odd_w_out[o].astype(MXU_DTYPE), batch)
        wkv = jnp.concatenate([xa_wk[layer], xa_wv[layer]], axis=1).astype(MXU_DTYPE)
        kv = _mem_kv(mem2, norm_mem[layer][None, :], wkv)
        h = _xattn(h, norm_xattn[layer][None, :], xa_wq[layer].astype(MXU_DTYPE), kv,
                   xa_wo[layer].astype(MXU_DTYPE), batch)
        h = _moe(h, norm_ffn[layer][None, :], moe_w_group[layer], moe_b_group[layer], moe_w_expert[layer],
                 moe_b_expert[layer], moe_w_gate[layer], moe_w_up[layer], moe_w_down[layer])
    return _final_norm(h, norm_final[None, :]).reshape(batch, t, d)
```

```python
import functools

import jax
import jax.numpy as jnp
import numpy as np
from jax import lax
from jax.experimental import pallas as pl
from jax.experimental.pallas import tpu as pltpu

F32 = jnp.float32
MXU_DTYPE = jnp.bfloat16

D_MODEL = 1024
LRU_WIDTH = 512
LRU_CONV = 4
LRU_C = 8.0
NSA_HEADS = 8
NSA_KV_HEADS = 2
HEADS_PER_GROUP = NSA_HEADS // NSA_KV_HEADS
HEAD_DIM = 64
CMP_STRIDE = 16
CMP_LEN = 32
CMP_HIDDEN = 128
SEL_BLOCK = 64
SEL_TOP = 16
WINDOW = 512
ROT_HALF = 8
ROPE_THETA = 500000.0
SC_CONV = 3
XA_HEADS = 4
XA_HEAD_DIM = 256
N_GROUPS = 4
EXPERTS_PER_GROUP = 4
N_EXPERTS = 16
D_EXPERT = 512
EPS = 1e-6
NEG = -1e30
FORCE = 1e9

LANES = 128
SUBLANES = 8
VMEM_LIMIT = 56 * 1024 * 1024

C_XL, C_GL, C_Q, C_KCV, C_KS, C_VS, C_KW, C_VW, C_GATE, C_END = (
    0, 512, 1024, 1536, 1792, 1920, 2048, 2176, 2304, 2560)

ROW_TILE = 512
NSA_TQ = 128
NSA_TK = 256
EXPERT_TILE = 256
DMA_UNROLL = 8


def _cparams(n_axes):
    return pltpu.CompilerParams(dimension_semantics=("arbitrary",) * n_axes,
                                vmem_limit_bytes=VMEM_LIMIT)


def _mm(a, b):
    return jnp.dot(a, b, preferred_element_type=F32)


def _mm_nt(a, b):
    return lax.dot_general(a, b, (((1,), (1,)), ((), ())), preferred_element_type=F32)


def _rms(x, g):
    return x * lax.rsqrt(jnp.mean(x * x, axis=-1, keepdims=True) + EPS) * g


def _shift_rows(xx, s, rows):
    if s == 0:
        return xx[SUBLANES:SUBLANES + rows]
    return pltpu.roll(xx, s, 0)[SUBLANES:SUBLANES + rows]


def _rope(x, cos_t, sin_t):
    width = x.shape[-1]
    lane = lax.broadcasted_iota(jnp.int32, x.shape, 1) % HEAD_DIM
    partner = jnp.where(lane < ROT_HALF, pltpu.roll(x, width - ROT_HALF, 1), pltpu.roll(x, ROT_HALF, 1))
    return x * cos_t + partner * sin_t


def _even_proj_kernel(h_ref, g_ref, w_ref, cos_ref, sin_ref,
                      xl_ref, gg_ref, qn_ref, qr_ref, kcv_ref, ksr_ref, vs_ref, kwr_ref, vw_ref,
                      gate_ref):
    xn = _rms(h_ref[...], g_ref[...]).astype(MXU_DTYPE)

    def proj(c0, c1):
        return _mm(xn, w_ref[:, c0:c1])

    cos_t = cos_ref[...]
    sin_t = sin_ref[...]
    xl_ref[...] = proj(C_XL, C_GL)
    gg_ref[...] = jax.nn.gelu(proj(C_GL, C_Q))
    q = proj(C_Q, C_KCV) * (HEAD_DIM ** -0.5)
    reps = (C_KCV - C_Q) // LANES
    cos_q = jnp.concatenate([cos_t] * reps, axis=1)
    sin_q = jnp.concatenate([sin_t] * reps, axis=1)
    qn_ref[...] = q.astype(qn_ref.dtype)
    qr_ref[...] = _rope(q, cos_q, sin_q).astype(qr_ref.dtype)
    kcv_ref[...] = proj(C_KCV, C_KS)
    ksr_ref[...] = _rope(proj(C_KS, C_VS), cos_t, sin_t).astype(ksr_ref.dtype)
    vs_ref[...] = proj(C_VS, C_KW).astype(vs_ref.dtype)
    kwr_ref[...] = _rope(proj(C_KW, C_VW), cos_t, sin_t).astype(kwr_ref.dtype)
    vw_ref[...] = proj(C_VW, C_GATE).astype(vw_ref.dtype)
    gate_ref[...] = jax.nn.sigmoid(proj(C_GATE, C_END))


def _even_proj(h, g, w, cos_t, sin_t):
    n = h.shape[0]
    tm = min(ROW_TILE, n)
    row = lambda c: pl.BlockSpec((tm, c), lambda i: (i, 0))
    full = lambda a: pl.BlockSpec(a.shape, lambda i: (0,) * a.ndim)
    widths = [(512, F32), (512, F32), (512, MXU_DTYPE), (512, MXU_DTYPE), (256, F32),
              (128, MXU_DTYPE), (128, MXU_DTYPE), (128, MXU_DTYPE), (128, MXU_DTYPE), (256, F32)]
    return pl.pallas_call(
        _even_proj_kernel,
        grid=(n // tm,),
        in_specs=[row(D_MODEL), full(g), full(w), row(LANES), row(LANES)],
        out_specs=[row(c) for c, _ in widths],
        out_shape=[jax.ShapeDtypeStruct((n, c), dt) for c, dt in widths],
        compiler_params=_cparams(1),
        name="even_proj",
    )(h, g, w, cos_t, sin_t)


def _lru_kernel(xl_ref, gg_ref, cw_ref, cb_ref, wr_ref, wi_ref, br_ref, bi_ref, sp_ref,
                y_ref, tail_ref, h_ref, a_scr, u_scr):
    tt = xl_ref.shape[0]

    @pl.when(pl.program_id(1) == 0)
    def _():
        tail_ref[...] = jnp.zeros_like(tail_ref)
        h_ref[...] = jnp.zeros_like(h_ref)

    x = xl_ref[...]
    xx = jnp.concatenate([tail_ref[...], x], axis=0)
    tail_ref[...] = x[tt - SUBLANES:tt]
    xc = cb_ref[...] + sum(cw_ref[k:k + 1, :] * _shift_rows(xx, LRU_CONV - 1 - k, tt)
                           for k in range(LRU_CONV))
    xcb = xc.astype(MXU_DTYPE)
    half = LRU_WIDTH // 2
    r_lin = jnp.concatenate([_mm(xcb[:, j * half:(j + 1) * half], wr_ref[j]) for j in range(2)], axis=1)
    i_lin = jnp.concatenate([_mm(xcb[:, j * half:(j + 1) * half], wi_ref[j]) for j in range(2)], axis=1)
    r = jax.nn.sigmoid(r_lin + br_ref[...])
    i = jax.nn.sigmoid(i_lin + bi_ref[...])
    log_a = -LRU_C * r * sp_ref[...]
    a = jnp.exp(log_a)
    u = jnp.sqrt(jnp.tanh(-log_a) * (1.0 + a * a)) * (i * xc)

    r8 = lax.broadcasted_iota(jnp.int32, a.shape, 0) % SUBLANES
    for s in (1, 2, 4):
        keep = r8 >= s
        u = jnp.where(keep, a * pltpu.roll(u, s, 0) + u, u)
        a = jnp.where(keep, a * pltpu.roll(a, s, 0), a)
    a_scr[...] = a
    u_scr[...] = u

    def body(gidx, h):
        r0 = pl.multiple_of(gidx * SUBLANES, SUBLANES)
        out = a_scr[pl.ds(r0, SUBLANES), :] * h + u_scr[pl.ds(r0, SUBLANES), :]
        u_scr[pl.ds(r0, SUBLANES), :] = out
        return out[SUBLANES - 1:SUBLANES, :]

    h_ref[...] = lax.fori_loop(0, tt // SUBLANES, body, h_ref[...])
    y_ref[...] = (u_scr[...] * gg_ref[...]).astype(y_ref.dtype)


def _lru(xl, gg, cw, cb, wr, wi, br, bi, sp, batch):
    n = xl.shape[0]
    t = n // batch
    tt = min(ROW_TILE, t)
    nt = t // tt
    row = pl.BlockSpec((tt, LRU_WIDTH), lambda b, i: (b * nt + i, 0))
    full = lambda a: pl.BlockSpec(a.shape, lambda b, i: (0,) * a.ndim)
    return pl.pallas_call(
        _lru_kernel,
        grid=(batch, nt),
        in_specs=[row, row] + [full(a) for a in (cw, cb, wr, wi, br, bi, sp)],
        out_specs=row,
        out_shape=jax.ShapeDtypeStruct((n, LRU_WIDTH), MXU_DTYPE),
        scratch_shapes=[pltpu.VMEM((SUBLANES, LRU_WIDTH), F32), pltpu.VMEM((1, LRU_WIDTH), F32),
                        pltpu.VMEM((tt, LRU_WIDTH), F32), pltpu.VMEM((tt, LRU_WIDTH), F32)],
        compiler_params=_cparams(2),
        name="lru",
    )(xl, gg, cw, cb, wr, wi, br, bi, sp)


def _compress_kernel(x_ref, pos_ref, w1_ref, w2_ref, o_ref):
    x = x_ref[0, 0]
    nchunk = x.shape[0]
    lo = (x + pos_ref[0, 0]).astype(MXU_DTYPE)
    hi = (x + pos_ref[0, 1]).astype(MXU_DTYPE)
    p_lo = _mm(lo, w1_ref[0, 0])
    p_hi = _mm(hi, w1_ref[0, 1])
    a = p_lo + pltpu.roll(p_hi, nchunk - 1, 0)
    o_ref[0, 0] = _mm(jax.nn.gelu(a).astype(MXU_DTYPE), w2_ref[0]).astype(o_ref.dtype)


def _compress(x16, pos, w1, w2):
    b, _, nchunk, width = x16.shape
    return pl.pallas_call(
        _compress_kernel,
        grid=(b, 2 * NSA_KV_HEADS),
        in_specs=[pl.BlockSpec((1, 1, nchunk, width), lambda i, j: (i, j, 0, 0)),
                  pl.BlockSpec((1, 2, 1, width), lambda i, j: (j // NSA_KV_HEADS, 0, 0, 0)),
                  pl.BlockSpec((1, 2, width, CMP_HIDDEN), lambda i, j: (j // NSA_KV_HEADS, 0, 0, 0)),
                  pl.BlockSpec((1, CMP_HIDDEN, HEAD_DIM), lambda i, j: (j // NSA_KV_HEADS, 0, 0))],
        out_specs=pl.BlockSpec((1, 1, nchunk, HEAD_DIM), lambda i, j: (i, j, 0, 0)),
        out_shape=jax.ShapeDtypeStruct((b, 2 * NSA_KV_HEADS, nchunk, HEAD_DIM), MXU_DTYPE),
        compiler_params=_cparams(2),
        name="compress",
    )(x16, pos, w1, w2)


def _nsa_kernel(qn_ref, qr_ref, kc_ref, vct_ref, ks_ref, vst_ref, kw_ref, vwt_ref, gate_ref,
                y_ref, chosen_scr, acc_scr):
    rows = qn_ref.shape[4]
    tq = rows // HEADS_PER_GROUP
    t = ks_ref.shape[2]
    nc = kc_ref.shape[2]
    nsel = t // SEL_BLOCK
    n_top = min(SEL_TOP, nsel)
    tk = NSA_TK
    blocks_per_tile = tk // SEL_BLOCK
    tiles_per_group = SUBLANES // blocks_per_tile
    t0 = pl.program_id(2) * tq
    qn = qn_ref[0, 0, 0]
    qr = qr_ref[0, 0, 0]

    s = _mm(kc_ref[0, 0], qn)
    tq_row = t0 + lax.broadcasted_iota(jnp.int32, (nc, rows), 1) % tq
    cmp_end = lax.broadcasted_iota(jnp.int32, (nc, rows), 0) * CMP_STRIDE + (CMP_LEN - 1)
    valid = cmp_end <= tq_row
    s = jnp.where(valid, s, NEG)
    e = jnp.where(valid, jnp.exp(s - jnp.max(s, axis=0, keepdims=True)), 0.0)
    den = jnp.sum(e, axis=0, keepdims=True)
    p = e * (1.0 / jnp.where(den > 0.0, den, 1.0))
    o_cmp = _mm(vct_ref[0, 0], p.astype(MXU_DTYPE))

    psum = p[:, 0:tq]
    for hh in range(1, HEADS_PER_GROUP):
        psum = psum + p[:, hh * tq:(hh + 1) * tq]
    p_hi = psum.astype(MXU_DTYPE)
    p_lo = (psum - p_hi.astype(F32)).astype(MXU_DTYPE)
    cj = lax.broadcasted_iota(jnp.int32, (nsel, nc), 0)
    cn = lax.broadcasted_iota(jnp.int32, (nsel, nc), 1)
    ratio = SEL_BLOCK // CMP_STRIDE
    cover = jnp.where((cn >= ratio * cj - (CMP_LEN // CMP_STRIDE - 1)) & (cn <= ratio * cj + ratio - 1)
                      & (cn < nc - 1), 1.0, 0.0).astype(MXU_DTYPE)
    imp = _mm(cover, p_hi) + _mm(cover, p_lo)

    blk = lax.broadcasted_iota(jnp.int32, (nsel, tq), 0)
    tq_col = t0 + lax.broadcasted_iota(jnp.int32, (nsel, tq), 1)
    cur = tq_col // SEL_BLOCK
    forced = (blk == 0) | (blk == cur) | (blk == cur - 1)
    causal = blk * SEL_BLOCK <= tq_col
    score0 = jnp.where(causal, jnp.where(forced, FORCE, imp), NEG)
    blk_f = blk.astype(F32)

    def pick_one(_, carry):
        score, chosen = carry
        best = jnp.max(score, axis=0, keepdims=True)
        first = jnp.min(jnp.where(score == best, blk_f, float(nsel)), axis=0, keepdims=True)
        hit = blk_f == first
        return jnp.where(hit, -jnp.inf, score), jnp.where(hit, 1.0, chosen)

    _, chosen = lax.fori_loop(0, n_top, pick_one, (score0, jnp.zeros((nsel, tq), F32)))
    chosen_scr[...] = chosen

    def sweep(k_ref, vt_ref, kt_lo, kt_hi, allowed_fn):
        acc_scr[...] = jnp.zeros_like(acc_scr)

        def body(kt, carry):
            m_old, l_old = carry
            k0 = pl.multiple_of(kt * tk, tk)
            kpos = k0 + lax.broadcasted_iota(jnp.int32, (tk, tq), 0)
            tqm = t0 + lax.broadcasted_iota(jnp.int32, (tk, tq), 1)
            bias = jnp.where(allowed_fn(kt, kpos, tqm), 0.0, NEG)
            bias = jnp.concatenate([bias] * HEADS_PER_GROUP, axis=1)
            sc = _mm(k_ref[0, 0, pl.ds(k0, tk), :], qr) + bias
            m_new = jnp.maximum(m_old, jnp.max(sc, axis=0, keepdims=True))
            alpha = jnp.exp(m_old - m_new)
            pe = jnp.exp(sc - m_new)
            l_new = alpha * l_old + jnp.sum(pe, axis=0, keepdims=True)
            acc_scr[...] = alpha * acc_scr[...] + _mm(vt_ref[0, 0, :, pl.ds(k0, tk)], pe.astype(MXU_DTYPE))
            return m_new, l_new

        init = (jnp.full((1, rows), NEG, F32), jnp.zeros((1, rows), F32))
        _, l_fin = lax.fori_loop(kt_lo, kt_hi, body, init)
        return acc_scr[...] * (1.0 / l_fin)

    kt_hi = (t0 + tq - 1) // tk + 1

    def sel_allowed(kt, kpos, tqm):
        grp = pl.multiple_of(kt // tiles_per_group * SUBLANES, SUBLANES)
        first_blk = kt % tiles_per_group * blocks_per_tile
        ei = lax.broadcasted_iota(jnp.int32, (tk, SUBLANES), 0) // SEL_BLOCK + first_blk
        ej = lax.broadcasted_iota(jnp.int32, (tk, SUBLANES), 1)
        expand = jnp.where(ei == ej, 1.0, 0.0)
        return (_mm(expand, chosen_scr[pl.ds(grp, SUBLANES), :]) > 0.5) & (kpos <= tqm)

    def win_allowed(kt, kpos, tqm):
        return (kpos <= tqm) & (kpos > tqm - WINDOW)

    o_sel = sweep(ks_ref, vst_ref, 0, kt_hi, sel_allowed)
    o_win = sweep(kw_ref, vwt_ref, jnp.maximum(t0 - (WINDOW - 1), 0) // tk, kt_hi, win_allowed)

    gate = gate_ref[0, 0, 0]
    y_ref[0, 0, 0] = (gate[0:1] * o_cmp + gate[1:2] * o_sel + gate[2:3] * o_win).astype(y_ref.dtype)


def _nsa(qn_t, qr_t, kvc, vct, ks, vst, kw, vwt, gates_t):
    b, _, nq, _, rows = qn_t.shape
    t = ks.shape[2]
    nc = kvc.shape[2]
    assert t % (SEL_BLOCK * SUBLANES) == 0 and t % NSA_TK == 0
    q_spec = pl.BlockSpec((1, 1, 1, HEAD_DIM, rows), lambda i, g, j: (i, g, j, 0, 0))
    k_spec = pl.BlockSpec((1, 1, t, HEAD_DIM), lambda i, g, j: (i, g, 0, 0))
    vt_spec = pl.BlockSpec((1, 1, HEAD_DIM, t), lambda i, g, j: (i, g, 0, 0))
    return pl.pallas_call(
        _nsa_kernel,
        grid=(b, NSA_KV_HEADS, nq),
        in_specs=[q_spec, q_spec,
                  pl.BlockSpec((1, 1, nc, HEAD_DIM), lambda i, g, j: (i, g, 0, 0)),
                  pl.BlockSpec((1, 1, HEAD_DIM, nc), lambda i, g, j: (i, g, 0, 0)),
                  k_spec, vt_spec, k_spec, vt_spec,
                  pl.BlockSpec((1, 1, 1, SUBLANES, rows), lambda i, g, j: (i, g, j, 0, 0))],
        out_specs=q_spec,
        out_shape=jax.ShapeDtypeStruct(qn_t.shape, MXU_DTYPE),
        scratch_shapes=[pltpu.VMEM((t // SEL_BLOCK, rows // HEADS_PER_GROUP), F32),
                        pltpu.VMEM((HEAD_DIM, rows), F32)],
        compiler_params=_cparams(3),
        name="nsa",
    )(qn_t, qr_t, kvc, vct, ks, vst, kw, vwt, gates_t)


def _out_proj_kernel(h_ref, a_ref, b_ref, wa_ref, wb_ref, o_ref):
    o_ref[...] = h_ref[...] + _mm(a_ref[...], wa_ref[...]) + _mm(b_ref[...], wb_ref[...])


def _out_proj(h, a, b, wa, wb):
    n = h.shape[0]
    tm = min(ROW_TILE, n)
    full = lambda x: pl.BlockSpec(x.shape, lambda i: (0,) * x.ndim)
    return pl.pallas_call(
        _out_proj_kernel,
        grid=(n // tm,),
        in_specs=[pl.BlockSpec((tm, D_MODEL), lambda i: (i, 0)),
                  pl.BlockSpec((tm, a.shape[1]), lambda i: (i, 0)),
                  pl.BlockSpec((tm, b.shape[1]), lambda i: (i, 0)), full(wa), full(wb)],
        out_specs=pl.BlockSpec((tm, D_MODEL), lambda i: (i, 0)),
        out_shape=jax.ShapeDtypeStruct((n, D_MODEL), F32),
        compiler_params=_cparams(1),
        name="out_proj",
    )(h, a, b, wa, wb)


def _short_conv_kernel(h_ref, g_ref, win_ref, cw_ref, wout_ref, o_ref, tail_ref):
    tt = h_ref.shape[0]

    @pl.when(pl.program_id(1) == 0)
    def _():
        tail_ref[...] = jnp.zeros_like(tail_ref)

    h = h_ref[...]
    xn = _rms(h, g_ref[...]).astype(MXU_DTYPE)
    b_g = _mm(xn, win_ref[:, 0:D_MODEL])
    cv = _mm(xn, win_ref[:, D_MODEL:2 * D_MODEL]) * _mm(xn, win_ref[:, 2 * D_MODEL:3 * D_MODEL])
    xx = jnp.concatenate([tail_ref[...], cv], axis=0)
    tail_ref[...] = cv[tt - SUBLANES:tt]
    conv = sum(cw_ref[k:k + 1, :] * _shift_rows(xx, SC_CONV - 1 - k, tt) for k in range(SC_CONV))
    o_ref[...] = h + _mm((b_g * conv).astype(MXU_DTYPE), wout_ref[...])


def _short_conv(h, g, w_in, cw, w_out, batch):
    n = h.shape[0]
    t = n // batch
    tt = min(ROW_TILE, t)
    nt = t // tt
    row = pl.BlockSpec((tt, D_MODEL), lambda b, i: (b * nt + i, 0))
    full = lambda a: pl.BlockSpec(a.shape, lambda b, i: (0,) * a.ndim)
    return pl.pallas_call(
        _short_conv_kernel,
        grid=(batch, nt),
        in_specs=[row, full(g), full(w_in), full(cw), full(w_out)],
        out_specs=row,
        out_shape=jax.ShapeDtypeStruct((n, D_MODEL), F32),
        scratch_shapes=[pltpu.VMEM((SUBLANES, D_MODEL), F32)],
        compiler_params=_cparams(2),
        name="short_conv",
    )(h, g, w_in, cw, w_out)


def _mem_kv_kernel(m_ref, g_ref, w_ref, o_ref):
    o_ref[...] = _mm(_rms(m_ref[...], g_ref[...]).astype(MXU_DTYPE), w_ref[...]).astype(o_ref.dtype)


def _mem_kv(mem, g, wkv):
    n = mem.shape[0]
    tm = min(ROW_TILE, n)
    tn = 1024
    return pl.pallas_call(
        _mem_kv_kernel,
        grid=(n // tm, wkv.shape[1] // tn),
        in_specs=[pl.BlockSpec((tm, D_MODEL), lambda i, j: (i, 0)),
                  pl.BlockSpec(g.shape, lambda i, j: (0, 0)),
                  pl.BlockSpec((D_MODEL, tn), lambda i, j: (0, j))],
        out_specs=pl.BlockSpec((tm, tn), lambda i, j: (i, j)),
        out_shape=jax.ShapeDtypeStruct((n, wkv.shape[1]), MXU_DTYPE),
        compiler_params=_cparams(2),
        name="mem_kv",
    )(mem, g, wkv)


def _xattn_kernel(h_ref, g_ref, wq_ref, kv_ref, wo_ref, o_ref):
    h = h_ref[...]
    xn = _rms(h, g_ref[...]).astype(MXU_DTYPE)
    q = (_mm(xn, wq_ref[...]) * (XA_HEAD_DIM ** -0.5)).astype(MXU_DTYPE)
    width = XA_HEADS * XA_HEAD_DIM
    outs = []
    for hd in range(XA_HEADS):
        sl = slice(hd * XA_HEAD_DIM, (hd + 1) * XA_HEAD_DIM)
        s = _mm_nt(q[:, sl], kv_ref[:, sl])
        e = jnp.exp(s - jnp.max(s, axis=-1, keepdims=True))
        p = e / jnp.sum(e, axis=-1, keepdims=True)
        outs.append(_mm(p.astype(MXU_DTYPE), kv_ref[:, width + hd * XA_HEAD_DIM:width + (hd + 1) * XA_HEAD_DIM]))
    o = jnp.concatenate(outs, axis=1).astype(MXU_DTYPE)
    o_ref[...] = h + _mm(o, wo_ref[...])


def _xattn(h, g, wq, kv, wo, batch):
    n = h.shape[0]
    t = n // batch
    tm = min(ROW_TILE, t)
    nt = t // tm
    mlen = kv.shape[0] // batch
    full = lambda a: pl.BlockSpec(a.shape, lambda b, i: (0,) * a.ndim)
    row = pl.BlockSpec((tm, D_MODEL), lambda b, i: (b * nt + i, 0))
    return pl.pallas_call(
        _xattn_kernel,
        grid=(batch, nt),
        in_specs=[row, full(g), full(wq), pl.BlockSpec((mlen, kv.shape[1]), lambda b, i: (b, 0)), full(wo)],
        out_specs=row,
        out_shape=jax.ShapeDtypeStruct((n, D_MODEL), F32),
        compiler_params=_cparams(2),
        name="xattn",
    )(h, g, wq, kv, wo)


def _router_kernel(h_ref, g_ref, whi_ref, wlo_ref, b_ref, ri_ref, rw_ref, cnt_ref, carry_ref):
    tm = h_ref.shape[0]

    @pl.when(pl.program_id(0) == 0)
    def _():
        carry_ref[...] = jnp.zeros_like(carry_ref)

    xn = _rms(h_ref[...], g_ref[...])
    x_hi = xn.astype(MXU_DTYPE)
    x_lo = (xn - x_hi.astype(F32)).astype(MXU_DTYPE)
    logits = _mm(x_hi, whi_ref[...]) + (_mm(x_lo, whi_ref[...]) + _mm(x_hi, wlo_ref[...])) + b_ref[...]

    lane = lax.broadcasted_iota(jnp.int32, logits.shape, 1)
    lane_f = lane.astype(F32)
    none = float(LANES)
    is_g = lane < N_GROUPS
    g_max = jnp.max(jnp.where(is_g, logits, -jnp.inf), axis=-1, keepdims=True)
    g_sum = jnp.sum(jnp.where(is_g, jnp.exp(logits - g_max), 0.0), axis=-1, keepdims=True)
    g_top = 1.0 / g_sum
    g_idx = jnp.min(jnp.where(is_g & (logits == g_max), lane_f, none), axis=-1, keepdims=True)
    first = N_GROUPS + EXPERTS_PER_GROUP * g_idx
    in_g = (lane_f >= first) & (lane_f < first + EXPERTS_PER_GROUP)
    e1 = jnp.max(jnp.where(in_g, logits, -jnp.inf), axis=-1, keepdims=True)
    i1 = jnp.min(jnp.where(in_g & (logits == e1), lane_f, none), axis=-1, keepdims=True)
    rest = in_g & (lane_f != i1)
    e2 = jnp.max(jnp.where(rest, logits, -jnp.inf), axis=-1, keepdims=True)
    i2 = jnp.min(jnp.where(rest & (logits == e2), lane_f, none), axis=-1, keepdims=True)
    ratio = jnp.exp(e2 - e1)
    w1 = g_top / (1.0 + ratio)
    w2 = g_top * ratio / (1.0 + ratio)

    onehot = jnp.where((lane_f == i1) | (lane_f == i2), 1.0, 0.0)
    tri = jnp.where(lax.broadcasted_iota(jnp.int32, (tm, tm), 0) >= lax.broadcasted_iota(jnp.int32, (tm, tm), 1),
                    1.0, 0.0).astype(MXU_DTYPE)
    incl = _mm(tri, onehot.astype(MXU_DTYPE))
    before = incl - onehot + carry_ref[...]
    carry_ref[...] = carry_ref[...] + incl[tm - 1:tm, :]
    rank1 = jnp.sum(jnp.where(lane_f == i1, before, 0.0), axis=-1, keepdims=True)
    rank2 = jnp.sum(jnp.where(lane_f == i2, before, 0.0), axis=-1, keepdims=True)

    ri = jnp.where(lane == 0, i1 - N_GROUPS,
                   jnp.where(lane == 1, i2 - N_GROUPS, jnp.where(lane == 2, rank1, jnp.where(lane == 3, rank2, 0.0))))
    ri_ref[...] = ri.astype(jnp.int32)
    rw_ref[...] = jnp.where(lane == 0, w1, jnp.where(lane == 1, w2, 0.0))
    cnt_ref[...] = jnp.broadcast_to(carry_ref[...], cnt_ref.shape).astype(jnp.int32)


def _router(h, g, w_hi, w_lo, bias):
    n = h.shape[0]
    tm = min(ROW_TILE, n)
    full = lambda a: pl.BlockSpec(a.shape, lambda i: (0,) * a.ndim)
    return pl.pallas_call(
        _router_kernel,
        grid=(n // tm,),
        in_specs=[pl.BlockSpec((tm, D_MODEL), lambda i: (i, 0)), full(g), full(w_hi), full(w_lo), full(bias)],
        out_specs=[pl.BlockSpec((tm, LANES), lambda i: (i, 0)), pl.BlockSpec((tm, LANES), lambda i: (i, 0)),
                   pl.BlockSpec((SUBLANES, LANES), lambda i: (i, 0))],
        out_shape=[jax.ShapeDtypeStruct((n, LANES), jnp.int32), jax.ShapeDtypeStruct((n, LANES), F32),
                   jax.ShapeDtypeStruct((n // tm * SUBLANES, LANES), jnp.int32)],
        scratch_shapes=[pltpu.VMEM((1, LANES), F32)],
        compiler_params=_cparams(1),
        name="router",
    )(h, g, w_hi, w_lo, bias)


def _row_copy(src_ref, src_row, dst_ref, dst_row, sem):
    return pltpu.make_async_copy(src_ref.at[pl.ds(src_row, 1)], dst_ref.at[pl.ds(dst_row, 1)], sem)


def _dispatch_kernel(pos_ref, h_ref, g_ref, xs_in_ref, xs_ref, xn_scr, sem):
    del xs_in_ref
    tm = h_ref.shape[0]
    base = pl.program_id(0) * (2 * tm)
    xn_scr[...] = _rms(h_ref[...], g_ref[...])

    def issue(r, _):
        _row_copy(xn_scr, r, xs_ref, pos_ref[base + 2 * r], sem).start()
        _row_copy(xn_scr, r, xs_ref, pos_ref[base + 2 * r + 1], sem).start()
        return 0

    lax.fori_loop(0, tm, issue, 0, unroll=DMA_UNROLL)
    for _ in range(2):
        pltpu.make_async_copy(xn_scr, xs_ref.at[pl.ds(0, tm)], sem).wait()


def _dispatch(pos, h, g, xs_zero):
    n = h.shape[0]
    tm = min(ROW_TILE, n)
    grid_spec = pltpu.PrefetchScalarGridSpec(
        num_scalar_prefetch=1,
        grid=(n // tm,),
        in_specs=[pl.BlockSpec((tm, D_MODEL), lambda i, pos: (i, 0)),
                  pl.BlockSpec(g.shape, lambda i, pos: (0, 0)),
                  pl.BlockSpec(memory_space=pl.ANY)],
        out_specs=pl.BlockSpec(memory_space=pl.ANY),
        scratch_shapes=[pltpu.VMEM((tm, D_MODEL), F32), pltpu.SemaphoreType.DMA(())],
    )
    return pl.pallas_call(
        _dispatch_kernel,
        grid_spec=grid_spec,
        out_shape=jax.ShapeDtypeStruct(xs_zero.shape, F32),
        input_output_aliases={3: 0},
        compiler_params=_cparams(1),
        name="dispatch",
    )(pos, h, g, xs_zero)


def _expert_kernel(te_ref, ta_ref, x_ref, wg_ref, wu_ref, wd_ref, y_ref):
    j = pl.program_id(0)

    @pl.when(ta_ref[j] == 1)
    def _():
        x = x_ref[...].astype(MXU_DTYPE)
        hid = jax.nn.silu(_mm(x, wg_ref[0])) * _mm(x, wu_ref[0])
        y_ref[...] = _mm(hid.astype(MXU_DTYPE), wd_ref[0])

    @pl.when(ta_ref[j] == 0)
    def _():
        y_ref[...] = jnp.zeros_like(y_ref)


def _experts(tile_expert, tile_active, xs, wg, wu, wd):
    p = xs.shape[0]
    grid_spec = pltpu.PrefetchScalarGridSpec(
        num_scalar_prefetch=2,
        grid=(p // EXPERT_TILE,),
        in_specs=[pl.BlockSpec((EXPERT_TILE, D_MODEL), lambda j, te, ta: (j, 0)),
                  pl.BlockSpec((1, D_MODEL, D_EXPERT), lambda j, te, ta: (te[j], 0, 0)),
                  pl.BlockSpec((1, D_MODEL, D_EXPERT), lambda j, te, ta: (te[j], 0, 0)),
                  pl.BlockSpec((1, D_EXPERT, D_MODEL), lambda j, te, ta: (te[j], 0, 0))],
        out_specs=pl.BlockSpec((EXPERT_TILE, D_MODEL), lambda j, te, ta: (j, 0)),
    )
    return pl.pallas_call(
        _expert_kernel,
        grid_spec=grid_spec,
        out_shape=jax.ShapeDtypeStruct((p, D_MODEL), F32),
        compiler_params=_cparams(1),
        name="experts",
    )(tile_expert, tile_active, xs, wg, wu, wd)


def _combine_kernel(pos_ref, h_ref, rw_ref, ys_ref, o_ref, y1_scr, y2_scr, sem):
    tm = h_ref.shape[0]
    base = pl.program_id(0) * (2 * tm)

    def issue(r, _):
        _row_copy(ys_ref, pos_ref[base + 2 * r], y1_scr, r, sem).start()
        _row_copy(ys_ref, pos_ref[base + 2 * r + 1], y2_scr, r, sem).start()
        return 0

    lax.fori_loop(0, tm, issue, 0, unroll=DMA_UNROLL)
    for buf in (y1_scr, y2_scr):
        pltpu.make_async_copy(ys_ref.at[pl.ds(0, tm)], buf, sem).wait()
    rw = rw_ref[...]
    o_ref[...] = h_ref[...] + (rw[:, 0:1] * y1_scr[...] + rw[:, 1:2] * y2_scr[...])


def _combine(pos, h, rw, ys):
    n = h.shape[0]
    tm = min(ROW_TILE, n)
    grid_spec = pltpu.PrefetchScalarGridSpec(
        num_scalar_prefetch=1,
        grid=(n // tm,),
        in_specs=[pl.BlockSpec((tm, D_MODEL), lambda i, pos: (i, 0)),
                  pl.BlockSpec((tm, LANES), lambda i, pos: (i, 0)),
                  pl.BlockSpec(memory_space=pl.ANY)],
        out_specs=pl.BlockSpec((tm, D_MODEL), lambda i, pos: (i, 0)),
        scratch_shapes=[pltpu.VMEM((tm, D_MODEL), F32), pltpu.VMEM((tm, D_MODEL), F32),
                        pltpu.SemaphoreType.DMA(())],
    )
    return pl.pallas_call(
        _combine_kernel,
        grid_spec=grid_spec,
        out_shape=jax.ShapeDtypeStruct((n, D_MODEL), F32),
        compiler_params=_cparams(1),
        name="combine",
    )(pos, h, rw, ys)


def _final_norm_kernel(h_ref, g_ref, o_ref):
    o_ref[...] = _rms(h_ref[...], g_ref[...])


def _final_norm(h, g):
    n = h.shape[0]
    tm = min(ROW_TILE, n)
    return pl.pallas_call(
        _final_norm_kernel,
        grid=(n // tm,),
        in_specs=[pl.BlockSpec((tm, D_MODEL), lambda i: (i, 0)), pl.BlockSpec(g.shape, lambda i: (0, 0))],
        out_specs=pl.BlockSpec((tm, D_MODEL), lambda i: (i, 0)),
        out_shape=jax.ShapeDtypeStruct((n, D_MODEL), F32),
        compiler_params=_cparams(1),
        name="final_norm",
    )(h, g)


def _moe(h, g, w_group, b_group, w_expert, b_expert, w_gate, w_up, w_down):
    n = h.shape[0]
    w_r = jnp.zeros((D_MODEL, LANES), F32).at[:, :N_GROUPS].set(w_group)
    w_r = w_r.at[:, N_GROUPS:N_GROUPS + N_EXPERTS].set(w_expert)
    b_r = jnp.zeros((1, LANES), F32).at[0, :N_GROUPS].set(b_group)
    b_r = b_r.at[0, N_GROUPS:N_GROUPS + N_EXPERTS].set(b_expert)
    w_hi = w_r.astype(MXU_DTYPE)
    w_lo = (w_r - w_hi.astype(F32)).astype(MXU_DTYPE)
    ri, rw, cnt = _router(h, g, w_hi, w_lo, b_r)

    counts = cnt[-1, N_GROUPS:N_GROUPS + N_EXPERTS]
    padded = (counts + EXPERT_TILE - 1) // EXPERT_TILE * EXPERT_TILE
    ends = jnp.cumsum(padded)
    starts = ends - padded
    pos = (starts[ri[:, 0:2]] + ri[:, 2:4]).reshape(-1).astype(jnp.int32)
    p_rows = 2 * n + N_EXPERTS * EXPERT_TILE
    tile_start = jnp.arange(p_rows // EXPERT_TILE, dtype=jnp.int32) * EXPERT_TILE
    tile_expert = jnp.minimum(jnp.searchsorted(ends, tile_start, side="right"), N_EXPERTS - 1).astype(jnp.int32)
    tile_active = (tile_start < ends[-1]).astype(jnp.int32)

    xs = _dispatch(pos, h, g, jnp.zeros((p_rows, D_MODEL), F32))
    ys = _experts(tile_expert, tile_active, xs, w_gate.astype(MXU_DTYPE), w_up.astype(MXU_DTYPE),
                  w_down.astype(MXU_DTYPE))
    return _combine(pos, h, rw, ys)


def _even_weights(w_in):
    gate_cols = w_in[:, C_GATE:C_GATE + 3 * NSA_HEADS].reshape(D_MODEL, NSA_KV_HEADS, HEADS_PER_GROUP, 3)
    gate_cols = gate_cols.transpose(0, 1, 3, 2).reshape(D_MODEL, NSA_KV_HEADS, 3 * HEADS_PER_GROUP)
    gate_cols = jnp.pad(gate_cols, ((0, 0), (0, 0), (0, LANES - 3 * HEADS_PER_GROUP)))
    return jnp.concatenate([w_in[:, :C_GATE], gate_cols.reshape(D_MODEL, NSA_KV_HEADS * LANES)], axis=1)


def _block_diag_halves(w):
    blocks = w.shape[0] // 2
    out = jnp.zeros((2, blocks * w.shape[1], blocks * w.shape[2]), w.dtype)
    for j in range(2):
        for k in range(blocks):
            out = out.at[j, k * w.shape[1]:(k + 1) * w.shape[1], k * w.shape[2]:(k + 1) * w.shape[2]].set(
                w[j * blocks + k])
    return out


def _even_mixer(h, g, batch, cos_t, sin_t, w_in, w_out, conv_w, conv_b, w_r, b_r, w_i, b_i, lam,
                pos_k, w1_k, w2_k, pos_v, w1_v, w2_v):
    n = h.shape[0]
    t = n // batch
    xl, gg, qn, qr, kcv, ksr, vs, kwr, vw, gates = _even_proj(
        h, g, _even_weights(w_in).astype(MXU_DTYPE), cos_t, sin_t)

    y_lru = _lru(xl, gg, conv_w, conv_b[None, :], _block_diag_halves(w_r).astype(MXU_DTYPE),
                 _block_diag_halves(w_i).astype(MXU_DTYPE), b_r[None, :], b_i[None, :],
                 jax.nn.softplus(-lam)[None, :], batch)

    nchunk = t // CMP_STRIDE
    x16 = kcv.reshape(batch, nchunk, CMP_STRIDE, 2 * NSA_KV_HEADS, HEAD_DIM).transpose(0, 3, 1, 2, 4)
    x16 = x16.reshape(batch, 2 * NSA_KV_HEADS, nchunk, CMP_STRIDE * HEAD_DIM)
    half = CMP_STRIDE * HEAD_DIM
    pos = jnp.stack([pos_k.reshape(2, 1, half), pos_v.reshape(2, 1, half)])
    w1 = jnp.stack([w1_k.reshape(2, half, CMP_HIDDEN), w1_v.reshape(2, half, CMP_HIDDEN)]).astype(MXU_DTYPE)
    w2 = jnp.stack([w2_k, w2_v]).astype(MXU_DTYPE)
    kvc = _compress(x16, pos, w1, w2)

    tq = min(NSA_TQ, t)
    nq = t // tq
    rows = HEADS_PER_GROUP * tq

    def q_tiles(z):
        z = z.reshape(batch, nq, tq, NSA_KV_HEADS, HEADS_PER_GROUP, HEAD_DIM)
        return z.transpose(0, 3, 1, 5, 4, 2).reshape(batch, NSA_KV_HEADS, nq, HEAD_DIM, rows)

    def kv_heads(z):
        return z.reshape(batch, t, NSA_KV_HEADS, HEAD_DIM).transpose(0, 2, 1, 3)

    def kv_heads_t(z):
        return z.reshape(batch, t, NSA_KV_HEADS, HEAD_DIM).transpose(0, 2, 3, 1)

    n_gate = 3 * HEADS_PER_GROUP
    gates_t = gates.reshape(batch, nq, tq, NSA_KV_HEADS, LANES)[..., :n_gate]
    gates_t = gates_t.reshape(batch, nq, tq, NSA_KV_HEADS, 3, HEADS_PER_GROUP).transpose(0, 3, 1, 4, 5, 2)
    gates_t = jnp.pad(gates_t.reshape(batch, NSA_KV_HEADS, nq, 3, rows), ((0, 0),) * 3 + ((0, SUBLANES - 3), (0, 0)))
    vct = kvc[:, NSA_KV_HEADS:].transpose(0, 1, 3, 2)
    y_t = _nsa(q_tiles(qn), q_tiles(qr), kvc, vct, kv_heads(ksr), kv_heads_t(vs), kv_heads(kwr), kv_heads_t(vw),
               gates_t)
    y_nsa = y_t.reshape(batch, NSA_KV_HEADS, nq, HEAD_DIM, HEADS_PER_GROUP, tq).transpose(0, 2, 5, 1, 4, 3)
    w_out = w_out.astype(MXU_DTYPE)
    return _out_proj(h, y_lru, y_nsa.reshape(n, NSA_HEADS * HEAD_DIM), w_out[:LRU_WIDTH], w_out[LRU_WIDTH:])


def _rope_tables(positions):
    inv = ROPE_THETA ** (-jnp.arange(0, 2 * ROT_HALF, 2, dtype=F32) / (2 * ROT_HALF))
    ang = positions.reshape(-1).astype(F32)[:, None] * inv
    cos, sin = jnp.cos(ang), jnp.sin(ang)
    rest = HEAD_DIM - 2 * ROT_HALF
    cos_h = jnp.concatenate([cos, cos, jnp.ones((cos.shape[0], rest), F32)], axis=1)
    sin_h = jnp.concatenate([-sin, sin, jnp.zeros((cos.shape[0], rest), F32)], axis=1)
    reps = LANES // HEAD_DIM
    return jnp.tile(cos_h, (1, reps)), jnp.tile(sin_h, (1, reps))


def kernel(x, mem, positions, norm_mix, norm_xattn, norm_mem, norm_ffn, norm_final, even_w_in, even_w_out, lru_conv_w, lru_conv_b, lru_w_r, lru_b_r, lru_w_i, lru_b_i, lru_lambda, nsa_cmp_pos_k, nsa_cmp_w1_k, nsa_cmp_w2_k, nsa_cmp_pos_v, nsa_cmp_w1_v, nsa_cmp_w2_v, odd_w_in, odd_conv_w, odd_w_out, xa_wq, xa_wk, xa_wv, xa_wo, moe_w_group, moe_b_group, moe_w_expert, moe_b_expert, moe_w_gate, moe_w_up, moe_w_down):
    batch, t, d = x.shape
    n = batch * t
    depth = norm_mix.shape[0]
    cos_t, sin_t = _rope_tables(positions)
    h = x.reshape(n, d)
    mem2 = mem.reshape(-1, d)
    for layer in range(depth):
        g_mix = norm_mix[layer][None, :]
        if layer % 2 == 0:
            e = layer // 2
            h = _even_mixer(h, g_mix, batch, cos_t, sin_t, even_w_in[e], even_w_out[e], lru_conv_w[e],
                            lru_conv_b[e], lru_w_r[e], lru_b_r[e], lru_w_i[e], lru_b_i[e], lru_lambda[e],
                            nsa_cmp_pos_k[e], nsa_cmp_w1_k[e], nsa_cmp_w2_k[e],
                            nsa_cmp_pos_v[e], nsa_cmp_w1_v[e], nsa_cmp_w2_v[e])
        else:
            o = layer // 2
            h = _short_conv(h, g_mix, odd_w_in[o].astype(MXU_DTYPE), odd_conv_w[o],
                            odd_w_out[o].astype(MXU_DTYPE), batch)
        wkv = jnp.concatenate([xa_wk[layer], xa_wv[layer]], axis=1).astype(MXU_DTYPE)
        kv = _mem_kv(mem2, norm_mem[layer][None, :], wkv)
        h = _xattn(h, norm_xattn[layer][None, :], xa_wq[layer].astype(MXU_DTYPE), kv,
                   xa_wo[layer].astype(MXU_DTYPE), batch)
        h = _moe(h, norm_ffn[layer][None, :], moe_w_group[layer], moe_b_group[layer], moe_w_expert[layer],
                 moe_b_expert[layer], moe_w_gate[layer], moe_w_up[layer], moe_w_down[layer])
    return _final_norm(h, norm_final[None, :]).reshape(batch, t, d)
```

```python
import functools

import jax
import jax.numpy as jnp
import numpy as np
from jax import lax
from jax.experimental import pallas as pl
from jax.experimental.pallas import tpu as pltpu

F32 = jnp.float32
MXU_DTYPE = jnp.bfloat16

D_MODEL = 1024
LRU_WIDTH = 512
LRU_CONV = 4
LRU_C = 8.0
NSA_HEADS = 8
NSA_KV_HEADS = 2
HEADS_PER_GROUP = NSA_HEADS // NSA_KV_HEADS
HEAD_DIM = 64
CMP_STRIDE = 16
CMP_LEN = 32
CMP_HIDDEN = 128
SEL_BLOCK = 64
SEL_TOP = 16
WINDOW = 512
ROT_HALF = 8
ROPE_THETA = 500000.0
SC_CONV = 3
XA_HEADS = 4
XA_HEAD_DIM = 256
N_GROUPS = 4
EXPERTS_PER_GROUP = 4
N_EXPERTS = 16
D_EXPERT = 512
EPS = 1e-6
NEG = -1e30
FORCE = 1e9

LANES = 128
SUBLANES = 8
VMEM_LIMIT = 56 * 1024 * 1024

C_XL, C_GL, C_Q, C_KCV, C_KS, C_VS, C_KW, C_VW, C_GATE, C_END = (
    0, 512, 1024, 1536, 1792, 1920, 2048, 2176, 2304, 2560)

ROW_TILE = 512
NSA_TQ = 128
NSA_TK = 512
EXPERT_TILE = 256
DMA_UNROLL = 8
ROW_CHUNKS = D_MODEL // LANES


def _cparams(n_axes):
    return pltpu.CompilerParams(dimension_semantics=("arbitrary",) * n_axes,
                                vmem_limit_bytes=VMEM_LIMIT)


def _mm(a, b):
    return jnp.dot(a, b, preferred_element_type=F32)


def _mm_nt(a, b):
    return lax.dot_general(a, b, (((1,), (1,)), ((), ())), preferred_element_type=F32)


def _rms(x, g):
    return x * lax.rsqrt(jnp.mean(x * x, axis=-1, keepdims=True) + EPS) * g


def _shift_rows(xx, s, rows):
    if s == 0:
        return xx[SUBLANES:SUBLANES + rows]
    return pltpu.roll(xx, s, 0)[SUBLANES:SUBLANES + rows]


def _rope(x, cos_t, sin_t):
    width = x.shape[-1]
    lane = lax.broadcasted_iota(jnp.int32, x.shape, 1) % HEAD_DIM
    partner = jnp.where(lane < ROT_HALF, pltpu.roll(x, width - ROT_HALF, 1), pltpu.roll(x, ROT_HALF, 1))
    return x * cos_t + partner * sin_t


def _even_proj_kernel(h_ref, g_ref, w_ref, cos_ref, sin_ref,
                      xl_ref, gg_ref, qn_ref, qr_ref, kcv_ref, ksr_ref, vs_ref, kwr_ref, vw_ref,
                      gate_ref):
    xn = _rms(h_ref[...], g_ref[...]).astype(MXU_DTYPE)

    def proj(c0, c1):
        return _mm(xn, w_ref[:, c0:c1])

    cos_t = cos_ref[...]
    sin_t = sin_ref[...]
    xl_ref[...] = proj(C_XL, C_GL)
    gg_ref[...] = jax.nn.gelu(proj(C_GL, C_Q))
    q = proj(C_Q, C_KCV) * (HEAD_DIM ** -0.5)
    reps = (C_KCV - C_Q) // LANES
    cos_q = jnp.concatenate([cos_t] * reps, axis=1)
    sin_q = jnp.concatenate([sin_t] * reps, axis=1)
    qn_ref[...] = q.astype(qn_ref.dtype)
    qr_ref[...] = _rope(q, cos_q, sin_q).astype(qr_ref.dtype)
    kcv_ref[...] = proj(C_KCV, C_KS)
    ksr_ref[...] = _rope(proj(C_KS, C_VS), cos_t, sin_t).astype(ksr_ref.dtype)
    vs_ref[...] = proj(C_VS, C_KW).astype(vs_ref.dtype)
    kwr_ref[...] = _rope(proj(C_KW, C_VW), cos_t, sin_t).astype(kwr_ref.dtype)
    vw_ref[...] = proj(C_VW, C_GATE).astype(vw_ref.dtype)
    gate_ref[...] = jax.nn.sigmoid(proj(C_GATE, C_END))


def _even_proj(h, g, w, cos_t, sin_t):
    n = h.shape[0]
    tm = min(ROW_TILE, n)
    row = lambda c: pl.BlockSpec((tm, c), lambda i: (i, 0))
    full = lambda a: pl.BlockSpec(a.shape, lambda i: (0,) * a.ndim)
    widths = [(512, F32), (512, F32), (512, MXU_DTYPE), (512, MXU_DTYPE), (256, F32),
              (128, MXU_DTYPE), (128, MXU_DTYPE), (128, MXU_DTYPE), (128, MXU_DTYPE), (256, F32)]
    return pl.pallas_call(
        _even_proj_kernel,
        grid=(n // tm,),
        in_specs=[row(D_MODEL), full(g), full(w), row(LANES), row(LANES)],
        out_specs=[row(c) for c, _ in widths],
        out_shape=[jax.ShapeDtypeStruct((n, c), dt) for c, dt in widths],
        compiler_params=_cparams(1),
        name="even_proj",
    )(h, g, w, cos_t, sin_t)


def _lru_kernel(xl_ref, gg_ref, cw_ref, cb_ref, wr_ref, wi_ref, br_ref, bi_ref, sp_ref,
                y_ref, tail_ref, h_ref, a_scr, u_scr):
    tt = xl_ref.shape[0]

    @pl.when(pl.program_id(1) == 0)
    def _():
        tail_ref[...] = jnp.zeros_like(tail_ref)
        h_ref[...] = jnp.zeros_like(h_ref)

    x = xl_ref[...]
    xx = jnp.concatenate([tail_ref[...], x], axis=0)
    tail_ref[...] = x[tt - SUBLANES:tt]
    xc = cb_ref[...] + sum(cw_ref[k:k + 1, :] * _shift_rows(xx, LRU_CONV - 1 - k, tt)
                           for k in range(LRU_CONV))
    xcb = xc.astype(MXU_DTYPE)
    half = LRU_WIDTH // 2
    r_lin = jnp.concatenate([_mm(xcb[:, j * half:(j + 1) * half], wr_ref[j]) for j in range(2)], axis=1)
    i_lin = jnp.concatenate([_mm(xcb[:, j * half:(j + 1) * half], wi_ref[j]) for j in range(2)], axis=1)
    r = jax.nn.sigmoid(r_lin + br_ref[...])
    i = jax.nn.sigmoid(i_lin + bi_ref[...])
    log_a = -LRU_C * r * sp_ref[...]
    a = jnp.exp(log_a)
    u = jnp.sqrt(jnp.tanh(-log_a) * (1.0 + a * a)) * (i * xc)

    r8 = lax.broadcasted_iota(jnp.int32, a.shape, 0) % SUBLANES
    for s in (1, 2, 4):
        keep = r8 >= s
        u = jnp.where(keep, a * pltpu.roll(u, s, 0) + u, u)
        a = jnp.where(keep, a * pltpu.roll(a, s, 0), a)
    a_scr[...] = a
    u_scr[...] = u

    def body(gidx, h):
        r0 = pl.multiple_of(gidx * SUBLANES, SUBLANES)
        out = a_scr[pl.ds(r0, SUBLANES), :] * h + u_scr[pl.ds(r0, SUBLANES), :]
        u_scr[pl.ds(r0, SUBLANES), :] = out
        return out[SUBLANES - 1:SUBLANES, :]

    h_ref[...] = lax.fori_loop(0, tt // SUBLANES, body, h_ref[...])
    y_ref[...] = (u_scr[...] * gg_ref[...]).astype(y_ref.dtype)


def _lru(xl, gg, cw, cb, wr, wi, br, bi, sp, batch):
    n = xl.shape[0]
    t = n // batch
    tt = min(ROW_TILE, t)
    nt = t // tt
    row = pl.BlockSpec((tt, LRU_WIDTH), lambda b, i: (b * nt + i, 0))
    full = lambda a: pl.BlockSpec(a.shape, lambda b, i: (0,) * a.ndim)
    return pl.pallas_call(
        _lru_kernel,
        grid=(batch, nt),
        in_specs=[row, row] + [full(a) for a in (cw, cb, wr, wi, br, bi, sp)],
        out_specs=row,
        out_shape=jax.ShapeDtypeStruct((n, LRU_WIDTH), MXU_DTYPE),
        scratch_shapes=[pltpu.VMEM((SUBLANES, LRU_WIDTH), F32), pltpu.VMEM((1, LRU_WIDTH), F32),
                        pltpu.VMEM((tt, LRU_WIDTH), F32), pltpu.VMEM((tt, LRU_WIDTH), F32)],
        compiler_params=_cparams(2),
        name="lru",
    )(xl, gg, cw, cb, wr, wi, br, bi, sp)


def _compress_kernel(x_ref, pos_ref, w1_ref, w2_ref, o_ref):
    x = x_ref[0, 0]
    nchunk = x.shape[0]
    lo = (x + pos_ref[0, 0]).astype(MXU_DTYPE)
    hi = (x + pos_ref[0, 1]).astype(MXU_DTYPE)
    p_lo = _mm(lo, w1_ref[0, 0])
    p_hi = _mm(hi, w1_ref[0, 1])
    a = p_lo + pltpu.roll(p_hi, nchunk - 1, 0)
    o_ref[0, 0] = _mm(jax.nn.gelu(a).astype(MXU_DTYPE), w2_ref[0]).astype(o_ref.dtype)


def _compress(x16, pos, w1, w2):
    b, _, nchunk, width = x16.shape
    return pl.pallas_call(
        _compress_kernel,
        grid=(b, 2 * NSA_KV_HEADS),
        in_specs=[pl.BlockSpec((1, 1, nchunk, width), lambda i, j: (i, j, 0, 0)),
                  pl.BlockSpec((1, 2, 1, width), lambda i, j: (j // NSA_KV_HEADS, 0, 0, 0)),
                  pl.BlockSpec((1, 2, width, CMP_HIDDEN), lambda i, j: (j // NSA_KV_HEADS, 0, 0, 0)),
                  pl.BlockSpec((1, CMP_HIDDEN, HEAD_DIM), lambda i, j: (j // NSA_KV_HEADS, 0, 0))],
        out_specs=pl.BlockSpec((1, 1, nchunk, HEAD_DIM), lambda i, j: (i, j, 0, 0)),
        out_shape=jax.ShapeDtypeStruct((b, 2 * NSA_KV_HEADS, nchunk, HEAD_DIM), MXU_DTYPE),
        compiler_params=_cparams(2),
        name="compress",
    )(x16, pos, w1, w2)


def _nsa_kernel(qn_ref, qr_ref, kc_ref, vct_ref, ks_ref, vst_ref, kw_ref, vwt_ref, gate_ref,
                y_ref, chosen_scr, acc_scr):
    rows = qn_ref.shape[4]
    tq = rows // HEADS_PER_GROUP
    t = ks_ref.shape[2]
    nc = kc_ref.shape[2]
    nsel = t // SEL_BLOCK
    n_top = min(SEL_TOP, nsel)
    tk = NSA_TK
    blocks_per_tile = tk // SEL_BLOCK
    tiles_per_group = SUBLANES // blocks_per_tile
    t0 = pl.program_id(2) * tq
    qn = qn_ref[0, 0, 0]
    qr = qr_ref[0, 0, 0]

    s = _mm(kc_ref[0, 0], qn)
    tq_row = t0 + lax.broadcasted_iota(jnp.int32, (nc, rows), 1) % tq
    cmp_end = lax.broadcasted_iota(jnp.int32, (nc, rows), 0) * CMP_STRIDE + (CMP_LEN - 1)
    valid = cmp_end <= tq_row
    s = jnp.where(valid, s, NEG)
    e = jnp.where(valid, jnp.exp(s - jnp.max(s, axis=0, keepdims=True)), 0.0)
    den = jnp.sum(e, axis=0, keepdims=True)
    p = e * (1.0 / jnp.where(den > 0.0, den, 1.0))
    o_cmp = _mm(vct_ref[0, 0], p.astype(MXU_DTYPE))

    psum = p[:, 0:tq]
    for hh in range(1, HEADS_PER_GROUP):
        psum = psum + p[:, hh * tq:(hh + 1) * tq]
    p_hi = psum.astype(MXU_DTYPE)
    p_lo = (psum - p_hi.astype(F32)).astype(MXU_DTYPE)
    cj = lax.broadcasted_iota(jnp.int32, (nsel, nc), 0)
    cn = lax.broadcasted_iota(jnp.int32, (nsel, nc), 1)
    ratio = SEL_BLOCK // CMP_STRIDE
    cover = jnp.where((cn >= ratio * cj - (CMP_LEN // CMP_STRIDE - 1)) & (cn <= ratio * cj + ratio - 1)
                      & (cn < nc - 1), 1.0, 0.0).astype(MXU_DTYPE)
    imp = _mm(cover, p_hi) + _mm(cover, p_lo)

    blk = lax.broadcasted_iota(jnp.int32, (nsel, tq), 0)
    tq_col = t0 + lax.broadcasted_iota(jnp.int32, (nsel, tq), 1)
    cur = tq_col // SEL_BLOCK
    forced = (blk == 0) | (blk == cur) | (blk == cur - 1)
    causal = blk * SEL_BLOCK <= tq_col
    score0 = jnp.where(causal, jnp.where(forced, FORCE, imp), NEG)
    blk_f = blk.astype(F32)

    def pick_one(_, carry):
        score, chosen = carry
        best = jnp.max(score, axis=0, keepdims=True)
        first = jnp.min(jnp.where(score == best, blk_f, float(nsel)), axis=0, keepdims=True)
        hit = blk_f == first
        return jnp.where(hit, -jnp.inf, score), jnp.where(hit, 1.0, chosen)

    _, chosen = lax.fori_loop(0, n_top, pick_one, (score0, jnp.zeros((nsel, tq), F32)))
    chosen_scr[...] = chosen

    def sweep(k_ref, vt_ref, kt_lo, kt_hi, allowed_fn):
        acc_scr[...] = jnp.zeros_like(acc_scr)

        def body(kt, carry):
            m_old, l_old = carry
            k0 = pl.multiple_of(kt * tk, tk)
            kpos = k0 + lax.broadcasted_iota(jnp.int32, (tk, tq), 0)
            tqm = t0 + lax.broadcasted_iota(jnp.int32, (tk, tq), 1)
            bias = jnp.where(allowed_fn(kt, kpos, tqm), 0.0, NEG)
            bias = jnp.concatenate([bias] * HEADS_PER_GROUP, axis=1)
            sc = _mm(k_ref[0, 0, pl.ds(k0, tk), :], qr) + bias
            m_new = jnp.maximum(m_old, jnp.max(sc, axis=0, keepdims=True))
            alpha = jnp.exp(m_old - m_new)
            pe = jnp.exp(sc - m_new)
            l_new = alpha * l_old + jnp.sum(pe, axis=0, keepdims=True)
            acc_scr[...] = alpha * acc_scr[...] + _mm(vt_ref[0, 0, :, pl.ds(k0, tk)], pe.astype(MXU_DTYPE))
            return m_new, l_new

        init = (jnp.full((1, rows), NEG, F32), jnp.zeros((1, rows), F32))
        _, l_fin = lax.fori_loop(kt_lo, kt_hi, body, init)
        return acc_scr[...] * (1.0 / l_fin)

    kt_hi = (t0 + tq - 1) // tk + 1

    def sel_allowed(kt, kpos, tqm):
        grp = pl.multiple_of(kt // tiles_per_group * SUBLANES, SUBLANES)
        first_blk = kt % tiles_per_group * blocks_per_tile
        ei = lax.broadcasted_iota(jnp.int32, (tk, SUBLANES), 0) // SEL_BLOCK + first_blk
        ej = lax.broadcasted_iota(jnp.int32, (tk, SUBLANES), 1)
        expand = jnp.where(ei == ej, 1.0, 0.0)
        return (_mm(expand, chosen_scr[pl.ds(grp, SUBLANES), :]) > 0.5) & (kpos <= tqm)

    def win_allowed(kt, kpos, tqm):
        return (kpos <= tqm) & (kpos > tqm - WINDOW)

    o_sel = sweep(ks_ref, vst_ref, 0, kt_hi, sel_allowed)
    o_win = sweep(kw_ref, vwt_ref, jnp.maximum(t0 - (WINDOW - 1), 0) // tk, kt_hi, win_allowed)

    gate = gate_ref[0, 0, 0]
    y_ref[0, 0, 0] = (gate[0:1] * o_cmp + gate[1:2] * o_sel + gate[2:3] * o_win).astype(y_ref.dtype)


def _nsa(qn_t, qr_t, kvc, vct, ks, vst, kw, vwt, gates_t):
    b, _, nq, _, rows = qn_t.shape
    t = ks.shape[2]
    nc = kvc.shape[2]
    assert t % (SEL_BLOCK * SUBLANES) == 0 and t % NSA_TK == 0
    q_spec = pl.BlockSpec((1, 1, 1, HEAD_DIM, rows), lambda i, g, j: (i, g, j, 0, 0))
    k_spec = pl.BlockSpec((1, 1, t, HEAD_DIM), lambda i, g, j: (i, g, 0, 0))
    vt_spec = pl.BlockSpec((1, 1, HEAD_DIM, t), lambda i, g, j: (i, g, 0, 0))
    return pl.pallas_call(
        _nsa_kernel,
        grid=(b, NSA_KV_HEADS, nq),
        in_specs=[q_spec, q_spec,
                  pl.BlockSpec((1, 1, nc, HEAD_DIM), lambda i, g, j: (i, g, 0, 0)),
                  pl.BlockSpec((1, 1, HEAD_DIM, nc), lambda i, g, j: (i, g, 0, 0)),
                  k_spec, vt_spec, k_spec, vt_spec,
                  pl.BlockSpec((1, 1, 1, SUBLANES, rows), lambda i, g, j: (i, g, j, 0, 0))],
        out_specs=q_spec,
        out_shape=jax.ShapeDtypeStruct(qn_t.shape, MXU_DTYPE),
        scratch_shapes=[pltpu.VMEM((t // SEL_BLOCK, rows // HEADS_PER_GROUP), F32),
                        pltpu.VMEM((HEAD_DIM, rows), F32)],
        compiler_params=_cparams(3),
        name="nsa",
    )(qn_t, qr_t, kvc, vct, ks, vst, kw, vwt, gates_t)


def _out_proj_kernel(h_ref, a_ref, b_ref, wa_ref, wb_ref, o_ref):
    o_ref[...] = h_ref[...] + _mm(a_ref[...], wa_ref[...]) + _mm(b_ref[...], wb_ref[...])


def _out_proj(h, a, b, wa, wb):
    n = h.shape[0]
    tm = min(ROW_TILE, n)
    full = lambda x: pl.BlockSpec(x.shape, lambda i: (0,) * x.ndim)
    return pl.pallas_call(
        _out_proj_kernel,
        grid=(n // tm,),
        in_specs=[pl.BlockSpec((tm, D_MODEL), lambda i: (i, 0)),
                  pl.BlockSpec((tm, a.shape[1]), lambda i: (i, 0)),
                  pl.BlockSpec((tm, b.shape[1]), lambda i: (i, 0)), full(wa), full(wb)],
        out_specs=pl.BlockSpec((tm, D_MODEL), lambda i: (i, 0)),
        out_shape=jax.ShapeDtypeStruct((n, D_MODEL), F32),
        compiler_params=_cparams(1),
        name="out_proj",
    )(h, a, b, wa, wb)


def _short_conv_kernel(h_ref, g_ref, win_ref, cw_ref, wout_ref, o_ref, tail_ref):
    tt = h_ref.shape[0]

    @pl.when(pl.program_id(1) == 0)
    def _():
        tail_ref[...] = jnp.zeros_like(tail_ref)

    h = h_ref[...]
    xn = _rms(h, g_ref[...]).astype(MXU_DTYPE)
    b_g = _mm(xn, win_ref[:, 0:D_MODEL])
    cv = _mm(xn, win_ref[:, D_MODEL:2 * D_MODEL]) * _mm(xn, win_ref[:, 2 * D_MODEL:3 * D_MODEL])
    xx = jnp.concatenate([tail_ref[...], cv], axis=0)
    tail_ref[...] = cv[tt - SUBLANES:tt]
    conv = sum(cw_ref[k:k + 1, :] * _shift_rows(xx, SC_CONV - 1 - k, tt) for k in range(SC_CONV))
    o_ref[...] = h + _mm((b_g * conv).astype(MXU_DTYPE), wout_ref[...])


def _short_conv(h, g, w_in, cw, w_out, batch):
    n = h.shape[0]
    t = n // batch
    tt = min(ROW_TILE, t)
    nt = t // tt
    row = pl.BlockSpec((tt, D_MODEL), lambda b, i: (b * nt + i, 0))
    full = lambda a: pl.BlockSpec(a.shape, lambda b, i: (0,) * a.ndim)
    return pl.pallas_call(
        _short_conv_kernel,
        grid=(batch, nt),
        in_specs=[row, full(g), full(w_in), full(cw), full(w_out)],
        out_specs=row,
        out_shape=jax.ShapeDtypeStruct((n, D_MODEL), F32),
        scratch_shapes=[pltpu.VMEM((SUBLANES, D_MODEL), F32)],
        compiler_params=_cparams(2),
        name="short_conv",
    )(h, g, w_in, cw, w_out)


def _mem_kv_kernel(m_ref, g_ref, w_ref, o_ref):
    o_ref[...] = _mm(_rms(m_ref[...], g_ref[...]).astype(MXU_DTYPE), w_ref[...]).astype(o_ref.dtype)


def _mem_kv(mem, g, wkv):
    n = mem.shape[0]
    tm = min(ROW_TILE, n)
    tn = 1024
    return pl.pallas_call(
        _mem_kv_kernel,
        grid=(n // tm, wkv.shape[1] // tn),
        in_specs=[pl.BlockSpec((tm, D_MODEL), lambda i, j: (i, 0)),
                  pl.BlockSpec(g.shape, lambda i, j: (0, 0)),
                  pl.BlockSpec((D_MODEL, tn), lambda i, j: (0, j))],
        out_specs=pl.BlockSpec((tm, tn), lambda i, j: (i, j)),
        out_shape=jax.ShapeDtypeStruct((n, wkv.shape[1]), MXU_DTYPE),
        compiler_params=_cparams(2),
        name="mem_kv",
    )(mem, g, wkv)


def _xattn_kernel(h_ref, g_ref, wq_ref, kv_ref, wo_ref, o_ref):
    h = h_ref[...]
    xn = _rms(h, g_ref[...]).astype(MXU_DTYPE)
    q = (_mm(xn, wq_ref[...]) * (XA_HEAD_DIM ** -0.5)).astype(MXU_DTYPE)
    width = XA_HEADS * XA_HEAD_DIM
    outs = []
    for hd in range(XA_HEADS):
        sl = slice(hd * XA_HEAD_DIM, (hd + 1) * XA_HEAD_DIM)
        s = _mm_nt(q[:, sl], kv_ref[:, sl])
        e = jnp.exp(s - jnp.max(s, axis=-1, keepdims=True))
        p = e / jnp.sum(e, axis=-1, keepdims=True)
        outs.append(_mm(p.astype(MXU_DTYPE), kv_ref[:, width + hd * XA_HEAD_DIM:width + (hd + 1) * XA_HEAD_DIM]))
    o = jnp.concatenate(outs, axis=1).astype(MXU_DTYPE)
    o_ref[...] = h + _mm(o, wo_ref[...])


def _xattn(h, g, wq, kv, wo, batch):
    n = h.shape[0]
    t = n // batch
    tm = min(ROW_TILE, t)
    nt = t // tm
    mlen = kv.shape[0] // batch
    full = lambda a: pl.BlockSpec(a.shape, lambda b, i: (0,) * a.ndim)
    row = pl.BlockSpec((tm, D_MODEL), lambda b, i: (b * nt + i, 0))
    return pl.pallas_call(
        _xattn_kernel,
        grid=(batch, nt),
        in_specs=[row, full(g), full(wq), pl.BlockSpec((mlen, kv.shape[1]), lambda b, i: (b, 0)), full(wo)],
        out_specs=row,
        out_shape=jax.ShapeDtypeStruct((n, D_MODEL), F32),
        compiler_params=_cparams(2),
        name="xattn",
    )(h, g, wq, kv, wo)


def _router_kernel(h_ref, g_ref, whi_ref, wlo_ref, b_ref, ri_ref, rw_ref, cnt_ref, carry_ref):
    tm = h_ref.shape[0]

    @pl.when(pl.program_id(0) == 0)
    def _():
        carry_ref[...] = jnp.zeros_like(carry_ref)

    xn = _rms(h_ref[...], g_ref[...])
    x_hi = xn.astype(MXU_DTYPE)
    x_lo = (xn - x_hi.astype(F32)).astype(MXU_DTYPE)
    logits = _mm(x_hi, whi_ref[...]) + (_mm(x_lo, whi_ref[...]) + _mm(x_hi, wlo_ref[...])) + b_ref[...]

    lane = lax.broadcasted_iota(jnp.int32, logits.shape, 1)
    lane_f = lane.astype(F32)
    none = float(LANES)
    is_g = lane < N_GROUPS
    g_max = jnp.max(jnp.where(is_g, logits, -jnp.inf), axis=-1, keepdims=True)
    g_sum = jnp.sum(jnp.where(is_g, jnp.exp(logits - g_max), 0.0), axis=-1, keepdims=True)
    g_top = 1.0 / g_sum
    g_idx = jnp.min(jnp.where(is_g & (logits == g_max), lane_f, none), axis=-1, keepdims=True)
    first = N_GROUPS + EXPERTS_PER_GROUP * g_idx
    in_g = (lane_f >= first) & (lane_f < first + EXPERTS_PER_GROUP)
    e1 = jnp.max(jnp.where(in_g, logits, -jnp.inf), axis=-1, keepdims=True)
    i1 = jnp.min(jnp.where(in_g & (logits == e1), lane_f, none), axis=-1, keepdims=True)
    rest = in_g & (lane_f != i1)
    e2 = jnp.max(jnp.where(rest, logits, -jnp.inf), axis=-1, keepdims=True)
    i2 = jnp.min(jnp.where(rest & (logits == e2), lane_f, none), axis=-1, keepdims=True)
    ratio = jnp.exp(e2 - e1)
    w1 = g_top / (1.0 + ratio)
    w2 = g_top * ratio / (1.0 + ratio)

    onehot = jnp.where((lane_f == i1) | (lane_f == i2), 1.0, 0.0)
    tri = jnp.where(lax.broadcasted_iota(jnp.int32, (tm, tm), 0) >= lax.broadcasted_iota(jnp.int32, (tm, tm), 1),
                    1.0, 0.0).astype(MXU_DTYPE)
    incl = _mm(tri, onehot.astype(MXU_DTYPE))
    before = incl - onehot + carry_ref[...]
    carry_ref[...] = carry_ref[...] + incl[tm - 1:tm, :]
    rank1 = jnp.sum(jnp.where(lane_f == i1, before, 0.0), axis=-1, keepdims=True)
    rank2 = jnp.sum(jnp.where(lane_f == i2, before, 0.0), axis=-1, keepdims=True)

    ri = jnp.where(lane == 0, i1 - N_GROUPS,
                   jnp.where(lane == 1, i2 - N_GROUPS, jnp.where(lane == 2, rank1, jnp.where(lane == 3, rank2, 0.0))))
    ri_ref[...] = ri.astype(jnp.int32)
    rw_ref[...] = jnp.where(lane == 0, w1, jnp.where(lane == 1, w2, 0.0))
    cnt_ref[...] = jnp.broadcast_to(carry_ref[...], cnt_ref.shape).astype(jnp.int32)


def _router(h, g, w_hi, w_lo, bias):
    n = h.shape[0]
    tm = min(ROW_TILE, n)
    full = lambda a: pl.BlockSpec(a.shape, lambda i: (0,) * a.ndim)
    return pl.pallas_call(
        _router_kernel,
        grid=(n // tm,),
        in_specs=[pl.BlockSpec((tm, D_MODEL), lambda i: (i, 0)), full(g), full(w_hi), full(w_lo), full(bias)],
        out_specs=[pl.BlockSpec((tm, LANES), lambda i: (i, 0)), pl.BlockSpec((tm, LANES), lambda i: (i, 0)),
                   pl.BlockSpec((SUBLANES, LANES), lambda i: (i, 0))],
        out_shape=[jax.ShapeDtypeStruct((n, LANES), jnp.int32), jax.ShapeDtypeStruct((n, LANES), F32),
                   jax.ShapeDtypeStruct((n // tm * SUBLANES, LANES), jnp.int32)],
        scratch_shapes=[pltpu.VMEM((1, LANES), F32)],
        compiler_params=_cparams(1),
        name="router",
    )(h, g, w_hi, w_lo, bias)


def _row_copy(src_ref, src_row, dst_ref, dst_row, sem):
    src = src_ref.at[pl.ds(pl.multiple_of(src_row * ROW_CHUNKS, ROW_CHUNKS), ROW_CHUNKS)]
    dst = dst_ref.at[pl.ds(pl.multiple_of(dst_row * ROW_CHUNKS, ROW_CHUNKS), ROW_CHUNKS)]
    return pltpu.make_async_copy(src, dst, sem)


def _to_token_tiles(ref, x):
    rows = x.shape[0]
    for c in range(ROW_CHUNKS):
        ref[pl.ds(c, rows, stride=ROW_CHUNKS), :] = x[:, c * LANES:(c + 1) * LANES]


def _from_token_tiles(ref):
    rows = ref.shape[0] // ROW_CHUNKS
    return jnp.concatenate([ref[pl.ds(c, rows, stride=ROW_CHUNKS), :] for c in range(ROW_CHUNKS)], axis=1)


def _dispatch_kernel(pos_ref, h_ref, g_ref, xs_in_ref, xs_ref, xn_scr, sem):
    del xs_in_ref
    tm = h_ref.shape[0]
    base = pl.program_id(0) * (2 * tm)
    _to_token_tiles(xn_scr, _rms(h_ref[...], g_ref[...]))

    def issue(r, _):
        _row_copy(xn_scr, r, xs_ref, pos_ref[base + 2 * r], sem).start()
        _row_copy(xn_scr, r, xs_ref, pos_ref[base + 2 * r + 1], sem).start()
        return 0

    lax.fori_loop(0, tm, issue, 0, unroll=DMA_UNROLL)
    for _ in range(2):
        pltpu.make_async_copy(xn_scr, xs_ref.at[pl.ds(0, tm * ROW_CHUNKS)], sem).wait()


def _dispatch(pos, h, g, xs_zero):
    n = h.shape[0]
    tm = min(ROW_TILE, n)
    grid_spec = pltpu.PrefetchScalarGridSpec(
        num_scalar_prefetch=1,
        grid=(n // tm,),
        in_specs=[pl.BlockSpec((tm, D_MODEL), lambda i, pos: (i, 0)),
                  pl.BlockSpec(g.shape, lambda i, pos: (0, 0)),
                  pl.BlockSpec(memory_space=pl.ANY)],
        out_specs=pl.BlockSpec(memory_space=pl.ANY),
        scratch_shapes=[pltpu.VMEM((tm * ROW_CHUNKS, LANES), F32), pltpu.SemaphoreType.DMA(())],
    )
    return pl.pallas_call(
        _dispatch_kernel,
        grid_spec=grid_spec,
        out_shape=jax.ShapeDtypeStruct(xs_zero.shape, F32),
        input_output_aliases={3: 0},
        compiler_params=_cparams(1),
        name="dispatch",
    )(pos, h, g, xs_zero)


def _expert_kernel(te_ref, ta_ref, x_ref, wg_ref, wu_ref, wd_ref, y_ref):
    j = pl.program_id(0)

    @pl.when(ta_ref[j] == 1)
    def _():
        x = _from_token_tiles(x_ref).astype(MXU_DTYPE)
        hid = jax.nn.silu(_mm(x, wg_ref[0])) * _mm(x, wu_ref[0])
        _to_token_tiles(y_ref, _mm(hid.astype(MXU_DTYPE), wd_ref[0]))

    @pl.when(ta_ref[j] == 0)
    def _():
        y_ref[...] = jnp.zeros_like(y_ref)


def _experts(tile_expert, tile_active, xs, wg, wu, wd):
    p = xs.shape[0] // ROW_CHUNKS
    grid_spec = pltpu.PrefetchScalarGridSpec(
        num_scalar_prefetch=2,
        grid=(p // EXPERT_TILE,),
        in_specs=[pl.BlockSpec((EXPERT_TILE * ROW_CHUNKS, LANES), lambda j, te, ta: (j, 0)),
                  pl.BlockSpec((1, D_MODEL, D_EXPERT), lambda j, te, ta: (te[j], 0, 0)),
                  pl.BlockSpec((1, D_MODEL, D_EXPERT), lambda j, te, ta: (te[j], 0, 0)),
                  pl.BlockSpec((1, D_EXPERT, D_MODEL), lambda j, te, ta: (te[j], 0, 0))],
        out_specs=pl.BlockSpec((EXPERT_TILE * ROW_CHUNKS, LANES), lambda j, te, ta: (j, 0)),
    )
    return pl.pallas_call(
        _expert_kernel,
        grid_spec=grid_spec,
        out_shape=jax.ShapeDtypeStruct(xs.shape, F32),
        compiler_params=_cparams(1),
        name="experts",
    )(tile_expert, tile_active, xs, wg, wu, wd)


def _combine_kernel(pos_ref, h_ref, rw_ref, ys_ref, o_ref, y1_scr, y2_scr, sem):
    tm = h_ref.shape[0]
    base = pl.program_id(0) * (2 * tm)

    def issue(r, _):
        _row_copy(ys_ref, pos_ref[base + 2 * r], y1_scr, r, sem).start()
        _row_copy(ys_ref, pos_ref[base + 2 * r + 1], y2_scr, r, sem).start()
        return 0

    lax.fori_loop(0, tm, issue, 0, unroll=DMA_UNROLL)
    for buf in (y1_scr, y2_scr):
        pltpu.make_async_copy(ys_ref.at[pl.ds(0, tm * ROW_CHUNKS)], buf, sem).wait()
    rw = rw_ref[...]
    o_ref[...] = h_ref[...] + (rw[:, 0:1] * _from_token_tiles(y1_scr) + rw[:, 1:2] * _from_token_tiles(y2_scr))


def _combine(pos, h, rw, ys):
    n = h.shape[0]
    tm = min(ROW_TILE, n)
    grid_spec = pltpu.PrefetchScalarGridSpec(
        num_scalar_prefetch=1,
        grid=(n // tm,),
        in_specs=[pl.BlockSpec((tm, D_MODEL), lambda i, pos: (i, 0)),
                  pl.BlockSpec((tm, LANES), lambda i, pos: (i, 0)),
                  pl.BlockSpec(memory_space=pl.ANY)],
        out_specs=pl.BlockSpec((tm, D_MODEL), lambda i, pos: (i, 0)),
        scratch_shapes=[pltpu.VMEM((tm * ROW_CHUNKS, LANES), F32), pltpu.VMEM((tm * ROW_CHUNKS, LANES), F32),
                        pltpu.SemaphoreType.DMA(())],
    )
    return pl.pallas_call(
        _combine_kernel,
        grid_spec=grid_spec,
        out_shape=jax.ShapeDtypeStruct((n, D_MODEL), F32),
        compiler_params=_cparams(1),
        name="combine",
    )(pos, h, rw, ys)


def _final_norm_kernel(h_ref, g_ref, o_ref):
    o_ref[...] = _rms(h_ref[...], g_ref[...])


def _final_norm(h, g):
    n = h.shape[0]
    tm = min(ROW_TILE, n)
    return pl.pallas_call(
        _final_norm_kernel,
        grid=(n // tm,),
        in_specs=[pl.BlockSpec((tm, D_MODEL), lambda i: (i, 0)), pl.BlockSpec(g.shape, lambda i: (0, 0))],
        out_specs=pl.BlockSpec((tm, D_MODEL), lambda i: (i, 0)),
        out_shape=jax.ShapeDtypeStruct((n, D_MODEL), F32),
        compiler_params=_cparams(1),
        name="final_norm",
    )(h, g)


def _moe(h, g, w_group, b_group, w_expert, b_expert, w_gate, w_up, w_down):
    n = h.shape[0]
    w_r = jnp.zeros((D_MODEL, LANES), F32).at[:, :N_GROUPS].set(w_group)
    w_r = w_r.at[:, N_GROUPS:N_GROUPS + N_EXPERTS].set(w_expert)
    b_r = jnp.zeros((1, LANES), F32).at[0, :N_GROUPS].set(b_group)
    b_r = b_r.at[0, N_GROUPS:N_GROUPS + N_EXPERTS].set(b_expert)
    w_hi = w_r.astype(MXU_DTYPE)
    w_lo = (w_r - w_hi.astype(F32)).astype(MXU_DTYPE)
    ri, rw, cnt = _router(h, g, w_hi, w_lo, b_r)

    counts = cnt[-1, N_GROUPS:N_GROUPS + N_EXPERTS]
    padded = (counts + EXPERT_TILE - 1) // EXPERT_TILE * EXPERT_TILE
    ends = jnp.cumsum(padded)
    starts = ends - padded
    experts = jnp.arange(N_EXPERTS, dtype=jnp.int32)
    seg_start = jnp.sum(jnp.where(ri[:, 0:2, None] == experts, starts, 0), axis=-1)
    pos = (seg_start + ri[:, 2:4]).reshape(-1).astype(jnp.int32)
    p_rows = 2 * n + N_EXPERTS * EXPERT_TILE
    tile_start = jnp.arange(p_rows // EXPERT_TILE, dtype=jnp.int32) * EXPERT_TILE
    tile_expert = jnp.minimum(jnp.sum(tile_start[:, None] >= ends[None, :], axis=1), N_EXPERTS - 1).astype(jnp.int32)
    tile_active = (tile_start < ends[-1]).astype(jnp.int32)

    xs = _dispatch(pos, h, g, jnp.zeros((p_rows * ROW_CHUNKS, LANES), F32))
    ys = _experts(tile_expert, tile_active, xs, w_gate.astype(MXU_DTYPE), w_up.astype(MXU_DTYPE),
                  w_down.astype(MXU_DTYPE))
    return _combine(pos, h, rw, ys)


def _even_weights(w_in):
    gate_cols = w_in[:, C_GATE:C_GATE + 3 * NSA_HEADS].reshape(D_MODEL, NSA_KV_HEADS, HEADS_PER_GROUP, 3)
    gate_cols = gate_cols.transpose(0, 1, 3, 2).reshape(D_MODEL, NSA_KV_HEADS, 3 * HEADS_PER_GROUP)
    gate_cols = jnp.pad(gate_cols, ((0, 0), (0, 0), (0, LANES - 3 * HEADS_PER_GROUP)))
    return jnp.concatenate([w_in[:, :C_GATE], gate_cols.reshape(D_MODEL, NSA_KV_HEADS * LANES)], axis=1)


def _block_diag_halves(w):
    blocks = w.shape[0] // 2
    out = jnp.zeros((2, blocks * w.shape[1], blocks * w.shape[2]), w.dtype)
    for j in range(2):
        for k in range(blocks):
            out = out.at[j, k * w.shape[1]:(k + 1) * w.shape[1], k * w.shape[2]:(k + 1) * w.shape[2]].set(
                w[j * blocks + k])
    return out


def _even_mixer(h, g, batch, cos_t, sin_t, w_in, w_out, conv_w, conv_b, w_r, b_r, w_i, b_i, lam,
                pos_k, w1_k, w2_k, pos_v, w1_v, w2_v):
    n = h.shape[0]
    t = n // batch
    xl, gg, qn, qr, kcv, ksr, vs, kwr, vw, gates = _even_proj(
        h, g, _even_weights(w_in).astype(MXU_DTYPE), cos_t, sin_t)

    y_lru = _lru(xl, gg, conv_w, conv_b[None, :], _block_diag_halves(w_r).astype(MXU_DTYPE),
                 _block_diag_halves(w_i).astype(MXU_DTYPE), b_r[None, :], b_i[None, :],
                 jax.nn.softplus(-lam)[None, :], batch)

    nchunk = t // CMP_STRIDE
    x16 = kcv.reshape(batch, nchunk, CMP_STRIDE, 2 * NSA_KV_HEADS, HEAD_DIM).transpose(0, 3, 1, 2, 4)
    x16 = x16.reshape(batch, 2 * NSA_KV_HEADS, nchunk, CMP_STRIDE * HEAD_DIM)
    half = CMP_STRIDE * HEAD_DIM
    pos = jnp.stack([pos_k.reshape(2, 1, half), pos_v.reshape(2, 1, half)])
    w1 = jnp.stack([w1_k.reshape(2, half, CMP_HIDDEN), w1_v.reshape(2, half, CMP_HIDDEN)]).astype(MXU_DTYPE)
    w2 = jnp.stack([w2_k, w2_v]).astype(MXU_DTYPE)
    kvc = _compress(x16, pos, w1, w2)

    tq = min(NSA_TQ, t)
    nq = t // tq
    rows = HEADS_PER_GROUP * tq

    def q_tiles(z):
        z = z.reshape(batch, nq, tq, NSA_KV_HEADS, HEADS_PER_GROUP, HEAD_DIM)
        return z.transpose(0, 3, 1, 5, 4, 2).reshape(batch, NSA_KV_HEADS, nq, HEAD_DIM, rows)

    def kv_heads(z):
        return z.reshape(batch, t, NSA_KV_HEADS, HEAD_DIM).transpose(0, 2, 1, 3)

    def kv_heads_t(z):
        return z.reshape(batch, t, NSA_KV_HEADS, HEAD_DIM).transpose(0, 2, 3, 1)

    n_gate = 3 * HEADS_PER_GROUP
    gates_t = gates.reshape(batch, nq, tq, NSA_KV_HEADS, LANES)[..., :n_gate]
    gates_t = gates_t.reshape(batch, nq, tq, NSA_KV_HEADS, 3, HEADS_PER_GROUP).transpose(0, 3, 1, 4, 5, 2)
    gates_t = jnp.pad(gates_t.reshape(batch, NSA_KV_HEADS, nq, 3, rows), ((0, 0),) * 3 + ((0, SUBLANES - 3), (0, 0)))
    vct = kvc[:, NSA_KV_HEADS:].transpose(0, 1, 3, 2)
    y_t = _nsa(q_tiles(qn), q_tiles(qr), kvc, vct, kv_heads(ksr), kv_heads_t(vs), kv_heads(kwr), kv_heads_t(vw),
               gates_t)
    y_nsa = y_t.reshape(batch, NSA_KV_HEADS, nq, HEAD_DIM, HEADS_PER_GROUP, tq).transpose(0, 2, 5, 1, 4, 3)
    w_out = w_out.astype(MXU_DTYPE)
    return _out_proj(h, y_lru, y_nsa.reshape(n, NSA_HEADS * HEAD_DIM), w_out[:LRU_WIDTH], w_out[LRU_WIDTH:])


def _rope_tables(positions):
    inv = ROPE_THETA ** (-jnp.arange(0, 2 * ROT_HALF, 2, dtype=F32) / (2 * ROT_HALF))
    ang = positions.reshape(-1).astype(F32)[:, None] * inv
    cos, sin = jnp.cos(ang), jnp.sin(ang)
    rest = HEAD_DIM - 2 * ROT_HALF
    cos_h = jnp.concatenate([cos, cos, jnp.ones((cos.shape[0], rest), F32)], axis=1)
    sin_h = jnp.concatenate([-sin, sin, jnp.zeros((cos.shape[0], rest), F32)], axis=1)
    reps = LANES // HEAD_DIM
    return jnp.tile(cos_h, (1, reps)), jnp.tile(sin_h, (1, reps))


def kernel(x, mem, positions, norm_mix, norm_xattn, norm_mem, norm_ffn, norm_final, even_w_in, even_w_out, lru_conv_w, lru_conv_b, lru_w_r, lru_b_r, lru_w_i, lru_b_i, lru_lambda, nsa_cmp_pos_k, nsa_cmp_w1_k, nsa_cmp_w2_k, nsa_cmp_pos_v, nsa_cmp_w1_v, nsa_cmp_w2_v, odd_w_in, odd_conv_w, odd_w_out, xa_wq, xa_wk, xa_wv, xa_wo, moe_w_group, moe_b_group, moe_w_expert, moe_b_expert, moe_w_gate, moe_w_up, moe_w_down):
    batch, t, d = x.shape
    n = batch * t
    depth = norm_mix.shape[0]
    cos_t, sin_t = _rope_tables(positions)
    h = x.reshape(n, d)
    mem2 = mem.reshape(-1, d)
    for layer in range(depth):
        g_mix = norm_mix[layer][None, :]
        if layer % 2 == 0:
            e = layer // 2
            h = _even_mixer(h, g_mix, batch, cos_t, sin_t, even_w_in[e], even_w_out[e], lru_conv_w[e],
                            lru_conv_b[e], lru_w_r[e], lru_b_r[e], lru_w_i[e], lru_b_i[e], lru_lambda[e],
                            nsa_cmp_pos_k[e], nsa_cmp_w1_k[e], nsa_cmp_w2_k[e],
                            nsa_cmp_pos_v[e], nsa_cmp_w1_v[e], nsa_cmp_w2_v[e])
        else:
            o = layer // 2
            h = _short_conv(h, g_mix, odd_w_in[o].astype(MXU_DTYPE), odd_conv_w[o],
                            odd_w_out[o].astype(MXU_DTYPE), batch)
        wkv = jnp.concatenate([xa_wk[layer], xa_wv[layer]], axis=1).astype(MXU_DTYPE)
        kv = _mem_kv(mem2, norm_mem[layer][None, :], wkv)
        h = _xattn(h, norm_xattn[layer][None, :], xa_wq[layer].astype(MXU_DTYPE), kv,
                   xa_wo[layer].astype(MXU_DTYPE), batch)
        h = _moe(h, norm_ffn[layer][None, :], moe_w_group[layer], moe_b_group[layer], moe_w_expert[layer],
                 moe_b_expert[layer], moe_w_gate[layer], moe_w_up[layer], moe_w_down[layer])
    return _final_norm(h, norm_final[None, :]).reshape(batch, t, d)
```

```python
import functools

import jax
import jax.numpy as jnp
import numpy as np
from jax import lax
from jax.experimental import pallas as pl
from jax.experimental.pallas import tpu as pltpu

F32 = jnp.float32
MXU_DTYPE = jnp.bfloat16

D_MODEL = 1024
LRU_WIDTH = 512
LRU_CONV = 4
LRU_C = 8.0
NSA_HEADS = 8
NSA_KV_HEADS = 2
HEADS_PER_GROUP = NSA_HEADS // NSA_KV_HEADS
HEAD_DIM = 64
CMP_STRIDE = 16
CMP_LEN = 32
CMP_HIDDEN = 128
SEL_BLOCK = 64
SEL_TOP = 16
WINDOW = 512
ROT_HALF = 8
ROPE_THETA = 500000.0
SC_CONV = 3
XA_HEADS = 4
XA_HEAD_DIM = 256
N_GROUPS = 4
EXPERTS_PER_GROUP = 4
N_EXPERTS = 16
D_EXPERT = 512
EPS = 1e-6
NEG = -1e30
LOG2E = 1.4426950408889634
FORCE = 1e9

LANES = 128
SUBLANES = 8
VMEM_LIMIT = 56 * 1024 * 1024

C_XL, C_GL, C_Q, C_KCV, C_KS, C_VS, C_KW, C_VW, C_GATE, C_END = (
    0, 512, 1024, 1536, 1792, 1920, 2048, 2176, 2304, 2560)

ROW_TILE = 512
NSA_TQ = 128
NSA_TK = 512
EXPERT_TILE = 256
DMA_UNROLL = 8
ROW_CHUNKS = D_MODEL // LANES


def _cparams(n_axes):
    return pltpu.CompilerParams(dimension_semantics=("arbitrary",) * n_axes,
                                vmem_limit_bytes=VMEM_LIMIT)


def _mm(a, b):
    return jnp.dot(a, b, preferred_element_type=F32)


def _mm_nt(a, b):
    return lax.dot_general(a, b, (((1,), (1,)), ((), ())), preferred_element_type=F32)


def _rms(x, g):
    return x * lax.rsqrt(jnp.mean(x * x, axis=-1, keepdims=True) + EPS) * g


def _shift_rows(xx, s, rows):
    if s == 0:
        return xx[SUBLANES:SUBLANES + rows]
    return pltpu.roll(xx, s, 0)[SUBLANES:SUBLANES + rows]


def _rope(x, cos_t, sin_t):
    width = x.shape[-1]
    lane = lax.broadcasted_iota(jnp.int32, x.shape, 1) % HEAD_DIM
    partner = jnp.where(lane < ROT_HALF, pltpu.roll(x, width - ROT_HALF, 1), pltpu.roll(x, ROT_HALF, 1))
    return x * cos_t + partner * sin_t


def _even_proj_kernel(h_ref, g_ref, w_ref, cos_ref, sin_ref,
                      xl_ref, gg_ref, qn_ref, qr_ref, kcv_ref, ksr_ref, vs_ref, kwr_ref, vw_ref,
                      gate_ref):
    xn = _rms(h_ref[...], g_ref[...]).astype(MXU_DTYPE)

    def proj(c0, c1):
        return _mm(xn, w_ref[:, c0:c1])

    cos_t = cos_ref[...]
    sin_t = sin_ref[...]
    xl_ref[...] = proj(C_XL, C_GL)
    gg_ref[...] = jax.nn.gelu(proj(C_GL, C_Q))
    q = proj(C_Q, C_KCV) * (HEAD_DIM ** -0.5)
    reps = (C_KCV - C_Q) // LANES
    cos_q = jnp.concatenate([cos_t] * reps, axis=1)
    sin_q = jnp.concatenate([sin_t] * reps, axis=1)
    qn_ref[...] = q.astype(qn_ref.dtype)
    qr_ref[...] = (_rope(q, cos_q, sin_q) * LOG2E).astype(qr_ref.dtype)
    kcv_ref[...] = proj(C_KCV, C_KS)
    ksr_ref[...] = _rope(proj(C_KS, C_VS), cos_t, sin_t).astype(ksr_ref.dtype)
    vs_ref[...] = proj(C_VS, C_KW).astype(vs_ref.dtype)
    kwr_ref[...] = _rope(proj(C_KW, C_VW), cos_t, sin_t).astype(kwr_ref.dtype)
    vw_ref[...] = proj(C_VW, C_GATE).astype(vw_ref.dtype)
    gate_ref[...] = jax.nn.sigmoid(proj(C_GATE, C_END))


def _even_proj(h, g, w, cos_t, sin_t):
    n = h.shape[0]
    tm = min(ROW_TILE, n)
    row = lambda c: pl.BlockSpec((tm, c), lambda i: (i, 0))
    full = lambda a: pl.BlockSpec(a.shape, lambda i: (0,) * a.ndim)
    widths = [(512, F32), (512, F32), (512, MXU_DTYPE), (512, MXU_DTYPE), (256, F32),
              (128, MXU_DTYPE), (128, MXU_DTYPE), (128, MXU_DTYPE), (128, MXU_DTYPE), (256, F32)]
    return pl.pallas_call(
        _even_proj_kernel,
        grid=(n // tm,),
        in_specs=[row(D_MODEL), full(g), full(w), row(LANES), row(LANES)],
        out_specs=[row(c) for c, _ in widths],
        out_shape=[jax.ShapeDtypeStruct((n, c), dt) for c, dt in widths],
        compiler_params=_cparams(1),
        name="even_proj",
    )(h, g, w, cos_t, sin_t)


def _lru_kernel(xl_ref, gg_ref, cw_ref, cb_ref, wr_ref, wi_ref, br_ref, bi_ref, sp_ref,
                y_ref, tail_ref, h_ref, a_scr, u_scr):
    tt = xl_ref.shape[0]

    @pl.when(pl.program_id(1) == 0)
    def _():
        tail_ref[...] = jnp.zeros_like(tail_ref)
        h_ref[...] = jnp.zeros_like(h_ref)

    x = xl_ref[...]
    xx = jnp.concatenate([tail_ref[...], x], axis=0)
    tail_ref[...] = x[tt - SUBLANES:tt]
    xc = cb_ref[...] + sum(cw_ref[k:k + 1, :] * _shift_rows(xx, LRU_CONV - 1 - k, tt)
                           for k in range(LRU_CONV))
    xcb = xc.astype(MXU_DTYPE)
    half = LRU_WIDTH // 2
    r_lin = jnp.concatenate([_mm(xcb[:, j * half:(j + 1) * half], wr_ref[j]) for j in range(2)], axis=1)
    i_lin = jnp.concatenate([_mm(xcb[:, j * half:(j + 1) * half], wi_ref[j]) for j in range(2)], axis=1)
    r = jax.nn.sigmoid(r_lin + br_ref[...])
    i = jax.nn.sigmoid(i_lin + bi_ref[...])
    log_a = -LRU_C * r * sp_ref[...]
    a = jnp.exp(log_a)
    u = jnp.sqrt(jnp.tanh(-log_a) * (1.0 + a * a)) * (i * xc)

    r8 = lax.broadcasted_iota(jnp.int32, a.shape, 0) % SUBLANES
    for s in (1, 2, 4):
        keep = r8 >= s
        u = jnp.where(keep, a * pltpu.roll(u, s, 0) + u, u)
        a = jnp.where(keep, a * pltpu.roll(a, s, 0), a)
    a_scr[...] = a
    u_scr[...] = u

    def body(gidx, h):
        r0 = pl.multiple_of(gidx * SUBLANES, SUBLANES)
        out = a_scr[pl.ds(r0, SUBLANES), :] * h + u_scr[pl.ds(r0, SUBLANES), :]
        u_scr[pl.ds(r0, SUBLANES), :] = out
        return out[SUBLANES - 1:SUBLANES, :]

    h_ref[...] = lax.fori_loop(0, tt // SUBLANES, body, h_ref[...])
    y_ref[...] = (u_scr[...] * gg_ref[...]).astype(y_ref.dtype)


def _lru(xl, gg, cw, cb, wr, wi, br, bi, sp, batch):
    n = xl.shape[0]
    t = n // batch
    tt = min(ROW_TILE, t)
    nt = t // tt
    row = pl.BlockSpec((tt, LRU_WIDTH), lambda b, i: (b * nt + i, 0))
    full = lambda a: pl.BlockSpec(a.shape, lambda b, i: (0,) * a.ndim)
    return pl.pallas_call(
        _lru_kernel,
        grid=(batch, nt),
        in_specs=[row, row] + [full(a) for a in (cw, cb, wr, wi, br, bi, sp)],
        out_specs=row,
        out_shape=jax.ShapeDtypeStruct((n, LRU_WIDTH), MXU_DTYPE),
        scratch_shapes=[pltpu.VMEM((SUBLANES, LRU_WIDTH), F32), pltpu.VMEM((1, LRU_WIDTH), F32),
                        pltpu.VMEM((tt, LRU_WIDTH), F32), pltpu.VMEM((tt, LRU_WIDTH), F32)],
        compiler_params=_cparams(2),
        name="lru",
    )(xl, gg, cw, cb, wr, wi, br, bi, sp)


def _compress_kernel(x_ref, pos_ref, w1_ref, w2_ref, o_ref):
    x = x_ref[0, 0]
    nchunk = x.shape[0]
    lo = (x + pos_ref[0, 0]).astype(MXU_DTYPE)
    hi = (x + pos_ref[0, 1]).astype(MXU_DTYPE)
    p_lo = _mm(lo, w1_ref[0, 0])
    p_hi = _mm(hi, w1_ref[0, 1])
    a = p_lo + pltpu.roll(p_hi, nchunk - 1, 0)
    o_ref[0, 0] = _mm(jax.nn.gelu(a).astype(MXU_DTYPE), w2_ref[0]).astype(o_ref.dtype)


def _compress(x16, pos, w1, w2):
    b, _, nchunk, width = x16.shape
    return pl.pallas_call(
        _compress_kernel,
        grid=(b, 2 * NSA_KV_HEADS),
        in_specs=[pl.BlockSpec((1, 1, nchunk, width), lambda i, j: (i, j, 0, 0)),
                  pl.BlockSpec((1, 2, 1, width), lambda i, j: (j // NSA_KV_HEADS, 0, 0, 0)),
                  pl.BlockSpec((1, 2, width, CMP_HIDDEN), lambda i, j: (j // NSA_KV_HEADS, 0, 0, 0)),
                  pl.BlockSpec((1, CMP_HIDDEN, HEAD_DIM), lambda i, j: (j // NSA_KV_HEADS, 0, 0))],
        out_specs=pl.BlockSpec((1, 1, nchunk, HEAD_DIM), lambda i, j: (i, j, 0, 0)),
        out_shape=jax.ShapeDtypeStruct((b, 2 * NSA_KV_HEADS, nchunk, HEAD_DIM), MXU_DTYPE),
        compiler_params=_cparams(2),
        name="compress",
    )(x16, pos, w1, w2)


def _nsa_kernel(qn_ref, qr_ref, kc_ref, vct_ref, ks_ref, vst_ref, kw_ref, vwt_ref, gate_ref,
                y_ref, chosen_scr, acc_scr, accw_scr):
    rows = qn_ref.shape[4]
    tq = rows // HEADS_PER_GROUP
    t = ks_ref.shape[2]
    nc = kc_ref.shape[2]
    nsel = t // SEL_BLOCK
    n_top = min(SEL_TOP, nsel)
    tk = NSA_TK
    t0 = pl.program_id(2) * tq
    qn = qn_ref[0, 0, 0]
    qr = qr_ref[0, 0, 0]

    s = _mm(kc_ref[0, 0], qn)
    tq_row = t0 + lax.broadcasted_iota(jnp.int32, (nc, rows), 1) % tq
    cmp_end = lax.broadcasted_iota(jnp.int32, (nc, rows), 0) * CMP_STRIDE + (CMP_LEN - 1)
    valid = cmp_end <= tq_row
    s = jnp.where(valid, s, NEG)
    e = jnp.where(valid, jnp.exp(s - jnp.max(s, axis=0, keepdims=True)), 0.0)
    den = jnp.sum(e, axis=0, keepdims=True)
    p = e * (1.0 / jnp.where(den > 0.0, den, 1.0))
    o_cmp = _mm(vct_ref[0, 0], p.astype(MXU_DTYPE))

    psum = p[:, 0:tq]
    for hh in range(1, HEADS_PER_GROUP):
        psum = psum + p[:, hh * tq:(hh + 1) * tq]
    p_hi = psum.astype(MXU_DTYPE)
    p_lo = (psum - p_hi.astype(F32)).astype(MXU_DTYPE)
    cj = lax.broadcasted_iota(jnp.int32, (nsel, nc), 0)
    cn = lax.broadcasted_iota(jnp.int32, (nsel, nc), 1)
    ratio = SEL_BLOCK // CMP_STRIDE
    cover = jnp.where((cn >= ratio * cj - (CMP_LEN // CMP_STRIDE - 1)) & (cn <= ratio * cj + ratio - 1)
                      & (cn < nc - 1), 1.0, 0.0).astype(MXU_DTYPE)
    imp = _mm(cover, p_hi) + _mm(cover, p_lo)

    blk = lax.broadcasted_iota(jnp.int32, (nsel, tq), 0)
    tq_col = t0 + lax.broadcasted_iota(jnp.int32, (nsel, tq), 1)
    cur = tq_col // SEL_BLOCK
    forced = (blk == 0) | (blk == cur) | (blk == cur - 1)
    causal = blk * SEL_BLOCK <= tq_col
    score0 = jnp.where(causal, jnp.where(forced, FORCE, imp), NEG)
    blk_f = blk.astype(F32)

    def pick_one(_, carry):
        score, chosen = carry
        best = jnp.max(score, axis=0, keepdims=True)
        first = jnp.min(jnp.where(score == best, blk_f, float(nsel)), axis=0, keepdims=True)
        hit = blk_f == first
        return jnp.where(hit, -jnp.inf, score), jnp.where(hit, 1.0, chosen)

    _, chosen = lax.fori_loop(0, n_top, pick_one, (score0, jnp.zeros((nsel, tq), F32)))
    chosen_scr[...] = chosen

    def tile_update(k_ref, vt_ref, acc_ref, kt, bias, m_old, l_old):
        k0 = pl.multiple_of(kt * tk, tk)
        sc = _mm(k_ref[0, 0, pl.ds(k0, tk), :], qr) + jnp.concatenate([bias] * HEADS_PER_GROUP, axis=1)
        m_new = jnp.maximum(m_old, jnp.max(sc, axis=0, keepdims=True))
        alpha = jnp.exp2(m_old - m_new)
        pe = jnp.exp2(sc - m_new)
        l_new = alpha * l_old + jnp.sum(pe, axis=0, keepdims=True)
        acc_ref[...] = alpha * acc_ref[...] + _mm(vt_ref[0, 0, :, pl.ds(k0, tk)], pe.astype(MXU_DTYPE))
        return m_new, l_new

    expand = jnp.where(lax.broadcasted_iota(jnp.int32, (tk, SUBLANES), 0) // SEL_BLOCK
                       == lax.broadcasted_iota(jnp.int32, (tk, SUBLANES), 1), 1.0, 0.0)

    def key_minus_query(kt):
        return (kt * tk - t0 + lax.broadcasted_iota(jnp.int32, (tk, tq), 0)
                - lax.broadcasted_iota(jnp.int32, (tk, tq), 1))

    def sel_bias(kt, diagonal):
        grp = pl.multiple_of(kt * SUBLANES, SUBLANES)
        bias = _mm(expand, (chosen_scr[pl.ds(grp, SUBLANES), :] - 1.0) * (-NEG))
        return jnp.where(key_minus_query(kt) <= 0, bias, NEG) if diagonal else bias

    def win_bias(kt, diagonal):
        d = key_minus_query(kt)
        inside = (d > -WINDOW) & (d <= 0) if diagonal else d > -WINDOW
        return jnp.where(inside, 0.0, NEG)

    def sel_only(kt, carry):
        m_s, l_s, m_w, l_w = carry
        m_s, l_s = tile_update(ks_ref, vst_ref, acc_scr, kt, sel_bias(kt, False), m_s, l_s)
        return m_s, l_s, m_w, l_w

    def both(kt, carry, diagonal=False):
        m_s, l_s, m_w, l_w = carry
        m_s, l_s = tile_update(ks_ref, vst_ref, acc_scr, kt, sel_bias(kt, diagonal), m_s, l_s)
        m_w, l_w = tile_update(kw_ref, vwt_ref, accw_scr, kt, win_bias(kt, diagonal), m_w, l_w)
        return m_s, l_s, m_w, l_w

    kt_last = (t0 + tq - 1) // tk
    win_lo = jnp.maximum(t0 - (WINDOW - 1), 0) // tk
    acc_scr[...] = jnp.zeros_like(acc_scr)
    accw_scr[...] = jnp.zeros_like(accw_scr)
    lowest = jnp.full((1, rows), NEG, F32)
    zero = jnp.zeros((1, rows), F32)
    carry = lax.fori_loop(0, win_lo, sel_only, (lowest, zero, lowest, zero))
    carry = lax.fori_loop(win_lo, kt_last, both, carry)
    _, l_s, _, l_w = both(kt_last, carry, diagonal=True)
    o_sel = acc_scr[...] * (1.0 / l_s)
    o_win = accw_scr[...] * (1.0 / l_w)

    gate = gate_ref[0, 0, 0]
    y_ref[0, 0, 0] = (gate[0:1] * o_cmp + gate[1:2] * o_sel + gate[2:3] * o_win).astype(y_ref.dtype)


def _nsa(qn_t, qr_t, kvc, vct, ks, vst, kw, vwt, gates_t):
    b, _, nq, _, rows = qn_t.shape
    t = ks.shape[2]
    nc = kvc.shape[2]
    assert NSA_TK == SEL_BLOCK * SUBLANES and t % NSA_TK == 0
    q_spec = pl.BlockSpec((1, 1, 1, HEAD_DIM, rows), lambda i, g, j: (i, g, j, 0, 0))
    k_spec = pl.BlockSpec((1, 1, t, HEAD_DIM), lambda i, g, j: (i, g, 0, 0))
    vt_spec = pl.BlockSpec((1, 1, HEAD_DIM, t), lambda i, g, j: (i, g, 0, 0))
    return pl.pallas_call(
        _nsa_kernel,
        grid=(b, NSA_KV_HEADS, nq),
        in_specs=[q_spec, q_spec,
                  pl.BlockSpec((1, 1, nc, HEAD_DIM), lambda i, g, j: (i, g, 0, 0)),
                  pl.BlockSpec((1, 1, HEAD_DIM, nc), lambda i, g, j: (i, g, 0, 0)),
                  k_spec, vt_spec, k_spec, vt_spec,
                  pl.BlockSpec((1, 1, 1, SUBLANES, rows), lambda i, g, j: (i, g, j, 0, 0))],
        out_specs=q_spec,
        out_shape=jax.ShapeDtypeStruct(qn_t.shape, MXU_DTYPE),
        scratch_shapes=[pltpu.VMEM((t // SEL_BLOCK, rows // HEADS_PER_GROUP), F32),
                        pltpu.VMEM((HEAD_DIM, rows), F32), pltpu.VMEM((HEAD_DIM, rows), F32)],
        compiler_params=_cparams(3),
        name="nsa",
    )(qn_t, qr_t, kvc, vct, ks, vst, kw, vwt, gates_t)


def _out_proj_kernel(h_ref, a_ref, b_ref, wa_ref, wb_ref, o_ref):
    o_ref[...] = h_ref[...] + _mm(a_ref[...], wa_ref[...]) + _mm(b_ref[...], wb_ref[...])


def _out_proj(h, a, b, wa, wb):
    n = h.shape[0]
    tm = min(ROW_TILE, n)
    full = lambda x: pl.BlockSpec(x.shape, lambda i: (0,) * x.ndim)
    return pl.pallas_call(
        _out_proj_kernel,
        grid=(n // tm,),
        in_specs=[pl.BlockSpec((tm, D_MODEL), lambda i: (i, 0)),
                  pl.BlockSpec((tm, a.shape[1]), lambda i: (i, 0)),
                  pl.BlockSpec((tm, b.shape[1]), lambda i: (i, 0)), full(wa), full(wb)],
        out_specs=pl.BlockSpec((tm, D_MODEL), lambda i: (i, 0)),
        out_shape=jax.ShapeDtypeStruct((n, D_MODEL), F32),
        compiler_params=_cparams(1),
        name="out_proj",
    )(h, a, b, wa, wb)


def _short_conv_kernel(h_ref, g_ref, win_ref, cw_ref, wout_ref, o_ref, tail_ref):
    tt = h_ref.shape[0]

    @pl.when(pl.program_id(1) == 0)
    def _():
        tail_ref[...] = jnp.zeros_like(tail_ref)

    h = h_ref[...]
    xn = _rms(h, g_ref[...]).astype(MXU_DTYPE)
    b_g = _mm(xn, win_ref[:, 0:D_MODEL])
    cv = _mm(xn, win_ref[:, D_MODEL:2 * D_MODEL]) * _mm(xn, win_ref[:, 2 * D_MODEL:3 * D_MODEL])
    xx = jnp.concatenate([tail_ref[...], cv], axis=0)
    tail_ref[...] = cv[tt - SUBLANES:tt]
    conv = sum(cw_ref[k:k + 1, :] * _shift_rows(xx, SC_CONV - 1 - k, tt) for k in range(SC_CONV))
    o_ref[...] = h + _mm((b_g * conv).astype(MXU_DTYPE), wout_ref[...])


def _short_conv(h, g, w_in, cw, w_out, batch):
    n = h.shape[0]
    t = n // batch
    tt = min(ROW_TILE, t)
    nt = t // tt
    row = pl.BlockSpec((tt, D_MODEL), lambda b, i: (b * nt + i, 0))
    full = lambda a: pl.BlockSpec(a.shape, lambda b, i: (0,) * a.ndim)
    return pl.pallas_call(
        _short_conv_kernel,
        grid=(batch, nt),
        in_specs=[row, full(g), full(w_in), full(cw), full(w_out)],
        out_specs=row,
        out_shape=jax.ShapeDtypeStruct((n, D_MODEL), F32),
        scratch_shapes=[pltpu.VMEM((SUBLANES, D_MODEL), F32)],
        compiler_params=_cparams(2),
        name="short_conv",
    )(h, g, w_in, cw, w_out)


def _mem_kv_kernel(m_ref, g_ref, w_ref, o_ref):
    o_ref[...] = _mm(_rms(m_ref[...], g_ref[...]).astype(MXU_DTYPE), w_ref[...]).astype(o_ref.dtype)


def _mem_kv(mem, g, wkv):
    n = mem.shape[0]
    tm = min(ROW_TILE, n)
    tn = 1024
    return pl.pallas_call(
        _mem_kv_kernel,
        grid=(n // tm, wkv.shape[1] // tn),
        in_specs=[pl.BlockSpec((tm, D_MODEL), lambda i, j: (i, 0)),
                  pl.BlockSpec(g.shape, lambda i, j: (0, 0)),
                  pl.BlockSpec((D_MODEL, tn), lambda i, j: (0, j))],
        out_specs=pl.BlockSpec((tm, tn), lambda i, j: (i, j)),
        out_shape=jax.ShapeDtypeStruct((n, wkv.shape[1]), MXU_DTYPE),
        compiler_params=_cparams(2),
        name="mem_kv",
    )(mem, g, wkv)


def _xattn_kernel(h_ref, g_ref, wq_ref, kv_ref, wo_ref, o_ref):
    h = h_ref[...]
    xn = _rms(h, g_ref[...]).astype(MXU_DTYPE)
    q = (_mm(xn, wq_ref[...]) * (XA_HEAD_DIM ** -0.5)).astype(MXU_DTYPE)
    width = XA_HEADS * XA_HEAD_DIM
    outs = []
    for hd in range(XA_HEADS):
        sl = slice(hd * XA_HEAD_DIM, (hd + 1) * XA_HEAD_DIM)
        s = _mm_nt(q[:, sl], kv_ref[:, sl])
        e = jnp.exp(s - jnp.max(s, axis=-1, keepdims=True))
        p = e / jnp.sum(e, axis=-1, keepdims=True)
        outs.append(_mm(p.astype(MXU_DTYPE), kv_ref[:, width + hd * XA_HEAD_DIM:width + (hd + 1) * XA_HEAD_DIM]))
    o = jnp.concatenate(outs, axis=1).astype(MXU_DTYPE)
    o_ref[...] = h + _mm(o, wo_ref[...])


def _xattn(h, g, wq, kv, wo, batch):
    n = h.shape[0]
    t = n // batch
    tm = min(ROW_TILE, t)
    nt = t // tm
    mlen = kv.shape[0] // batch
    full = lambda a: pl.BlockSpec(a.shape, lambda b, i: (0,) * a.ndim)
    row = pl.BlockSpec((tm, D_MODEL), lambda b, i: (b * nt + i, 0))
    return pl.pallas_call(
        _xattn_kernel,
        grid=(batch, nt),
        in_specs=[row, full(g), full(wq), pl.BlockSpec((mlen, kv.shape[1]), lambda b, i: (b, 0)), full(wo)],
        out_specs=row,
        out_shape=jax.ShapeDtypeStruct((n, D_MODEL), F32),
        compiler_params=_cparams(2),
        name="xattn",
    )(h, g, wq, kv, wo)


def _router_kernel(h_ref, g_ref, whi_ref, wlo_ref, b_ref, ri_ref, rw_ref, cnt_ref, carry_ref):
    tm = h_ref.shape[0]

    @pl.when(pl.program_id(0) == 0)
    def _():
        carry_ref[...] = jnp.zeros_like(carry_ref)

    xn = _rms(h_ref[...], g_ref[...])
    x_hi = xn.astype(MXU_DTYPE)
    x_lo = (xn - x_hi.astype(F32)).astype(MXU_DTYPE)
    logits = _mm(x_hi, whi_ref[...]) + (_mm(x_lo, whi_ref[...]) + _mm(x_hi, wlo_ref[...])) + b_ref[...]

    lane = lax.broadcasted_iota(jnp.int32, logits.shape, 1)
    lane_f = lane.astype(F32)
    none = float(LANES)
    is_g = lane < N_GROUPS
    g_max = jnp.max(jnp.where(is_g, logits, -jnp.inf), axis=-1, keepdims=True)
    g_sum = jnp.sum(jnp.where(is_g, jnp.exp(logits - g_max), 0.0), axis=-1, keepdims=True)
    g_top = 1.0 / g_sum
    g_idx = jnp.min(jnp.where(is_g & (logits == g_max), lane_f, none), axis=-1, keepdims=True)
    first = N_GROUPS + EXPERTS_PER_GROUP * g_idx
    in_g = (lane_f >= first) & (lane_f < first + EXPERTS_PER_GROUP)
    e1 = jnp.max(jnp.where(in_g, logits, -jnp.inf), axis=-1, keepdims=True)
    i1 = jnp.min(jnp.where(in_g & (logits == e1), lane_f, none), axis=-1, keepdims=True)
    rest = in_g & (lane_f != i1)
    e2 = jnp.max(jnp.where(rest, logits, -jnp.inf), axis=-1, keepdims=True)
    i2 = jnp.min(jnp.where(rest & (logits == e2), lane_f, none), axis=-1, keepdims=True)
    ratio = jnp.exp(e2 - e1)
    w1 = g_top / (1.0 + ratio)
    w2 = g_top * ratio / (1.0 + ratio)

    onehot = jnp.where((lane_f == i1) | (lane_f == i2), 1.0, 0.0)
    tri = jnp.where(lax.broadcasted_iota(jnp.int32, (tm, tm), 0) >= lax.broadcasted_iota(jnp.int32, (tm, tm), 1),
                    1.0, 0.0).astype(MXU_DTYPE)
    incl = _mm(tri, onehot.astype(MXU_DTYPE))
    before = incl - onehot + carry_ref[...]
    carry_ref[...] = carry_ref[...] + incl[tm - 1:tm, :]
    rank1 = jnp.sum(jnp.where(lane_f == i1, before, 0.0), axis=-1, keepdims=True)
    rank2 = jnp.sum(jnp.where(lane_f == i2, before, 0.0), axis=-1, keepdims=True)

    ri = jnp.where(lane == 0, i1 - N_GROUPS,
                   jnp.where(lane == 1, i2 - N_GROUPS, jnp.where(lane == 2, rank1, jnp.where(lane == 3, rank2, 0.0))))
    ri_ref[...] = ri.astype(jnp.int32)
    rw_ref[...] = jnp.where(lane == 0, w1, jnp.where(lane == 1, w2, 0.0))
    cnt_ref[...] = jnp.broadcast_to(carry_ref[...], cnt_ref.shape).astype(jnp.int32)


def _router(h, g, w_hi, w_lo, bias):
    n = h.shape[0]
    tm = min(ROW_TILE, n)
    full = lambda a: pl.BlockSpec(a.shape, lambda i: (0,) * a.ndim)
    return pl.pallas_call(
        _router_kernel,
        grid=(n // tm,),
        in_specs=[pl.BlockSpec((tm, D_MODEL), lambda i: (i, 0)), full(g), full(w_hi), full(w_lo), full(bias)],
        out_specs=[pl.BlockSpec((tm, LANES), lambda i: (i, 0)), pl.BlockSpec((tm, LANES), lambda i: (i, 0)),
                   pl.BlockSpec((SUBLANES, LANES), lambda i: (i, 0))],
        out_shape=[jax.ShapeDtypeStruct((n, LANES), jnp.int32), jax.ShapeDtypeStruct((n, LANES), F32),
                   jax.ShapeDtypeStruct((n // tm * SUBLANES, LANES), jnp.int32)],
        scratch_shapes=[pltpu.VMEM((1, LANES), F32)],
        compiler_params=_cparams(1),
        name="router",
    )(h, g, w_hi, w_lo, bias)


def _row_copy(src_ref, src_row, dst_ref, dst_row, sem):
    src = src_ref.at[pl.ds(pl.multiple_of(src_row * ROW_CHUNKS, ROW_CHUNKS), ROW_CHUNKS)]
    dst = dst_ref.at[pl.ds(pl.multiple_of(dst_row * ROW_CHUNKS, ROW_CHUNKS), ROW_CHUNKS)]
    return pltpu.make_async_copy(src, dst, sem)


def _to_token_tiles(ref, x):
    rows = x.shape[0]
    for c in range(ROW_CHUNKS):
        ref[pl.ds(c, rows, stride=ROW_CHUNKS), :] = x[:, c * LANES:(c + 1) * LANES]


def _from_token_tiles(ref):
    rows = ref.shape[0] // ROW_CHUNKS
    return jnp.concatenate([ref[pl.ds(c, rows, stride=ROW_CHUNKS), :] for c in range(ROW_CHUNKS)], axis=1)


def _dispatch_kernel(pos_ref, fill_ref, h_ref, g_ref, xs_ref, xn_scr, zero_scr, sem, fill_sem):
    tm = h_ref.shape[0]

    @pl.when(pl.program_id(0) == 0)
    def _():
        zero_scr[...] = jnp.zeros_like(zero_scr)
        for e in range(fill_ref.shape[0]):
            first = pl.multiple_of(fill_ref[e] * ROW_CHUNKS, ROW_CHUNKS)
            fill = pltpu.make_async_copy(zero_scr, xs_ref.at[pl.ds(first, EXPERT_TILE * ROW_CHUNKS)], fill_sem)
            fill.start()
            fill.wait()

    base = pl.program_id(0) * (2 * tm)
    _to_token_tiles(xn_scr, _rms(h_ref[...], g_ref[...]))

    def issue(r, _):
        _row_copy(xn_scr, r, xs_ref, pos_ref[base + 2 * r], sem).start()
        _row_copy(xn_scr, r, xs_ref, pos_ref[base + 2 * r + 1], sem).start()
        return 0

    lax.fori_loop(0, tm, issue, 0, unroll=DMA_UNROLL)
    for _ in range(2):
        pltpu.make_async_copy(xn_scr, xs_ref.at[pl.ds(0, tm * ROW_CHUNKS)], sem).wait()


def _dispatch(pos, fill_start, h, g, p_rows):
    n = h.shape[0]
    tm = min(ROW_TILE, n)
    grid_spec = pltpu.PrefetchScalarGridSpec(
        num_scalar_prefetch=2,
        grid=(n // tm,),
        in_specs=[pl.BlockSpec((tm, D_MODEL), lambda i, pos, fill: (i, 0)),
                  pl.BlockSpec(g.shape, lambda i, pos, fill: (0, 0))],
        out_specs=pl.BlockSpec(memory_space=pl.ANY),
        scratch_shapes=[pltpu.VMEM((tm * ROW_CHUNKS, LANES), F32),
                        pltpu.VMEM((EXPERT_TILE * ROW_CHUNKS, LANES), F32),
                        pltpu.SemaphoreType.DMA(()), pltpu.SemaphoreType.DMA(())],
    )
    return pl.pallas_call(
        _dispatch_kernel,
        grid_spec=grid_spec,
        out_shape=jax.ShapeDtypeStruct(((p_rows + EXPERT_TILE) * ROW_CHUNKS, LANES), F32),
        compiler_params=_cparams(1),
        name="dispatch",
    )(pos, fill_start, h, g)


def _expert_kernel(te_ref, ta_ref, tx_ref, x_ref, wg_ref, wu_ref, wd_ref, y_ref):
    j = pl.program_id(0)

    @pl.when(ta_ref[j] == 1)
    def _():
        x = _from_token_tiles(x_ref).astype(MXU_DTYPE)
        hid = jax.nn.silu(_mm(x, wg_ref[0])) * _mm(x, wu_ref[0])
        _to_token_tiles(y_ref, _mm(hid.astype(MXU_DTYPE), wd_ref[0]))

    @pl.when(ta_ref[j] == 0)
    def _():
        y_ref[...] = jnp.zeros_like(y_ref)


def _experts(tile_expert, tile_active, tile_x, xs, wg, wu, wd, p_rows):
    grid_spec = pltpu.PrefetchScalarGridSpec(
        num_scalar_prefetch=3,
        grid=(p_rows // EXPERT_TILE,),
        in_specs=[pl.BlockSpec((EXPERT_TILE * ROW_CHUNKS, LANES), lambda j, te, ta, tx: (tx[j], 0)),
                  pl.BlockSpec((1, D_MODEL, D_EXPERT), lambda j, te, ta, tx: (te[j], 0, 0)),
                  pl.BlockSpec((1, D_MODEL, D_EXPERT), lambda j, te, ta, tx: (te[j], 0, 0)),
                  pl.BlockSpec((1, D_EXPERT, D_MODEL), lambda j, te, ta, tx: (te[j], 0, 0))],
        out_specs=pl.BlockSpec((EXPERT_TILE * ROW_CHUNKS, LANES), lambda j, te, ta, tx: (j, 0)),
    )
    return pl.pallas_call(
        _expert_kernel,
        grid_spec=grid_spec,
        out_shape=jax.ShapeDtypeStruct((p_rows * ROW_CHUNKS, LANES), F32),
        compiler_params=_cparams(1),
        name="experts",
    )(tile_expert, tile_active, tile_x, xs, wg, wu, wd)


def _combine_kernel(pos_ref, h_ref, rw_ref, g_ref, ys_ref, o_ref, y1_scr, y2_scr, sem, *, normalize):
    tm = h_ref.shape[0]
    base = pl.program_id(0) * (2 * tm)

    def issue(r, _):
        _row_copy(ys_ref, pos_ref[base + 2 * r], y1_scr, r, sem).start()
        _row_copy(ys_ref, pos_ref[base + 2 * r + 1], y2_scr, r, sem).start()
        return 0

    lax.fori_loop(0, tm, issue, 0, unroll=DMA_UNROLL)
    for buf in (y1_scr, y2_scr):
        pltpu.make_async_copy(ys_ref.at[pl.ds(0, tm * ROW_CHUNKS)], buf, sem).wait()
    rw = rw_ref[...]
    out = h_ref[...] + (rw[:, 0:1] * _from_token_tiles(y1_scr) + rw[:, 1:2] * _from_token_tiles(y2_scr))
    o_ref[...] = _rms(out, g_ref[...]) if normalize else out


def _combine(pos, h, rw, g_final, ys, normalize):
    n = h.shape[0]
    tm = min(ROW_TILE, n)
    grid_spec = pltpu.PrefetchScalarGridSpec(
        num_scalar_prefetch=1,
        grid=(n // tm,),
        in_specs=[pl.BlockSpec((tm, D_MODEL), lambda i, pos: (i, 0)),
                  pl.BlockSpec((tm, LANES), lambda i, pos: (i, 0)),
                  pl.BlockSpec(g_final.shape, lambda i, pos: (0, 0)),
                  pl.BlockSpec(memory_space=pl.ANY)],
        out_specs=pl.BlockSpec((tm, D_MODEL), lambda i, pos: (i, 0)),
        scratch_shapes=[pltpu.VMEM((tm * ROW_CHUNKS, LANES), F32), pltpu.VMEM((tm * ROW_CHUNKS, LANES), F32),
                        pltpu.SemaphoreType.DMA(())],
    )
    return pl.pallas_call(
        functools.partial(_combine_kernel, normalize=normalize),
        grid_spec=grid_spec,
        out_shape=jax.ShapeDtypeStruct((n, D_MODEL), F32),
        compiler_params=_cparams(1),
        name="combine",
    )(pos, h, rw, g_final, ys)


def _moe(h, g, w_group, b_group, w_expert, b_expert, w_gate, w_up, w_down, g_final, last):
    n = h.shape[0]
    w_r = jnp.zeros((D_MODEL, LANES), F32).at[:, :N_GROUPS].set(w_group)
    w_r = w_r.at[:, N_GROUPS:N_GROUPS + N_EXPERTS].set(w_expert)
    b_r = jnp.zeros((1, LANES), F32).at[0, :N_GROUPS].set(b_group)
    b_r = b_r.at[0, N_GROUPS:N_GROUPS + N_EXPERTS].set(b_expert)
    w_hi = w_r.astype(MXU_DTYPE)
    w_lo = (w_r - w_hi.astype(F32)).astype(MXU_DTYPE)
    ri, rw, cnt = _router(h, g, w_hi, w_lo, b_r)

    counts = cnt[-1, N_GROUPS:N_GROUPS + N_EXPERTS]
    padded = (counts + EXPERT_TILE - 1) // EXPERT_TILE * EXPERT_TILE
    ends = jnp.cumsum(padded)
    starts = ends - padded
    experts = jnp.arange(N_EXPERTS, dtype=jnp.int32)
    seg_start = jnp.sum(jnp.where(ri[:, 0:2, None] == experts, starts, 0), axis=-1)
    pos = (seg_start + ri[:, 2:4]).reshape(-1).astype(jnp.int32)
    p_rows = 2 * n + N_EXPERTS * EXPERT_TILE
    tile_start = jnp.arange(p_rows // EXPERT_TILE, dtype=jnp.int32) * EXPERT_TILE
    tile_expert = jnp.minimum(jnp.sum(tile_start[:, None] >= ends[None, :], axis=1), N_EXPERTS - 1).astype(jnp.int32)
    tile_active = (tile_start < ends[-1]).astype(jnp.int32)
    tile_x = jnp.minimum(tile_start, ends[-1] - EXPERT_TILE) // EXPERT_TILE

    tail = jnp.minimum(ends[-1] + jnp.arange(N_EXPERTS + 1, dtype=jnp.int32) * EXPERT_TILE, p_rows)
    xs = _dispatch(pos, jnp.concatenate([starts + counts, tail]).astype(jnp.int32), h, g, p_rows)
    ys = _experts(tile_expert, tile_active, tile_x.astype(jnp.int32), xs, w_gate.astype(MXU_DTYPE),
                  w_up.astype(MXU_DTYPE), w_down.astype(MXU_DTYPE), p_rows)
    return _combine(pos, h, rw, g_final, ys, normalize=last)


def _even_weights(w_in):
    gate_cols = w_in[:, C_GATE:C_GATE + 3 * NSA_HEADS].reshape(D_MODEL, NSA_KV_HEADS, HEADS_PER_GROUP, 3)
    gate_cols = gate_cols.transpose(0, 1, 3, 2).reshape(D_MODEL, NSA_KV_HEADS, 3 * HEADS_PER_GROUP)
    gate_cols = jnp.pad(gate_cols, ((0, 0), (0, 0), (0, LANES - 3 * HEADS_PER_GROUP)))
    return jnp.concatenate([w_in[:, :C_GATE], gate_cols.reshape(D_MODEL, NSA_KV_HEADS * LANES)], axis=1)


def _block_diag_halves(w):
    blocks = w.shape[0] // 2
    out = jnp.zeros((2, blocks * w.shape[1], blocks * w.shape[2]), w.dtype)
    for j in range(2):
        for k in range(blocks):
            out = out.at[j, k * w.shape[1]:(k + 1) * w.shape[1], k * w.shape[2]:(k + 1) * w.shape[2]].set(
                w[j * blocks + k])
    return out


def _even_mixer(h, g, batch, cos_t, sin_t, w_in, w_out, conv_w, conv_b, w_r, b_r, w_i, b_i, lam,
                pos_k, w1_k, w2_k, pos_v, w1_v, w2_v):
    n = h.shape[0]
    t = n // batch
    xl, gg, qn, qr, kcv, ksr, vs, kwr, vw, gates = _even_proj(
        h, g, _even_weights(w_in).astype(MXU_DTYPE), cos_t, sin_t)

    y_lru = _lru(xl, gg, conv_w, conv_b[None, :], _block_diag_halves(w_r).astype(MXU_DTYPE),
                 _block_diag_halves(w_i).astype(MXU_DTYPE), b_r[None, :], b_i[None, :],
                 jax.nn.softplus(-lam)[None, :], batch)

    nchunk = t // CMP_STRIDE
    x16 = kcv.reshape(batch, nchunk, CMP_STRIDE, 2 * NSA_KV_HEADS, HEAD_DIM).transpose(0, 3, 1, 2, 4)
    x16 = x16.reshape(batch, 2 * NSA_KV_HEADS, nchunk, CMP_STRIDE * HEAD_DIM)
    half = CMP_STRIDE * HEAD_DIM
    pos = jnp.stack([pos_k.reshape(2, 1, half), pos_v.reshape(2, 1, half)])
    w1 = jnp.stack([w1_k.reshape(2, half, CMP_HIDDEN), w1_v.reshape(2, half, CMP_HIDDEN)]).astype(MXU_DTYPE)
    w2 = jnp.stack([w2_k, w2_v]).astype(MXU_DTYPE)
    kvc = _compress(x16, pos, w1, w2)

    tq = min(NSA_TQ, t)
    nq = t // tq
    rows = HEADS_PER_GROUP * tq

    def q_tiles(z):
        z = z.reshape(batch, nq, tq, NSA_KV_HEADS, HEADS_PER_GROUP, HEAD_DIM)
        return z.transpose(0, 3, 1, 5, 4, 2).reshape(batch, NSA_KV_HEADS, nq, HEAD_DIM, rows)

    def kv_heads(z):
        return z.reshape(batch, t, NSA_KV_HEADS, HEAD_DIM).transpose(0, 2, 1, 3)

    def kv_heads_t(z):
        return z.reshape(batch, t, NSA_KV_HEADS, HEAD_DIM).transpose(0, 2, 3, 1)

    n_gate = 3 * HEADS_PER_GROUP
    gates_t = gates.reshape(batch, nq, tq, NSA_KV_HEADS, LANES)[..., :n_gate]
    gates_t = gates_t.reshape(batch, nq, tq, NSA_KV_HEADS, 3, HEADS_PER_GROUP).transpose(0, 3, 1, 4, 5, 2)
    gates_t = jnp.pad(gates_t.reshape(batch, NSA_KV_HEADS, nq, 3, rows), ((0, 0),) * 3 + ((0, SUBLANES - 3), (0, 0)))
    vct = kvc[:, NSA_KV_HEADS:].transpose(0, 1, 3, 2)
    y_t = _nsa(q_tiles(qn), q_tiles(qr), kvc, vct, kv_heads(ksr), kv_heads_t(vs), kv_heads(kwr), kv_heads_t(vw),
               gates_t)
    y_nsa = y_t.reshape(batch, NSA_KV_HEADS, nq, HEAD_DIM, HEADS_PER_GROUP, tq).transpose(0, 2, 5, 1, 4, 3)
    w_out = w_out.astype(MXU_DTYPE)
    return _out_proj(h, y_lru, y_nsa.reshape(n, NSA_HEADS * HEAD_DIM), w_out[:LRU_WIDTH], w_out[LRU_WIDTH:])


def _rope_tables(positions):
    inv = ROPE_THETA ** (-jnp.arange(0, 2 * ROT_HALF, 2, dtype=F32) / (2 * ROT_HALF))
    ang = positions.reshape(-1).astype(F32)[:, None] * inv
    cos, sin = jnp.cos(ang), jnp.sin(ang)
    rest = HEAD_DIM - 2 * ROT_HALF
    cos_h = jnp.concatenate([cos, cos, jnp.ones((cos.shape[0], rest), F32)], axis=1)
    sin_h = jnp.concatenate([-sin, sin, jnp.zeros((cos.shape[0], rest), F32)], axis=1)
    reps = LANES // HEAD_DIM
    return jnp.tile(cos_h, (1, reps)), jnp.tile(sin_h, (1, reps))


def kernel(x, mem, positions, norm_mix, norm_xattn, norm_mem, norm_ffn, norm_final, even_w_in, even_w_out, lru_conv_w, lru_conv_b, lru_w_r, lru_b_r, lru_w_i, lru_b_i, lru_lambda, nsa_cmp_pos_k, nsa_cmp_w1_k, nsa_cmp_w2_k, nsa_cmp_pos_v, nsa_cmp_w1_v, nsa_cmp_w2_v, odd_w_in, odd_conv_w, odd_w_out, xa_wq, xa_wk, xa_wv, xa_wo, moe_w_group, moe_b_group, moe_w_expert, moe_b_expert, moe_w_gate, moe_w_up, moe_w_down):
    batch, t, d = x.shape
    n = batch * t
    depth = norm_mix.shape[0]
    cos_t, sin_t = _rope_tables(positions)
    h = x.reshape(n, d)
    mem2 = mem.reshape(-1, d)
    for layer in range(depth):
        g_mix = norm_mix[layer][None, :]
        if layer % 2 == 0:
            e = layer // 2
            h = _even_mixer(h, g_mix, batch, cos_t, sin_t, even_w_in[e], even_w_out[e], lru_conv_w[e],
                            lru_conv_b[e], lru_w_r[e], lru_b_r[e], lru_w_i[e], lru_b_i[e], lru_lambda[e],
                            nsa_cmp_pos_k[e], nsa_cmp_w1_k[e], nsa_cmp_w2_k[e],
                            nsa_cmp_pos_v[e], nsa_cmp_w1_v[e], nsa_cmp_w2_v[e])
        else:
            o = layer // 2
            h = _short_conv(h, g_mix, odd_w_in[o].astype(MXU_DTYPE), odd_conv_w[o],
                            odd_w_out[o].astype(MXU_DTYPE), batch)
        wkv = jnp.concatenate([xa_wk[layer], xa_wv[layer]], axis=1).astype(MXU_DTYPE)
        kv = _mem_kv(mem2, norm_mem[layer][None, :], wkv)
        h = _xattn(h, norm_xattn[layer][None, :], xa_wq[layer].astype(MXU_DTYPE), kv,
                   xa_wo[layer].astype(MXU_DTYPE), batch)
        h = _moe(h, norm_ffn[layer][None, :], moe_w_group[layer], moe_b_group[layer], moe_w_expert[layer],
                 moe_b_expert[layer], moe_w_gate[layer], moe_w_up[layer], moe_w_down[layer],
                 norm_final[None, :], last=layer == depth - 1)
    return h.reshape(batch, t, d)
```

```python
import functools

import jax
import jax.numpy as jnp
import numpy as np
from jax import lax
from jax.experimental import pallas as pl
from jax.experimental.pallas import tpu as pltpu

F32 = jnp.float32
MXU_DTYPE = jnp.bfloat16

D_MODEL = 1024
LRU_WIDTH = 512
LRU_CONV = 4
LRU_C = 8.0
NSA_HEADS = 8
NSA_KV_HEADS = 2
HEADS_PER_GROUP = NSA_HEADS // NSA_KV_HEADS
HEAD_DIM = 64
CMP_STRIDE = 16
CMP_LEN = 32
CMP_HIDDEN = 128
SEL_BLOCK = 64
SEL_TOP = 16
WINDOW = 512
ROT_HALF = 8
ROPE_THETA = 500000.0
SC_CONV = 3
XA_HEADS = 4
XA_HEAD_DIM = 256
N_GROUPS = 4
EXPERTS_PER_GROUP = 4
N_EXPERTS = 16
D_EXPERT = 512
EPS = 1e-6
NEG = -1e30
LOG2E = 1.4426950408889634
FORCE = 1e9

LANES = 128
SUBLANES = 8
VMEM_LIMIT = 56 * 1024 * 1024

C_XL, C_GL, C_Q, C_KCV, C_KS, C_VS, C_KW, C_VW, C_GATE, C_END = (
    0, 512, 1024, 1536, 1792, 1920, 2048, 2176, 2304, 2560)

ROW_TILE = 512
NSA_TQ = 128
NSA_TK = 512
EXPERT_TILE = 256
DMA_UNROLL = 8
ROW_CHUNKS = D_MODEL // LANES


def _cparams(n_axes):
    return pltpu.CompilerParams(dimension_semantics=("arbitrary",) * n_axes,
                                vmem_limit_bytes=VMEM_LIMIT)


def _mm(a, b):
    return jnp.dot(a, b, preferred_element_type=F32)


def _mm_nt(a, b):
    return lax.dot_general(a, b, (((1,), (1,)), ((), ())), preferred_element_type=F32)


def _rms(x, g):
    return x * lax.rsqrt(jnp.mean(x * x, axis=-1, keepdims=True) + EPS) * g


def _shift_rows(xx, s, rows):
    if s == 0:
        return xx[SUBLANES:SUBLANES + rows]
    return pltpu.roll(xx, s, 0)[SUBLANES:SUBLANES + rows]


def _rope(x, cos_t, sin_t):
    width = x.shape[-1]
    lane = lax.broadcasted_iota(jnp.int32, x.shape, 1) % HEAD_DIM
    partner = jnp.where(lane < ROT_HALF, pltpu.roll(x, width - ROT_HALF, 1), pltpu.roll(x, ROT_HALF, 1))
    return x * cos_t + partner * sin_t


def _even_proj_kernel(h_ref, g_ref, w_ref, cos_ref, sin_ref,
                      xl_ref, gg_ref, qn_ref, qr_ref, kcv_ref, ksr_ref, vs_ref, kwr_ref, vw_ref,
                      gate_ref):
    xn = _rms(h_ref[...], g_ref[...]).astype(MXU_DTYPE)

    def proj(c0, c1):
        return _mm(xn, w_ref[:, c0:c1])

    cos_t = cos_ref[...]
    sin_t = sin_ref[...]
    xl_ref[...] = proj(C_XL, C_GL)
    gg_ref[...] = jax.nn.gelu(proj(C_GL, C_Q))
    q = proj(C_Q, C_KCV) * (HEAD_DIM ** -0.5)
    reps = (C_KCV - C_Q) // LANES
    cos_q = jnp.concatenate([cos_t] * reps, axis=1)
    sin_q = jnp.concatenate([sin_t] * reps, axis=1)
    _store_query_tiles(qn_ref, q, HEAD_DIM)
    _store_query_tiles(qr_ref, _rope(q, cos_q, sin_q) * LOG2E, HEAD_DIM)
    kcv_ref[...] = proj(C_KCV, C_KS)
    ksr_ref[...] = _rope(proj(C_KS, C_VS), cos_t, sin_t).astype(ksr_ref.dtype)
    vs_ref[...] = proj(C_VS, C_KW).T.astype(vs_ref.dtype)
    kwr_ref[...] = _rope(proj(C_KW, C_VW), cos_t, sin_t).astype(kwr_ref.dtype)
    vw_ref[...] = proj(C_VW, C_GATE).T.astype(vw_ref.dtype)
    _store_query_tiles(gate_ref, jax.nn.sigmoid(proj(C_GATE, C_END)), SUBLANES)


def _store_query_tiles(ref, x, per_head):
    group_width = x.shape[1] // NSA_KV_HEADS
    for grp in range(NSA_KV_HEADS):
        for j in range(x.shape[0] // NSA_TQ):
            blk = x[j * NSA_TQ:(j + 1) * NSA_TQ, grp * group_width:(grp + 1) * group_width].T
            ref[grp, j] = jnp.concatenate([blk[hh * per_head:(hh + 1) * per_head, :]
                                           for hh in range(HEADS_PER_GROUP)], axis=1).astype(ref.dtype)


def _even_proj(h, g, w, cos_t, sin_t):
    n = h.shape[0]
    tm = min(ROW_TILE, n)
    rows = HEADS_PER_GROUP * NSA_TQ
    row = lambda c: pl.BlockSpec((tm, c), lambda i: (i, 0))
    col = pl.BlockSpec((LANES, tm), lambda i: (0, i))
    full = lambda a: pl.BlockSpec(a.shape, lambda i: (0,) * a.ndim)
    tiles = lambda c: pl.BlockSpec((NSA_KV_HEADS, tm // NSA_TQ, c, rows), lambda i: (0, i, 0, 0))
    tiles_shape = lambda c, dt: jax.ShapeDtypeStruct((NSA_KV_HEADS, n // NSA_TQ, c, rows), dt)
    flat = lambda c, dt: jax.ShapeDtypeStruct((n, c), dt)
    return pl.pallas_call(
        _even_proj_kernel,
        grid=(n // tm,),
        in_specs=[row(D_MODEL), full(g), full(w), row(LANES), row(LANES)],
        out_specs=[row(512), row(512), tiles(HEAD_DIM), tiles(HEAD_DIM), row(256),
                   row(LANES), col, row(LANES), col, tiles(SUBLANES)],
        out_shape=[flat(512, F32), flat(512, F32), tiles_shape(HEAD_DIM, MXU_DTYPE), tiles_shape(HEAD_DIM, MXU_DTYPE),
                   flat(256, F32), flat(LANES, MXU_DTYPE), jax.ShapeDtypeStruct((LANES, n), MXU_DTYPE),
                   flat(LANES, MXU_DTYPE), jax.ShapeDtypeStruct((LANES, n), MXU_DTYPE),
                   tiles_shape(SUBLANES, F32)],
        compiler_params=_cparams(1),
        name="even_proj",
    )(h, g, w, cos_t, sin_t)


def _lru_kernel(xl_ref, gg_ref, cw_ref, cb_ref, wr_ref, wi_ref, br_ref, bi_ref, sp_ref,
                y_ref, tail_ref, h_ref, a_scr, u_scr):
    tt = xl_ref.shape[0]

    @pl.when(pl.program_id(1) == 0)
    def _():
        tail_ref[...] = jnp.zeros_like(tail_ref)
        h_ref[...] = jnp.zeros_like(h_ref)

    x = xl_ref[...]
    xx = jnp.concatenate([tail_ref[...], x], axis=0)
    tail_ref[...] = x[tt - SUBLANES:tt]
    xc = cb_ref[...] + sum(cw_ref[k:k + 1, :] * _shift_rows(xx, LRU_CONV - 1 - k, tt)
                           for k in range(LRU_CONV))
    xcb = xc.astype(MXU_DTYPE)
    half = LRU_WIDTH // 2
    r_lin = jnp.concatenate([_mm(xcb[:, j * half:(j + 1) * half], wr_ref[j]) for j in range(2)], axis=1)
    i_lin = jnp.concatenate([_mm(xcb[:, j * half:(j + 1) * half], wi_ref[j]) for j in range(2)], axis=1)
    r = jax.nn.sigmoid(r_lin + br_ref[...])
    i = jax.nn.sigmoid(i_lin + bi_ref[...])
    log_a = -LRU_C * r * sp_ref[...]
    a = jnp.exp(log_a)
    u = jnp.sqrt(jnp.tanh(-log_a) * (1.0 + a * a)) * (i * xc)

    r8 = lax.broadcasted_iota(jnp.int32, a.shape, 0) % SUBLANES
    for s in (1, 2, 4):
        keep = r8 >= s
        u = jnp.where(keep, a * pltpu.roll(u, s, 0) + u, u)
        a = jnp.where(keep, a * pltpu.roll(a, s, 0), a)
    a_scr[...] = a
    u_scr[...] = u

    def body(gidx, h):
        r0 = pl.multiple_of(gidx * SUBLANES, SUBLANES)
        out = a_scr[pl.ds(r0, SUBLANES), :] * h + u_scr[pl.ds(r0, SUBLANES), :]
        u_scr[pl.ds(r0, SUBLANES), :] = out
        return out[SUBLANES - 1:SUBLANES, :]

    h_ref[...] = lax.fori_loop(0, tt // SUBLANES, body, h_ref[...])
    y_ref[...] = (u_scr[...] * gg_ref[...]).astype(y_ref.dtype)


def _lru(xl, gg, cw, cb, wr, wi, br, bi, sp, batch):
    n = xl.shape[0]
    t = n // batch
    tt = min(ROW_TILE, t)
    nt = t // tt
    row = pl.BlockSpec((tt, LRU_WIDTH), lambda b, i: (b * nt + i, 0))
    full = lambda a: pl.BlockSpec(a.shape, lambda b, i: (0,) * a.ndim)
    return pl.pallas_call(
        _lru_kernel,
        grid=(batch, nt),
        in_specs=[row, row] + [full(a) for a in (cw, cb, wr, wi, br, bi, sp)],
        out_specs=row,
        out_shape=jax.ShapeDtypeStruct((n, LRU_WIDTH), MXU_DTYPE),
        scratch_shapes=[pltpu.VMEM((SUBLANES, LRU_WIDTH), F32), pltpu.VMEM((1, LRU_WIDTH), F32),
                        pltpu.VMEM((tt, LRU_WIDTH), F32), pltpu.VMEM((tt, LRU_WIDTH), F32)],
        compiler_params=_cparams(2),
        name="lru",
    )(xl, gg, cw, cb, wr, wi, br, bi, sp)


def _compress_kernel(x_ref, pos_ref, w1_ref, w2_ref, o_ref):
    x = x_ref[0, 0]
    nchunk = x.shape[0]
    lo = (x + pos_ref[0, 0]).astype(MXU_DTYPE)
    hi = (x + pos_ref[0, 1]).astype(MXU_DTYPE)
    p_lo = _mm(lo, w1_ref[0, 0])
    p_hi = _mm(hi, w1_ref[0, 1])
    a = p_lo + pltpu.roll(p_hi, nchunk - 1, 0)
    o_ref[0, 0] = _mm(jax.nn.gelu(a).astype(MXU_DTYPE), w2_ref[0]).astype(o_ref.dtype)


def _compress(x16, pos, w1, w2):
    b, _, nchunk, width = x16.shape
    return pl.pallas_call(
        _compress_kernel,
        grid=(b, 2 * NSA_KV_HEADS),
        in_specs=[pl.BlockSpec((1, 1, nchunk, width), lambda i, j: (i, j, 0, 0)),
                  pl.BlockSpec((1, 2, 1, width), lambda i, j: (j // NSA_KV_HEADS, 0, 0, 0)),
                  pl.BlockSpec((1, 2, width, CMP_HIDDEN), lambda i, j: (j // NSA_KV_HEADS, 0, 0, 0)),
                  pl.BlockSpec((1, CMP_HIDDEN, HEAD_DIM), lambda i, j: (j // NSA_KV_HEADS, 0, 0))],
        out_specs=pl.BlockSpec((1, 1, nchunk, HEAD_DIM), lambda i, j: (i, j, 0, 0)),
        out_shape=jax.ShapeDtypeStruct((b, 2 * NSA_KV_HEADS, nchunk, HEAD_DIM), MXU_DTYPE),
        compiler_params=_cparams(2),
        name="compress",
    )(x16, pos, w1, w2)


def _nsa_kernel(qn_ref, qr_ref, kc_ref, vct_ref, ks_ref, vst_ref, kw_ref, vwt_ref, gate_ref,
                y_ref, chosen_scr, acc_scr, accw_scr, sc_scr):
    rows = qn_ref.shape[3]
    tq = rows // HEADS_PER_GROUP
    t = ks_ref.shape[0]
    nc = kc_ref.shape[2]
    nsel = t // SEL_BLOCK
    n_top = min(SEL_TOP, nsel)
    tk = NSA_TK
    t0 = pl.program_id(2) * tq
    qn = qn_ref[0, 0]
    grp = pl.program_id(1)
    qr64 = qr_ref[0, 0]
    qr = jnp.concatenate([jnp.where(grp == gi, qr64, jnp.zeros_like(qr64)) for gi in range(NSA_KV_HEADS)], axis=0)

    s = _mm(kc_ref[0, 0], qn)
    tq_row = t0 + lax.broadcasted_iota(jnp.int32, (nc, rows), 1) % tq
    cmp_end = lax.broadcasted_iota(jnp.int32, (nc, rows), 0) * CMP_STRIDE + (CMP_LEN - 1)
    valid = cmp_end <= tq_row
    s = jnp.where(valid, s, NEG)
    e = jnp.where(valid, jnp.exp(s - jnp.max(s, axis=0, keepdims=True)), 0.0)
    den = jnp.sum(e, axis=0, keepdims=True)
    p = e * (1.0 / jnp.where(den > 0.0, den, 1.0))
    o_cmp = _mm(vct_ref[0, 0], p.astype(MXU_DTYPE))

    psum = p[:, 0:tq]
    for hh in range(1, HEADS_PER_GROUP):
        psum = psum + p[:, hh * tq:(hh + 1) * tq]
    p_hi = psum.astype(MXU_DTYPE)
    p_lo = (psum - p_hi.astype(F32)).astype(MXU_DTYPE)
    cj = lax.broadcasted_iota(jnp.int32, (nsel, nc), 0)
    cn = lax.broadcasted_iota(jnp.int32, (nsel, nc), 1)
    ratio = SEL_BLOCK // CMP_STRIDE
    cover = jnp.where((cn >= ratio * cj - (CMP_LEN // CMP_STRIDE - 1)) & (cn <= ratio * cj + ratio - 1)
                      & (cn < nc - 1), 1.0, 0.0).astype(MXU_DTYPE)
    imp = _mm(cover, p_hi) + _mm(cover, p_lo)

    blk = lax.broadcasted_iota(jnp.int32, (nsel, tq), 0)
    tq_col = t0 + lax.broadcasted_iota(jnp.int32, (nsel, tq), 1)
    cur = tq_col // SEL_BLOCK
    forced = (blk == 0) | (blk == cur) | (blk == cur - 1)
    causal = blk * SEL_BLOCK <= tq_col
    score0 = jnp.where(causal, jnp.where(forced, FORCE, imp), NEG)
    blk_f = blk.astype(F32)

    def pick_one(_, carry):
        score, chosen = carry
        best = jnp.max(score, axis=0, keepdims=True)
        first = jnp.min(jnp.where(score == best, blk_f, float(nsel)), axis=0, keepdims=True)
        hit = blk_f == first
        return jnp.where(hit, -jnp.inf, score), jnp.where(hit, 1.0, chosen)

    _, chosen = lax.fori_loop(0, n_top, pick_one, (score0, jnp.zeros((nsel, tq), F32)))
    chosen_scr[...] = chosen

    def scores(k_ref, kt, bias):
        k0 = pl.multiple_of(kt * tk, tk)
        return _mm(k_ref[pl.ds(k0, tk), :], qr) + jnp.concatenate([bias] * HEADS_PER_GROUP, axis=1)

    def soft(sc, m_old, l_old):
        m_new = jnp.maximum(m_old, jnp.max(sc, axis=0, keepdims=True))
        alpha = jnp.exp2(m_old - m_new)
        pe = jnp.exp2(sc - m_new)
        return m_new, alpha * l_old + jnp.sum(pe, axis=0, keepdims=True), alpha, pe.astype(MXU_DTYPE)

    def accumulate(vt_ref, acc_ref, kt, alpha, pe):
        k0 = pl.multiple_of(kt * tk, tk)
        acc_ref[...] = alpha * acc_ref[...] + _mm(vt_ref[:, pl.ds(k0, tk)], pe)

    expand = jnp.where(lax.broadcasted_iota(jnp.int32, (tk, SUBLANES), 0) // SEL_BLOCK
                       == lax.broadcasted_iota(jnp.int32, (tk, SUBLANES), 1), 1.0, 0.0)

    def key_minus_query(kt):
        return (kt * tk - t0 + lax.broadcasted_iota(jnp.int32, (tk, tq), 0)
                - lax.broadcasted_iota(jnp.int32, (tk, tq), 1))

    def sel_bias(kt):
        grp = pl.multiple_of(kt * SUBLANES, SUBLANES)
        bias = _mm(expand, (chosen_scr[pl.ds(grp, SUBLANES), :] - 1.0) * (-NEG))
        return jnp.where(key_minus_query(kt) <= 0, bias, NEG)

    def win_bias(kt, diagonal):
        d = key_minus_query(kt)
        inside = (d > -WINDOW) & (d <= 0) if diagonal else d > -WINDOW
        return jnp.where(inside, 0.0, NEG)

    def sel_only(kt, carry):
        m_s, l_s, m_w, l_w = carry
        sc_next = scores(ks_ref, kt + 1, sel_bias(kt + 1))
        m_s, l_s, alpha, pe = soft(sc_scr[...], m_s, l_s)
        accumulate(vst_ref, acc_scr, kt, alpha, pe)
        sc_scr[...] = sc_next
        return m_s, l_s, m_w, l_w

    def both(kt, carry, last=False):
        m_s, l_s, m_w, l_w = carry
        sc_w = scores(kw_ref, kt, win_bias(kt, last))
        if not last:
            sc_next = scores(ks_ref, kt + 1, sel_bias(kt + 1))
        m_s, l_s, alpha_s, pe_s = soft(sc_scr[...], m_s, l_s)
        m_w, l_w, alpha_w, pe_w = soft(sc_w, m_w, l_w)
        accumulate(vst_ref, acc_scr, kt, alpha_s, pe_s)
        accumulate(vwt_ref, accw_scr, kt, alpha_w, pe_w)
        if not last:
            sc_scr[...] = sc_next
        return m_s, l_s, m_w, l_w

    kt_last = (t0 + tq - 1) // tk
    win_lo = jnp.maximum(t0 - (WINDOW - 1), 0) // tk
    acc_scr[...] = jnp.zeros_like(acc_scr)
    accw_scr[...] = jnp.zeros_like(accw_scr)
    lowest = jnp.full((1, rows), NEG, F32)
    zero = jnp.zeros((1, rows), F32)
    sc_scr[...] = scores(ks_ref, 0, sel_bias(0))
    carry = lax.fori_loop(0, win_lo, sel_only, (lowest, zero, lowest, zero))
    carry = lax.fori_loop(win_lo, kt_last, both, carry)
    _, l_s, _, l_w = both(kt_last, carry, last=True)
    o_sel = acc_scr[...] * (1.0 / l_s)
    o_win = accw_scr[...] * (1.0 / l_w)

    gate = gate_ref[0, 0]
    y = gate[0:1] * o_cmp + gate[1:2] * o_sel + gate[2:3] * o_win
    y = jnp.concatenate([y[:, hh * tq:(hh + 1) * tq] for hh in range(HEADS_PER_GROUP)], axis=0)
    y_ref[...] = y.T.astype(y_ref.dtype)


def _nsa(qn_t, qr_t, kvc, vct, ks, vst, kw, vwt, gates_t, batch):
    _, n_tiles, _, rows = qn_t.shape
    tq = rows // HEADS_PER_GROUP
    nq = n_tiles // batch
    t = nq * tq
    nc = kvc.shape[2]
    assert NSA_TK == SEL_BLOCK * SUBLANES and t % NSA_TK == 0
    tile = lambda c: pl.BlockSpec((1, 1, c, rows), lambda i, g, j: (g, i * nq + j, 0, 0))
    k_spec = pl.BlockSpec((t, LANES), lambda i, g, j: (i, 0))
    vt_spec = pl.BlockSpec((HEAD_DIM, t), lambda i, g, j: (g, i))
    return pl.pallas_call(
        _nsa_kernel,
        grid=(batch, NSA_KV_HEADS, nq),
        in_specs=[tile(HEAD_DIM), tile(HEAD_DIM),
                  pl.BlockSpec((1, 1, nc, HEAD_DIM), lambda i, g, j: (i, g, 0, 0)),
                  pl.BlockSpec((1, 1, HEAD_DIM, nc), lambda i, g, j: (i, g, 0, 0)),
                  k_spec, vt_spec, k_spec, vt_spec, tile(SUBLANES)],
        out_specs=pl.BlockSpec((tq, HEADS_PER_GROUP * HEAD_DIM), lambda i, g, j: (i * nq + j, g)),
        out_shape=jax.ShapeDtypeStruct((batch * t, NSA_HEADS * HEAD_DIM), MXU_DTYPE),
        scratch_shapes=[pltpu.VMEM((t // SEL_BLOCK, tq), F32),
                        pltpu.VMEM((HEAD_DIM, rows), F32), pltpu.VMEM((HEAD_DIM, rows), F32),
                        pltpu.VMEM((NSA_TK, rows), F32)],
        compiler_params=_cparams(3),
        name="nsa",
    )(qn_t, qr_t, kvc, vct, ks, vst, kw, vwt, gates_t)


def _out_proj_kernel(h_ref, a_ref, b_ref, wa_ref, wb_ref, o_ref):
    o_ref[...] = h_ref[...] + _mm(a_ref[...], wa_ref[...]) + _mm(b_ref[...], wb_ref[...])


def _out_proj(h, a, b, wa, wb):
    n = h.shape[0]
    tm = min(ROW_TILE, n)
    full = lambda x: pl.BlockSpec(x.shape, lambda i: (0,) * x.ndim)
    return pl.pallas_call(
        _out_proj_kernel,
        grid=(n // tm,),
        in_specs=[pl.BlockSpec((tm, D_MODEL), lambda i: (i, 0)),
                  pl.BlockSpec((tm, a.shape[1]), lambda i: (i, 0)),
                  pl.BlockSpec((tm, b.shape[1]), lambda i: (i, 0)), full(wa), full(wb)],
        out_specs=pl.BlockSpec((tm, D_MODEL), lambda i: (i, 0)),
        out_shape=jax.ShapeDtypeStruct((n, D_MODEL), F32),
        compiler_params=_cparams(1),
        name="out_proj",
    )(h, a, b, wa, wb)


def _short_conv_kernel(h_ref, g_ref, win_ref, cw_ref, wout_ref, o_ref, tail_ref):
    tt = h_ref.shape[0]

    @pl.when(pl.program_id(1) == 0)
    def _():
        tail_ref[...] = jnp.zeros_like(tail_ref)

    h = h_ref[...]
    xn = _rms(h, g_ref[...]).astype(MXU_DTYPE)
    b_g = _mm(xn, win_ref[:, 0:D_MODEL])
    cv = _mm(xn, win_ref[:, D_MODEL:2 * D_MODEL]) * _mm(xn, win_ref[:, 2 * D_MODEL:3 * D_MODEL])
    xx = jnp.concatenate([tail_ref[...], cv], axis=0)
    tail_ref[...] = cv[tt - SUBLANES:tt]
    conv = sum(cw_ref[k:k + 1, :] * _shift_rows(xx, SC_CONV - 1 - k, tt) for k in range(SC_CONV))
    o_ref[...] = h + _mm((b_g * conv).astype(MXU_DTYPE), wout_ref[...])


def _short_conv(h, g, w_in, cw, w_out, batch):
    n = h.shape[0]
    t = n // batch
    tt = min(ROW_TILE, t)
    nt = t // tt
    row = pl.BlockSpec((tt, D_MODEL), lambda b, i: (b * nt + i, 0))
    full = lambda a: pl.BlockSpec(a.shape, lambda b, i: (0,) * a.ndim)
    return pl.pallas_call(
        _short_conv_kernel,
        grid=(batch, nt),
        in_specs=[row, full(g), full(w_in), full(cw), full(w_out)],
        out_specs=row,
        out_shape=jax.ShapeDtypeStruct((n, D_MODEL), F32),
        scratch_shapes=[pltpu.VMEM((SUBLANES, D_MODEL), F32)],
        compiler_params=_cparams(2),
        name="short_conv",
    )(h, g, w_in, cw, w_out)


def _mem_kv_kernel(m_ref, g_ref, w_ref, o_ref):
    o_ref[...] = _mm(_rms(m_ref[...], g_ref[...]).astype(MXU_DTYPE), w_ref[...]).astype(o_ref.dtype)


def _mem_kv(mem, g, wkv):
    n = mem.shape[0]
    tm = min(ROW_TILE, n)
    tn = 1024
    return pl.pallas_call(
        _mem_kv_kernel,
        grid=(n // tm, wkv.shape[1] // tn),
        in_specs=[pl.BlockSpec((tm, D_MODEL), lambda i, j: (i, 0)),
                  pl.BlockSpec(g.shape, lambda i, j: (0, 0)),
                  pl.BlockSpec((D_MODEL, tn), lambda i, j: (0, j))],
        out_specs=pl.BlockSpec((tm, tn), lambda i, j: (i, j)),
        out_shape=jax.ShapeDtypeStruct((n, wkv.shape[1]), MXU_DTYPE),
        compiler_params=_cparams(2),
        name="mem_kv",
    )(mem, g, wkv)


def _xattn_kernel(h_ref, g_ref, wq_ref, kv_ref, wo_ref, o_ref):
    h = h_ref[...]
    xn = _rms(h, g_ref[...]).astype(MXU_DTYPE)
    q = (_mm(xn, wq_ref[...]) * (XA_HEAD_DIM ** -0.5)).astype(MXU_DTYPE)
    width = XA_HEADS * XA_HEAD_DIM
    outs = []
    for hd in range(XA_HEADS):
        sl = slice(hd * XA_HEAD_DIM, (hd + 1) * XA_HEAD_DIM)
        s = _mm_nt(q[:, sl], kv_ref[:, sl])
        e = jnp.exp(s - jnp.max(s, axis=-1, keepdims=True))
        p = e / jnp.sum(e, axis=-1, keepdims=True)
        outs.append(_mm(p.astype(MXU_DTYPE), kv_ref[:, width + hd * XA_HEAD_DIM:width + (hd + 1) * XA_HEAD_DIM]))
    o = jnp.concatenate(outs, axis=1).astype(MXU_DTYPE)
    o_ref[...] = h + _mm(o, wo_ref[...])


def _xattn(h, g, wq, kv, wo, batch):
    n = h.shape[0]
    t = n // batch
    tm = min(ROW_TILE, t)
    nt = t // tm
    mlen = kv.shape[0] // batch
    full = lambda a: pl.BlockSpec(a.shape, lambda b, i: (0,) * a.ndim)
    row = pl.BlockSpec((tm, D_MODEL), lambda b, i: (b * nt + i, 0))
    return pl.pallas_call(
        _xattn_kernel,
        grid=(batch, nt),
        in_specs=[row, full(g), full(wq), pl.BlockSpec((mlen, kv.shape[1]), lambda b, i: (b, 0)), full(wo)],
        out_specs=row,
        out_shape=jax.ShapeDtypeStruct((n, D_MODEL), F32),
        compiler_params=_cparams(2),
        name="xattn",
    )(h, g, wq, kv, wo)


def _router_kernel(h_ref, g_ref, whi_ref, wlo_ref, b_ref, ri_ref, rw_ref, cnt_ref, carry_ref):
    tm = h_ref.shape[0]

    @pl.when(pl.program_id(0) == 0)
    def _():
        carry_ref[...] = jnp.zeros_like(carry_ref)

    xn = _rms(h_ref[...], g_ref[...])
    x_hi = xn.astype(MXU_DTYPE)
    x_lo = (xn - x_hi.astype(F32)).astype(MXU_DTYPE)
    logits = _mm(x_hi, whi_ref[...]) + (_mm(x_lo, whi_ref[...]) + _mm(x_hi, wlo_ref[...])) + b_ref[...]

    lane = lax.broadcasted_iota(jnp.int32, logits.shape, 1)
    lane_f = lane.astype(F32)
    none = float(LANES)
    is_g = lane < N_GROUPS
    g_max = jnp.max(jnp.where(is_g, logits, -jnp.inf), axis=-1, keepdims=True)
    g_sum = jnp.sum(jnp.where(is_g, jnp.exp(logits - g_max), 0.0), axis=-1, keepdims=True)
    g_top = 1.0 / g_sum
    g_idx = jnp.min(jnp.where(is_g & (logits == g_max), lane_f, none), axis=-1, keepdims=True)
    first = N_GROUPS + EXPERTS_PER_GROUP * g_idx
    in_g = (lane_f >= first) & (lane_f < first + EXPERTS_PER_GROUP)
    e1 = jnp.max(jnp.where(in_g, logits, -jnp.inf), axis=-1, keepdims=True)
    i1 = jnp.min(jnp.where(in_g & (logits == e1), lane_f, none), axis=-1, keepdims=True)
    rest = in_g & (lane_f != i1)
    e2 = jnp.max(jnp.where(rest, logits, -jnp.inf), axis=-1, keepdims=True)
    i2 = jnp.min(jnp.where(rest & (logits == e2), lane_f, none), axis=-1, keepdims=True)
    ratio = jnp.exp(e2 - e1)
    w1 = g_top / (1.0 + ratio)
    w2 = g_top * ratio / (1.0 + ratio)

    onehot = jnp.where((lane_f == i1) | (lane_f == i2), 1.0, 0.0)
    tri = jnp.where(lax.broadcasted_iota(jnp.int32, (tm, tm), 0) >= lax.broadcasted_iota(jnp.int32, (tm, tm), 1),
                    1.0, 0.0).astype(MXU_DTYPE)
    incl = _mm(tri, onehot.astype(MXU_DTYPE))
    before = incl - onehot + carry_ref[...]
    carry_ref[...] = carry_ref[...] + incl[tm - 1:tm, :]
    rank1 = jnp.sum(jnp.where(lane_f == i1, before, 0.0), axis=-1, keepdims=True)
    rank2 = jnp.sum(jnp.where(lane_f == i2, before, 0.0), axis=-1, keepdims=True)

    ri = jnp.where(lane == 0, i1 - N_GROUPS,
                   jnp.where(lane == 1, i2 - N_GROUPS, jnp.where(lane == 2, rank1, jnp.where(lane == 3, rank2, 0.0))))
    ri_ref[...] = ri.astype(jnp.int32)
    rw_ref[...] = jnp.where(lane == 0, w1, jnp.where(lane == 1, w2, 0.0))
    cnt_ref[...] = jnp.broadcast_to(carry_ref[...], cnt_ref.shape).astype(jnp.int32)


def _router(h, g, w_hi, w_lo, bias):
    n = h.shape[0]
    tm = min(ROW_TILE, n)
    full = lambda a: pl.BlockSpec(a.shape, lambda i: (0,) * a.ndim)
    return pl.pallas_call(
        _router_kernel,
        grid=(n // tm,),
        in_specs=[pl.BlockSpec((tm, D_MODEL), lambda i: (i, 0)), full(g), full(w_hi), full(w_lo), full(bias)],
        out_specs=[pl.BlockSpec((tm, LANES), lambda i: (i, 0)), pl.BlockSpec((tm, LANES), lambda i: (i, 0)),
                   pl.BlockSpec((SUBLANES, LANES), lambda i: (i, 0))],
        out_shape=[jax.ShapeDtypeStruct((n, LANES), jnp.int32), jax.ShapeDtypeStruct((n, LANES), F32),
                   jax.ShapeDtypeStruct((n // tm * SUBLANES, LANES), jnp.int32)],
        scratch_shapes=[pltpu.VMEM((1, LANES), F32)],
        compiler_params=_cparams(1),
        name="router",
    )(h, g, w_hi, w_lo, bias)


def _row_copy(src_ref, src_row, dst_ref, dst_row, sem):
    src = src_ref.at[pl.ds(pl.multiple_of(src_row * ROW_CHUNKS, ROW_CHUNKS), ROW_CHUNKS)]
    dst = dst_ref.at[pl.ds(pl.multiple_of(dst_row * ROW_CHUNKS, ROW_CHUNKS), ROW_CHUNKS)]
    return pltpu.make_async_copy(src, dst, sem)


def _to_token_tiles(ref, x):
    rows = x.shape[0]
    for c in range(ROW_CHUNKS):
        ref[pl.ds(c, rows, stride=ROW_CHUNKS), :] = x[:, c * LANES:(c + 1) * LANES]


def _from_token_tiles(ref):
    rows = ref.shape[0] // ROW_CHUNKS
    return jnp.concatenate([ref[pl.ds(c, rows, stride=ROW_CHUNKS), :] for c in range(ROW_CHUNKS)], axis=1)


def _dispatch_kernel(pos_ref, fill_ref, h_ref, g_ref, xs_ref, xn_scr, zero_scr, sem, fill_sem):
    tm = h_ref.shape[0]

    @pl.when(pl.program_id(0) == 0)
    def _():
        zero_scr[...] = jnp.zeros_like(zero_scr)
        for e in range(fill_ref.shape[0]):
            first = pl.multiple_of(fill_ref[e] * ROW_CHUNKS, ROW_CHUNKS)
            fill = pltpu.make_async_copy(zero_scr, xs_ref.at[pl.ds(first, EXPERT_TILE * ROW_CHUNKS)], fill_sem)
            fill.start()
            fill.wait()

    base = pl.program_id(0) * (2 * tm)
    _to_token_tiles(xn_scr, _rms(h_ref[...], g_ref[...]))

    def issue(r, _):
        _row_copy(xn_scr, r, xs_ref, pos_ref[base + 2 * r], sem).start()
        _row_copy(xn_scr, r, xs_ref, pos_ref[base + 2 * r + 1], sem).start()
        return 0

    lax.fori_loop(0, tm, issue, 0, unroll=DMA_UNROLL)
    for _ in range(2):
        pltpu.make_async_copy(xn_scr, xs_ref.at[pl.ds(0, tm * ROW_CHUNKS)], sem).wait()


def _dispatch(pos, fill_start, h, g, p_rows):
    n = h.shape[0]
    tm = min(ROW_TILE, n)
    grid_spec = pltpu.PrefetchScalarGridSpec(
        num_scalar_prefetch=2,
        grid=(n // tm,),
        in_specs=[pl.BlockSpec((tm, D_MODEL), lambda i, pos, fill: (i, 0)),
                  pl.BlockSpec(g.shape, lambda i, pos, fill: (0, 0))],
        out_specs=pl.BlockSpec(memory_space=pl.ANY),
        scratch_shapes=[pltpu.VMEM((tm * ROW_CHUNKS, LANES), F32),
                        pltpu.VMEM((EXPERT_TILE * ROW_CHUNKS, LANES), F32),
                        pltpu.SemaphoreType.DMA(()), pltpu.SemaphoreType.DMA(())],
    )
    return pl.pallas_call(
        _dispatch_kernel,
        grid_spec=grid_spec,
        out_shape=jax.ShapeDtypeStruct(((p_rows + EXPERT_TILE) * ROW_CHUNKS, LANES), F32),
        compiler_params=_cparams(1),
        name="dispatch",
    )(pos, fill_start, h, g)


def _expert_kernel(te_ref, ta_ref, tx_ref, x_ref, wg_ref, wu_ref, wd_ref, y_ref):
    j = pl.program_id(0)

    @pl.when(ta_ref[j] == 1)
    def _():
        x = _from_token_tiles(x_ref).astype(MXU_DTYPE)
        hid = jax.nn.silu(_mm(x, wg_ref[0])) * _mm(x, wu_ref[0])
        _to_token_tiles(y_ref, _mm(hid.astype(MXU_DTYPE), wd_ref[0]))

    @pl.when(ta_ref[j] == 0)
    def _():
        y_ref[...] = jnp.zeros_like(y_ref)


def _experts(tile_expert, tile_active, tile_x, xs, wg, wu, wd, p_rows):
    grid_spec = pltpu.PrefetchScalarGridSpec(
        num_scalar_prefetch=3,
        grid=(p_rows // EXPERT_TILE,),
        in_specs=[pl.BlockSpec((EXPERT_TILE * ROW_CHUNKS, LANES), lambda j, te, ta, tx: (tx[j], 0)),
                  pl.BlockSpec((1, D_MODEL, D_EXPERT), lambda j, te, ta, tx: (te[j], 0, 0)),
                  pl.BlockSpec((1, D_MODEL, D_EXPERT), lambda j, te, ta, tx: (te[j], 0, 0)),
                  pl.BlockSpec((1, D_EXPERT, D_MODEL), lambda j, te, ta, tx: (te[j], 0, 0))],
        out_specs=pl.BlockSpec((EXPERT_TILE * ROW_CHUNKS, LANES), lambda j, te, ta, tx: (j, 0)),
    )
    return pl.pallas_call(
        _expert_kernel,
        grid_spec=grid_spec,
        out_shape=jax.ShapeDtypeStruct((p_rows * ROW_CHUNKS, LANES), F32),
        compiler_params=_cparams(1),
        name="experts",
    )(tile_expert, tile_active, tile_x, xs, wg, wu, wd)


def _combine_kernel(pos_ref, h_ref, rw_ref, g_ref, ys_ref, o_ref, y1_scr, y2_scr, sem, *, normalize):
    tm = h_ref.shape[0]
    base = pl.program_id(0) * (2 * tm)

    def issue(r, _):
        _row_copy(ys_ref, pos_ref[base + 2 * r], y1_scr, r, sem).start()
        _row_copy(ys_ref, pos_ref[base + 2 * r + 1], y2_scr, r, sem).start()
        return 0

    lax.fori_loop(0, tm, issue, 0, unroll=DMA_UNROLL)
    for buf in (y1_scr, y2_scr):
        pltpu.make_async_copy(ys_ref.at[pl.ds(0, tm * ROW_CHUNKS)], buf, sem).wait()
    rw = rw_ref[...]
    out = h_ref[...] + (rw[:, 0:1] * _from_token_tiles(y1_scr) + rw[:, 1:2] * _from_token_tiles(y2_scr))
    o_ref[...] = _rms(out, g_ref[...]) if normalize else out


def _combine(pos, h, rw, g_final, ys, normalize):
    n = h.shape[0]
    tm = min(ROW_TILE, n)
    grid_spec = pltpu.PrefetchScalarGridSpec(
        num_scalar_prefetch=1,
        grid=(n // tm,),
        in_specs=[pl.BlockSpec((tm, D_MODEL), lambda i, pos: (i, 0)),
                  pl.BlockSpec((tm, LANES), lambda i, pos: (i, 0)),
                  pl.BlockSpec(g_final.shape, lambda i, pos: (0, 0)),
                  pl.BlockSpec(memory_space=pl.ANY)],
        out_specs=pl.BlockSpec((tm, D_MODEL), lambda i, pos: (i, 0)),
        scratch_shapes=[pltpu.VMEM((tm * ROW_CHUNKS, LANES), F32), pltpu.VMEM((tm * ROW_CHUNKS, LANES), F32),
                        pltpu.SemaphoreType.DMA(())],
    )
    return pl.pallas_call(
        functools.partial(_combine_kernel, normalize=normalize),
        grid_spec=grid_spec,
        out_shape=jax.ShapeDtypeStruct((n, D_MODEL), F32),
        compiler_params=_cparams(1),
        name="combine",
    )(pos, h, rw, g_final, ys)


def _moe(h, g, w_group, b_group, w_expert, b_expert, w_gate, w_up, w_down, g_final, last):
    n = h.shape[0]
    w_r = jnp.zeros((D_MODEL, LANES), F32).at[:, :N_GROUPS].set(w_group)
    w_r = w_r.at[:, N_GROUPS:N_GROUPS + N_EXPERTS].set(w_expert)
    b_r = jnp.zeros((1, LANES), F32).at[0, :N_GROUPS].set(b_group)
    b_r = b_r.at[0, N_GROUPS:N_GROUPS + N_EXPERTS].set(b_expert)
    w_hi = w_r.astype(MXU_DTYPE)
    w_lo = (w_r - w_hi.astype(F32)).astype(MXU_DTYPE)
    ri, rw, cnt = _router(h, g, w_hi, w_lo, b_r)

    counts = cnt[-1, N_GROUPS:N_GROUPS + N_EXPERTS]
    padded = (counts + EXPERT_TILE - 1) // EXPERT_TILE * EXPERT_TILE
    ends = jnp.cumsum(padded)
    starts = ends - padded
    experts = jnp.arange(N_EXPERTS, dtype=jnp.int32)
    seg_start = jnp.sum(jnp.where(ri[:, 0:2, None] == experts, starts, 0), axis=-1)
    pos = (seg_start + ri[:, 2:4]).reshape(-1).astype(jnp.int32)
    p_rows = 2 * n + N_EXPERTS * EXPERT_TILE
    tile_start = jnp.arange(p_rows // EXPERT_TILE, dtype=jnp.int32) * EXPERT_TILE
    tile_expert = jnp.minimum(jnp.sum(tile_start[:, None] >= ends[None, :], axis=1), N_EXPERTS - 1).astype(jnp.int32)
    tile_active = (tile_start < ends[-1]).astype(jnp.int32)
    tile_x = jnp.minimum(tile_start, ends[-1] - EXPERT_TILE) // EXPERT_TILE

    tail = jnp.minimum(ends[-1] + jnp.arange(N_EXPERTS + 1, dtype=jnp.int32) * EXPERT_TILE, p_rows)
    xs = _dispatch(pos, jnp.concatenate([starts + counts, tail]).astype(jnp.int32), h, g, p_rows)
    ys = _experts(tile_expert, tile_active, tile_x.astype(jnp.int32), xs, w_gate.astype(MXU_DTYPE),
                  w_up.astype(MXU_DTYPE), w_down.astype(MXU_DTYPE), p_rows)
    return _combine(pos, h, rw, g_final, ys, normalize=last)


def _even_weights(w_in):
    gate_cols = w_in[:, C_GATE:C_GATE + 3 * NSA_HEADS].reshape(D_MODEL, NSA_KV_HEADS, HEADS_PER_GROUP, 3)
    gate_cols = jnp.pad(gate_cols, ((0, 0), (0, 0), (0, 0), (0, SUBLANES - 3)))
    gate_cols = gate_cols.reshape(D_MODEL, NSA_KV_HEADS, HEADS_PER_GROUP * SUBLANES)
    gate_cols = jnp.pad(gate_cols, ((0, 0), (0, 0), (0, LANES - HEADS_PER_GROUP * SUBLANES)))
    return jnp.concatenate([w_in[:, :C_GATE], gate_cols.reshape(D_MODEL, NSA_KV_HEADS * LANES)], axis=1)


def _block_diag_halves(w):
    blocks = w.shape[0] // 2
    out = jnp.zeros((2, blocks * w.shape[1], blocks * w.shape[2]), w.dtype)
    for j in range(2):
        for k in range(blocks):
            out = out.at[j, k * w.shape[1]:(k + 1) * w.shape[1], k * w.shape[2]:(k + 1) * w.shape[2]].set(
                w[j * blocks + k])
    return out


def _even_mixer(h, g, batch, cos_t, sin_t, w_in, w_out, conv_w, conv_b, w_r, b_r, w_i, b_i, lam,
                pos_k, w1_k, w2_k, pos_v, w1_v, w2_v):
    n = h.shape[0]
    t = n // batch
    xl, gg, qn, qr, kcv, ksr, vs, kwr, vw, gates = _even_proj(
        h, g, _even_weights(w_in).astype(MXU_DTYPE), cos_t, sin_t)

    y_lru = _lru(xl, gg, conv_w, conv_b[None, :], _block_diag_halves(w_r).astype(MXU_DTYPE),
                 _block_diag_halves(w_i).astype(MXU_DTYPE), b_r[None, :], b_i[None, :],
                 jax.nn.softplus(-lam)[None, :], batch)

    nchunk = t // CMP_STRIDE
    x16 = kcv.reshape(batch, nchunk, CMP_STRIDE, 2 * NSA_KV_HEADS, HEAD_DIM).transpose(0, 3, 1, 2, 4)
    x16 = x16.reshape(batch, 2 * NSA_KV_HEADS, nchunk, CMP_STRIDE * HEAD_DIM)
    half = CMP_STRIDE * HEAD_DIM
    pos = jnp.stack([pos_k.reshape(2, 1, half), pos_v.reshape(2, 1, half)])
    w1 = jnp.stack([w1_k.reshape(2, half, CMP_HIDDEN), w1_v.reshape(2, half, CMP_HIDDEN)]).astype(MXU_DTYPE)
    w2 = jnp.stack([w2_k, w2_v]).astype(MXU_DTYPE)
    kvc = _compress(x16, pos, w1, w2)

    vct = kvc[:, NSA_KV_HEADS:].transpose(0, 1, 3, 2)
    y_nsa = _nsa(qn, qr, kvc, vct, ksr, vs, kwr, vw, gates, batch)
    w_out = w_out.astype(MXU_DTYPE)
    return _out_proj(h, y_lru, y_nsa, w_out[:LRU_WIDTH], w_out[LRU_WIDTH:])


def _rope_tables(positions):
    inv = ROPE_THETA ** (-jnp.arange(0, 2 * ROT_HALF, 2, dtype=F32) / (2 * ROT_HALF))
    ang = positions.reshape(-1).astype(F32)[:, None] * inv
    cos, sin = jnp.cos(ang), jnp.sin(ang)
    rest = HEAD_DIM - 2 * ROT_HALF
    cos_h = jnp.concatenate([cos, cos, jnp.ones((cos.shape[0], rest), F32)], axis=1)
    sin_h = jnp.concatenate([-sin, sin, jnp.zeros((cos.shape[0], rest), F32)], axis=1)
    reps = LANES // HEAD_DIM
    return jnp.tile(cos_h, (1, reps)), jnp.tile(sin_h, (1, reps))


def kernel(x, mem, positions, norm_mix, norm_xattn, norm_mem, norm_ffn, norm_final, even_w_in, even_w_out, lru_conv_w, lru_conv_b, lru_w_r, lru_b_r, lru_w_i, lru_b_i, lru_lambda, nsa_cmp_pos_k, nsa_cmp_w1_k, nsa_cmp_w2_k, nsa_cmp_pos_v, nsa_cmp_w1_v, nsa_cmp_w2_v, odd_w_in, odd_conv_w, odd_w_out, xa_wq, xa_wk, xa_wv, xa_wo, moe_w_group, moe_b_group, moe_w_expert, moe_b_expert, moe_w_gate, moe_w_up, moe_w_down):
    batch, t, d = x.shape
    n = batch * t
    depth = norm_mix.shape[0]
    cos_t, sin_t = _rope_tables(positions)
    h = x.reshape(n, d)
    mem2 = mem.reshape(-1, d)
    for layer in range(depth):
        g_mix = norm_mix[layer][None, :]
        if layer % 2 == 0:
            e = layer // 2
            h = _even_mixer(h, g_mix, batch, cos_t, sin_t, even_w_in[e], even_w_out[e], lru_conv_w[e],
                            lru_conv_b[e], lru_w_r[e], lru_b_r[e], lru_w_i[e], lru_b_i[e], lru_lambda[e],
                            nsa_cmp_pos_k[e], nsa_cmp_w1_k[e], nsa_cmp_w2_k[e],
                            nsa_cmp_pos_v[e], nsa_cmp_w1_v[e], nsa_cmp_w2_v[e])
        else:
            o = layer // 2
            h = _short_conv(h, g_mix, odd_w_in[o].astype(MXU_DTYPE), odd_conv_w[o],
                            odd_w_out[o].astype(MXU_DTYPE), batch)
        wkv = jnp.concatenate([xa_wk[layer], xa_wv[layer]], axis=1).astype(MXU_DTYPE)
        kv = _mem_kv(mem2, norm_mem[layer][None, :], wkv)
        h = _xattn(h, norm_xattn[layer][None, :], xa_wq[layer].astype(MXU_DTYPE), kv,
                   xa_wo[layer].astype(MXU_DTYPE), batch)
        h = _moe(h, norm_ffn[layer][None, :], moe_w_group[layer], moe_b_group[layer], moe_w_expert[layer],
                 moe_b_expert[layer], moe_w_gate[layer], moe_w_up[layer], moe_w_down[layer],
                 norm_final[None, :], last=layer == depth - 1)
    return h.reshape(batch, t, d)
```

```python
import functools

import jax
import jax.numpy as jnp
import numpy as np
from jax import lax
from jax.experimental import pallas as pl
from jax.experimental.pallas import tpu as pltpu

F32 = jnp.float32
MXU_DTYPE = jnp.bfloat16

D_MODEL = 1024
LRU_WIDTH = 512
LRU_CONV = 4
LRU_C = 8.0
NSA_HEADS = 8
NSA_KV_HEADS = 2
HEADS_PER_GROUP = NSA_HEADS // NSA_KV_HEADS
HEAD_DIM = 64
CMP_STRIDE = 16
CMP_LEN = 32
CMP_HIDDEN = 128
SEL_BLOCK = 64
SEL_TOP = 16
WINDOW = 512
ROT_HALF = 8
ROPE_THETA = 500000.0
SC_CONV = 3
XA_HEADS = 4
XA_HEAD_DIM = 256
N_GROUPS = 4
EXPERTS_PER_GROUP = 4
N_EXPERTS = 16
D_EXPERT = 512
EPS = 1e-6
NEG = -1e30
LOG2E = 1.4426950408889634
FORCE = 1e9

LANES = 128
SUBLANES = 8
VMEM_LIMIT = 56 * 1024 * 1024

C_XL, C_GL, C_Q, C_KCV, C_KS, C_VS, C_KW, C_VW, C_GATE, C_END = (
    0, 512, 1024, 1536, 1792, 1920, 2048, 2176, 2304, 2560)

ROW_TILE = 512
NSA_TQ = 128
NSA_TK = 512
EXPERT_TILE = 256
DMA_UNROLL = 8
ROW_CHUNKS = D_MODEL // LANES


def _cparams(n_axes):
    return pltpu.CompilerParams(dimension_semantics=("arbitrary",) * n_axes,
                                vmem_limit_bytes=VMEM_LIMIT)


def _mm(a, b):
    return jnp.dot(a, b, preferred_element_type=F32)


def _mm_nt(a, b):
    return lax.dot_general(a, b, (((1,), (1,)), ((), ())), preferred_element_type=F32)


def _rms(x, g):
    return x * lax.rsqrt(jnp.mean(x * x, axis=-1, keepdims=True) + EPS) * g


def _shift_rows(xx, s, rows):
    if s == 0:
        return xx[SUBLANES:SUBLANES + rows]
    return pltpu.roll(xx, s, 0)[SUBLANES:SUBLANES + rows]


def _rope(x, cos_t, sin_t):
    width = x.shape[-1]
    lane = lax.broadcasted_iota(jnp.int32, x.shape, 1) % HEAD_DIM
    partner = jnp.where(lane < ROT_HALF, pltpu.roll(x, width - ROT_HALF, 1), pltpu.roll(x, ROT_HALF, 1))
    return x * cos_t + partner * sin_t


def _even_proj_kernel(h_ref, g_ref, w_ref, cos_ref, sin_ref,
                      xl_ref, gg_ref, qn_ref, qr_ref, kcv_ref, ksr_ref, vs_ref, kwr_ref, vw_ref,
                      gate_ref):
    xn = _rms(h_ref[...], g_ref[...]).astype(MXU_DTYPE)

    def proj(c0, c1):
        return _mm(xn, w_ref[:, c0:c1])

    cos_t = cos_ref[...]
    sin_t = sin_ref[...]
    xl_ref[...] = proj(C_XL, C_GL)
    gg_ref[...] = jax.nn.gelu(proj(C_GL, C_Q))
    q = proj(C_Q, C_KCV) * (HEAD_DIM ** -0.5)
    reps = (C_KCV - C_Q) // LANES
    cos_q = jnp.concatenate([cos_t] * reps, axis=1)
    sin_q = jnp.concatenate([sin_t] * reps, axis=1)
    _store_query_tiles(qn_ref, q, HEAD_DIM)
    _store_query_tiles(qr_ref, _rope(q, cos_q, sin_q) * LOG2E, HEAD_DIM)
    kcv_ref[...] = proj(C_KCV, C_KS)
    ksr_ref[...] = _rope(proj(C_KS, C_VS), cos_t, sin_t).astype(ksr_ref.dtype)
    vs_ref[...] = proj(C_VS, C_KW).T.astype(vs_ref.dtype)
    kwr_ref[...] = _rope(proj(C_KW, C_VW), cos_t, sin_t).astype(kwr_ref.dtype)
    vw_ref[...] = proj(C_VW, C_GATE).T.astype(vw_ref.dtype)
    _store_query_tiles(gate_ref, jax.nn.sigmoid(proj(C_GATE, C_END)), SUBLANES)


def _store_query_tiles(ref, x, per_head):
    group_width = x.shape[1] // NSA_KV_HEADS
    for grp in range(NSA_KV_HEADS):
        for j in range(x.shape[0] // NSA_TQ):
            blk = x[j * NSA_TQ:(j + 1) * NSA_TQ, grp * group_width:(grp + 1) * group_width].T
            ref[grp, j] = jnp.concatenate([blk[hh * per_head:(hh + 1) * per_head, :]
                                           for hh in range(HEADS_PER_GROUP)], axis=1).astype(ref.dtype)


def _even_proj(h, g, w, cos_t, sin_t):
    n = h.shape[0]
    tm = min(ROW_TILE, n)
    rows = HEADS_PER_GROUP * NSA_TQ
    row = lambda c: pl.BlockSpec((tm, c), lambda i: (i, 0))
    col = pl.BlockSpec((LANES, tm), lambda i: (0, i))
    full = lambda a: pl.BlockSpec(a.shape, lambda i: (0,) * a.ndim)
    tiles = lambda c: pl.BlockSpec((NSA_KV_HEADS, tm // NSA_TQ, c, rows), lambda i: (0, i, 0, 0))
    tiles_shape = lambda c, dt: jax.ShapeDtypeStruct((NSA_KV_HEADS, n // NSA_TQ, c, rows), dt)
    flat = lambda c, dt: jax.ShapeDtypeStruct((n, c), dt)
    return pl.pallas_call(
        _even_proj_kernel,
        grid=(n // tm,),
        in_specs=[row(D_MODEL), full(g), full(w), row(LANES), row(LANES)],
        out_specs=[row(512), row(512), tiles(HEAD_DIM), tiles(HEAD_DIM), row(256),
                   row(LANES), col, row(LANES), col, tiles(SUBLANES)],
        out_shape=[flat(512, F32), flat(512, F32), tiles_shape(HEAD_DIM, MXU_DTYPE), tiles_shape(HEAD_DIM, MXU_DTYPE),
                   flat(256, F32), flat(LANES, MXU_DTYPE), jax.ShapeDtypeStruct((LANES, n), MXU_DTYPE),
                   flat(LANES, MXU_DTYPE), jax.ShapeDtypeStruct((LANES, n), MXU_DTYPE),
                   tiles_shape(SUBLANES, F32)],
        compiler_params=_cparams(1),
        name="even_proj",
    )(h, g, w, cos_t, sin_t)


def _lru_kernel(xl_ref, gg_ref, cw_ref, cb_ref, wr_ref, wi_ref, br_ref, bi_ref, sp_ref,
                y_ref, tail_ref, h_ref, a_scr, u_scr):
    tt = xl_ref.shape[0]

    @pl.when(pl.program_id(1) == 0)
    def _():
        tail_ref[...] = jnp.zeros_like(tail_ref)
        h_ref[...] = jnp.zeros_like(h_ref)

    x = xl_ref[...]
    xx = jnp.concatenate([tail_ref[...], x], axis=0)
    tail_ref[...] = x[tt - SUBLANES:tt]
    xc = cb_ref[...] + sum(cw_ref[k:k + 1, :] * _shift_rows(xx, LRU_CONV - 1 - k, tt)
                           for k in range(LRU_CONV))
    xcb = xc.astype(MXU_DTYPE)
    half = LRU_WIDTH // 2
    r_lin = jnp.concatenate([_mm(xcb[:, j * half:(j + 1) * half], wr_ref[j]) for j in range(2)], axis=1)
    i_lin = jnp.concatenate([_mm(xcb[:, j * half:(j + 1) * half], wi_ref[j]) for j in range(2)], axis=1)
    r = jax.nn.sigmoid(r_lin + br_ref[...])
    i = jax.nn.sigmoid(i_lin + bi_ref[...])
    log_a = -LRU_C * r * sp_ref[...]
    a = jnp.exp(log_a)
    u = jnp.sqrt(jnp.tanh(-log_a) * (1.0 + a * a)) * (i * xc)

    r8 = lax.broadcasted_iota(jnp.int32, a.shape, 0) % SUBLANES
    for s in (1, 2, 4):
        keep = r8 >= s
        u = jnp.where(keep, a * pltpu.roll(u, s, 0) + u, u)
        a = jnp.where(keep, a * pltpu.roll(a, s, 0), a)
    a_scr[...] = a
    u_scr[...] = u

    def body(gidx, h):
        r0 = pl.multiple_of(gidx * SUBLANES, SUBLANES)
        out = a_scr[pl.ds(r0, SUBLANES), :] * h + u_scr[pl.ds(r0, SUBLANES), :]
        u_scr[pl.ds(r0, SUBLANES), :] = out
        return out[SUBLANES - 1:SUBLANES, :]

    h_ref[...] = lax.fori_loop(0, tt // SUBLANES, body, h_ref[...])
    y_ref[...] = (u_scr[...] * gg_ref[...]).astype(y_ref.dtype)


def _lru(xl, gg, cw, cb, wr, wi, br, bi, sp, batch):
    n = xl.shape[0]
    t = n // batch
    tt = min(ROW_TILE, t)
    nt = t // tt
    row = pl.BlockSpec((tt, LRU_WIDTH), lambda b, i: (b * nt + i, 0))
    full = lambda a: pl.BlockSpec(a.shape, lambda b, i: (0,) * a.ndim)
    return pl.pallas_call(
        _lru_kernel,
        grid=(batch, nt),
        in_specs=[row, row] + [full(a) for a in (cw, cb, wr, wi, br, bi, sp)],
        out_specs=row,
        out_shape=jax.ShapeDtypeStruct((n, LRU_WIDTH), MXU_DTYPE),
        scratch_shapes=[pltpu.VMEM((SUBLANES, LRU_WIDTH), F32), pltpu.VMEM((1, LRU_WIDTH), F32),
                        pltpu.VMEM((tt, LRU_WIDTH), F32), pltpu.VMEM((tt, LRU_WIDTH), F32)],
        compiler_params=_cparams(2),
        name="lru",
    )(xl, gg, cw, cb, wr, wi, br, bi, sp)


def _compress_kernel(x_ref, pos_ref, w1_ref, w2_ref, o_ref):
    x = x_ref[0, 0]
    nchunk = x.shape[0]
    lo = (x + pos_ref[0, 0]).astype(MXU_DTYPE)
    hi = (x + pos_ref[0, 1]).astype(MXU_DTYPE)
    p_lo = _mm(lo, w1_ref[0, 0])
    p_hi = _mm(hi, w1_ref[0, 1])
    a = p_lo + pltpu.roll(p_hi, nchunk - 1, 0)
    o_ref[0, 0] = _mm(jax.nn.gelu(a).astype(MXU_DTYPE), w2_ref[0]).astype(o_ref.dtype)


def _compress(x16, pos, w1, w2):
    b, _, nchunk, width = x16.shape
    return pl.pallas_call(
        _compress_kernel,
        grid=(b, 2 * NSA_KV_HEADS),
        in_specs=[pl.BlockSpec((1, 1, nchunk, width), lambda i, j: (i, j, 0, 0)),
                  pl.BlockSpec((1, 2, 1, width), lambda i, j: (j // NSA_KV_HEADS, 0, 0, 0)),
                  pl.BlockSpec((1, 2, width, CMP_HIDDEN), lambda i, j: (j // NSA_KV_HEADS, 0, 0, 0)),
                  pl.BlockSpec((1, CMP_HIDDEN, HEAD_DIM), lambda i, j: (j // NSA_KV_HEADS, 0, 0))],
        out_specs=pl.BlockSpec((1, 1, nchunk, HEAD_DIM), lambda i, j: (i, j, 0, 0)),
        out_shape=jax.ShapeDtypeStruct((b, 2 * NSA_KV_HEADS, nchunk, HEAD_DIM), MXU_DTYPE),
        compiler_params=_cparams(2),
        name="compress",
    )(x16, pos, w1, w2)


def _nsa_kernel(qn_ref, qr_ref, kc_ref, vct_ref, ks_ref, vst_ref, kw_ref, vwt_ref, gate_ref,
                y_ref, chosen_scr, acc_scr, accw_scr, sc_scr):
    rows = qn_ref.shape[3]
    tq = rows // HEADS_PER_GROUP
    t = ks_ref.shape[0]
    nc = kc_ref.shape[2]
    nsel = t // SEL_BLOCK
    n_top = min(SEL_TOP, nsel)
    tk = NSA_TK
    t0 = pl.program_id(2) * tq
    qn = qn_ref[0, 0]
    grp = pl.program_id(1)
    qr64 = qr_ref[0, 0]
    qr = jnp.concatenate([jnp.where(grp == gi, qr64, jnp.zeros_like(qr64)) for gi in range(NSA_KV_HEADS)], axis=0)

    s = _mm(kc_ref[0, 0], qn)
    tq_row = t0 + lax.broadcasted_iota(jnp.int32, (nc, rows), 1) % tq
    cmp_end = lax.broadcasted_iota(jnp.int32, (nc, rows), 0) * CMP_STRIDE + (CMP_LEN - 1)
    valid = cmp_end <= tq_row
    s = jnp.where(valid, s, NEG)
    e = jnp.where(valid, jnp.exp(s - jnp.max(s, axis=0, keepdims=True)), 0.0)
    den = jnp.sum(e, axis=0, keepdims=True)
    p = e * (1.0 / jnp.where(den > 0.0, den, 1.0))
    o_cmp = _mm(vct_ref[0, 0], p.astype(MXU_DTYPE))

    psum = p[:, 0:tq]
    for hh in range(1, HEADS_PER_GROUP):
        psum = psum + p[:, hh * tq:(hh + 1) * tq]
    p_hi = psum.astype(MXU_DTYPE)
    p_lo = (psum - p_hi.astype(F32)).astype(MXU_DTYPE)
    cj = lax.broadcasted_iota(jnp.int32, (nsel, nc), 0)
    cn = lax.broadcasted_iota(jnp.int32, (nsel, nc), 1)
    ratio = SEL_BLOCK // CMP_STRIDE
    cover = jnp.where((cn >= ratio * cj - (CMP_LEN // CMP_STRIDE - 1)) & (cn <= ratio * cj + ratio - 1)
                      & (cn < nc - 1), 1.0, 0.0).astype(MXU_DTYPE)
    imp = _mm(cover, p_hi) + _mm(cover, p_lo)

    blk = lax.broadcasted_iota(jnp.int32, (nsel, tq), 0)
    tq_col = t0 + lax.broadcasted_iota(jnp.int32, (nsel, tq), 1)
    cur = tq_col // SEL_BLOCK
    forced = (blk == 0) | (blk == cur) | (blk == cur - 1)
    causal = blk * SEL_BLOCK <= tq_col
    score0 = jnp.where(causal, jnp.where(forced, FORCE, imp), NEG)
    blk_f = blk.astype(F32)

    def pick_one(_, carry):
        score, chosen = carry
        best = jnp.max(score, axis=0, keepdims=True)
        first = jnp.min(jnp.where(score == best, blk_f, float(nsel)), axis=0, keepdims=True)
        hit = blk_f == first
        return jnp.where(hit, -jnp.inf, score), jnp.where(hit, 1.0, chosen)

    _, chosen = lax.fori_loop(0, n_top, pick_one, (score0, jnp.zeros((nsel, tq), F32)))
    chosen_scr[...] = chosen

    def scores(k_ref, kt, bias):
        k0 = pl.multiple_of(kt * tk, tk)
        return _mm(k_ref[pl.ds(k0, tk), :], qr) + jnp.concatenate([bias] * HEADS_PER_GROUP, axis=1)

    def soft(sc, m_old, l_old):
        m_new = jnp.maximum(m_old, jnp.max(sc, axis=0, keepdims=True))
        alpha = jnp.exp2(m_old - m_new)
        pe = jnp.exp2(sc - m_new)
        return m_new, alpha * l_old + jnp.sum(pe, axis=0, keepdims=True), alpha, pe.astype(MXU_DTYPE)

    def accumulate(vt_ref, acc_ref, kt, alpha, pe):
        k0 = pl.multiple_of(kt * tk, tk)
        acc_ref[...] = alpha * acc_ref[...] + _mm(vt_ref[:, pl.ds(k0, tk)], pe)

    expand = jnp.where(lax.broadcasted_iota(jnp.int32, (tk, SUBLANES), 0) // SEL_BLOCK
                       == lax.broadcasted_iota(jnp.int32, (tk, SUBLANES), 1), 1.0, 0.0)

    def key_minus_query(kt):
        return (kt * tk - t0 + lax.broadcasted_iota(jnp.int32, (tk, tq), 0)
                - lax.broadcasted_iota(jnp.int32, (tk, tq), 1))

    def sel_bias(kt):
        grp = pl.multiple_of(kt * SUBLANES, SUBLANES)
        bias = _mm(expand, (chosen_scr[pl.ds(grp, SUBLANES), :] - 1.0) * (-NEG))
        return jnp.where(key_minus_query(kt) <= 0, bias, NEG)

    def win_bias(kt, diagonal):
        d = key_minus_query(kt)
        inside = (d > -WINDOW) & (d <= 0) if diagonal else d > -WINDOW
        return jnp.where(inside, 0.0, NEG)

    def sel_only(kt, carry):
        m_s, l_s, m_w, l_w = carry
        sc_next = scores(ks_ref, kt + 1, sel_bias(kt + 1))
        m_s, l_s, alpha, pe = soft(sc_scr[...], m_s, l_s)
        accumulate(vst_ref, acc_scr, kt, alpha, pe)
        sc_scr[...] = sc_next
        return m_s, l_s, m_w, l_w

    def both(kt, carry, last=False):
        m_s, l_s, m_w, l_w = carry
        sc_w = scores(kw_ref, kt, win_bias(kt, last))
        if not last:
            sc_next = scores(ks_ref, kt + 1, sel_bias(kt + 1))
        m_s, l_s, alpha_s, pe_s = soft(sc_scr[...], m_s, l_s)
        m_w, l_w, alpha_w, pe_w = soft(sc_w, m_w, l_w)
        accumulate(vst_ref, acc_scr, kt, alpha_s, pe_s)
        accumulate(vwt_ref, accw_scr, kt, alpha_w, pe_w)
        if not last:
            sc_scr[...] = sc_next
        return m_s, l_s, m_w, l_w

    kt_last = (t0 + tq - 1) // tk
    win_lo = jnp.maximum(t0 - (WINDOW - 1), 0) // tk
    acc_scr[...] = jnp.zeros_like(acc_scr)
    accw_scr[...] = jnp.zeros_like(accw_scr)
    lowest = jnp.full((1, rows), NEG, F32)
    zero = jnp.zeros((1, rows), F32)
    sc_scr[...] = scores(ks_ref, 0, sel_bias(0))
    carry = lax.fori_loop(0, win_lo, sel_only, (lowest, zero, lowest, zero))
    carry = lax.fori_loop(win_lo, kt_last, both, carry)
    _, l_s, _, l_w = both(kt_last, carry, last=True)
    o_sel = acc_scr[...] * (1.0 / l_s)
    o_win = accw_scr[...] * (1.0 / l_w)

    gate = gate_ref[0, 0]
    y = gate[0:1] * o_cmp + gate[1:2] * o_sel + gate[2:3] * o_win
    y = jnp.concatenate([y[:, hh * tq:(hh + 1) * tq] for hh in range(HEADS_PER_GROUP)], axis=0)
    y_ref[...] = y.T.astype(y_ref.dtype)


def _nsa(qn_t, qr_t, kvc, vct, ks, vst, kw, vwt, gates_t, batch):
    _, n_tiles, _, rows = qn_t.shape
    tq = rows // HEADS_PER_GROUP
    nq = n_tiles // batch
    t = nq * tq
    nc = kvc.shape[2]
    assert NSA_TK == SEL_BLOCK * SUBLANES and t % NSA_TK == 0
    tile = lambda c: pl.BlockSpec((1, 1, c, rows), lambda i, g, j: (g, i * nq + j, 0, 0))
    k_spec = pl.BlockSpec((t, LANES), lambda i, g, j: (i, 0))
    vt_spec = pl.BlockSpec((HEAD_DIM, t), lambda i, g, j: (g, i))
    return pl.pallas_call(
        _nsa_kernel,
        grid=(batch, NSA_KV_HEADS, nq),
        in_specs=[tile(HEAD_DIM), tile(HEAD_DIM),
                  pl.BlockSpec((1, 1, nc, HEAD_DIM), lambda i, g, j: (i, g, 0, 0)),
                  pl.BlockSpec((1, 1, HEAD_DIM, nc), lambda i, g, j: (i, g, 0, 0)),
                  k_spec, vt_spec, k_spec, vt_spec, tile(SUBLANES)],
        out_specs=pl.BlockSpec((tq, HEADS_PER_GROUP * HEAD_DIM), lambda i, g, j: (i * nq + j, g)),
        out_shape=jax.ShapeDtypeStruct((batch * t, NSA_HEADS * HEAD_DIM), MXU_DTYPE),
        scratch_shapes=[pltpu.VMEM((t // SEL_BLOCK, tq), F32),
                        pltpu.VMEM((HEAD_DIM, rows), F32), pltpu.VMEM((HEAD_DIM, rows), F32),
                        pltpu.VMEM((NSA_TK, rows), F32)],
        compiler_params=_cparams(3),
        name="nsa",
    )(qn_t, qr_t, kvc, vct, ks, vst, kw, vwt, gates_t)


def _out_proj_kernel(h_ref, a_ref, b_ref, wa_ref, wb_ref, o_ref):
    o_ref[...] = h_ref[...] + _mm(a_ref[...], wa_ref[...]) + _mm(b_ref[...], wb_ref[...])


def _out_proj(h, a, b, wa, wb):
    n = h.shape[0]
    tm = min(ROW_TILE, n)
    full = lambda x: pl.BlockSpec(x.shape, lambda i: (0,) * x.ndim)
    return pl.pallas_call(
        _out_proj_kernel,
        grid=(n // tm,),
        in_specs=[pl.BlockSpec((tm, D_MODEL), lambda i: (i, 0)),
                  pl.BlockSpec((tm, a.shape[1]), lambda i: (i, 0)),
                  pl.BlockSpec((tm, b.shape[1]), lambda i: (i, 0)), full(wa), full(wb)],
        out_specs=pl.BlockSpec((tm, D_MODEL), lambda i: (i, 0)),
        out_shape=jax.ShapeDtypeStruct((n, D_MODEL), F32),
        compiler_params=_cparams(1),
        name="out_proj",
    )(h, a, b, wa, wb)


def _short_conv_kernel(h_ref, g_ref, win_ref, cw_ref, wout_ref, o_ref, tail_ref):
    tt = h_ref.shape[0]

    @pl.when(pl.program_id(1) == 0)
    def _():
        tail_ref[...] = jnp.zeros_like(tail_ref)

    h = h_ref[...]
    xn = _rms(h, g_ref[...]).astype(MXU_DTYPE)
    b_g = _mm(xn, win_ref[:, 0:D_MODEL])
    cv = _mm(xn, win_ref[:, D_MODEL:2 * D_MODEL]) * _mm(xn, win_ref[:, 2 * D_MODEL:3 * D_MODEL])
    xx = jnp.concatenate([tail_ref[...], cv], axis=0)
    tail_ref[...] = cv[tt - SUBLANES:tt]
    conv = sum(cw_ref[k:k + 1, :] * _shift_rows(xx, SC_CONV - 1 - k, tt) for k in range(SC_CONV))
    o_ref[...] = h + _mm((b_g * conv).astype(MXU_DTYPE), wout_ref[...])


def _short_conv(h, g, w_in, cw, w_out, batch):
    n = h.shape[0]
    t = n // batch
    tt = min(ROW_TILE, t)
    nt = t // tt
    row = pl.BlockSpec((tt, D_MODEL), lambda b, i: (b * nt + i, 0))
    full = lambda a: pl.BlockSpec(a.shape, lambda b, i: (0,) * a.ndim)
    return pl.pallas_call(
        _short_conv_kernel,
        grid=(batch, nt),
        in_specs=[row, full(g), full(w_in), full(cw), full(w_out)],
        out_specs=row,
        out_shape=jax.ShapeDtypeStruct((n, D_MODEL), F32),
        scratch_shapes=[pltpu.VMEM((SUBLANES, D_MODEL), F32)],
        compiler_params=_cparams(2),
        name="short_conv",
    )(h, g, w_in, cw, w_out)


def _mem_kv_kernel(m_ref, g_ref, w_ref, o_ref):
    o_ref[...] = _mm(_rms(m_ref[...], g_ref[...]).astype(MXU_DTYPE), w_ref[...]).astype(o_ref.dtype)


def _mem_kv(mem, g, wkv):
    n = mem.shape[0]
    tm = min(ROW_TILE, n)
    tn = 1024
    return pl.pallas_call(
        _mem_kv_kernel,
        grid=(n // tm, wkv.shape[1] // tn),
        in_specs=[pl.BlockSpec((tm, D_MODEL), lambda i, j: (i, 0)),
                  pl.BlockSpec(g.shape, lambda i, j: (0, 0)),
                  pl.BlockSpec((D_MODEL, tn), lambda i, j: (0, j))],
        out_specs=pl.BlockSpec((tm, tn), lambda i, j: (i, j)),
        out_shape=jax.ShapeDtypeStruct((n, wkv.shape[1]), MXU_DTYPE),
        compiler_params=_cparams(2),
        name="mem_kv",
    )(mem, g, wkv)


def _xattn_kernel(h_ref, g_ref, wq_ref, kv_ref, wo_ref, o_ref):
    h = h_ref[...]
    xn = _rms(h, g_ref[...]).astype(MXU_DTYPE)
    q = (_mm(xn, wq_ref[...]) * (XA_HEAD_DIM ** -0.5)).astype(MXU_DTYPE)
    width = XA_HEADS * XA_HEAD_DIM
    outs = []
    for hd in range(XA_HEADS):
        sl = slice(hd * XA_HEAD_DIM, (hd + 1) * XA_HEAD_DIM)
        s = _mm_nt(q[:, sl], kv_ref[:, sl])
        e = jnp.exp(s - jnp.max(s, axis=-1, keepdims=True))
        p = e / jnp.sum(e, axis=-1, keepdims=True)
        outs.append(_mm(p.astype(MXU_DTYPE), kv_ref[:, width + hd * XA_HEAD_DIM:width + (hd + 1) * XA_HEAD_DIM]))
    o = jnp.concatenate(outs, axis=1).astype(MXU_DTYPE)
    o_ref[...] = h + _mm(o, wo_ref[...])


def _xattn(h, g, wq, kv, wo, batch):
    n = h.shape[0]
    t = n // batch
    tm = min(ROW_TILE, t)
    nt = t // tm
    mlen = kv.shape[0] // batch
    full = lambda a: pl.BlockSpec(a.shape, lambda b, i: (0,) * a.ndim)
    row = pl.BlockSpec((tm, D_MODEL), lambda b, i: (b * nt + i, 0))
    return pl.pallas_call(
        _xattn_kernel,
        grid=(batch, nt),
        in_specs=[row, full(g), full(wq), pl.BlockSpec((mlen, kv.shape[1]), lambda b, i: (b, 0)), full(wo)],
        out_specs=row,
        out_shape=jax.ShapeDtypeStruct((n, D_MODEL), F32),
        compiler_params=_cparams(2),
        name="xattn",
    )(h, g, wq, kv, wo)


def _router_kernel(h_ref, g_ref, whi_ref, wlo_ref, b_ref, ri_ref, rw_ref, cnt_ref, xn_ref, carry_ref):
    tm = h_ref.shape[0]

    @pl.when(pl.program_id(0) == 0)
    def _():
        carry_ref[...] = jnp.zeros_like(carry_ref)

    xn = _rms(h_ref[...], g_ref[...])
    _to_token_tiles(xn_ref, xn)
    x_hi = xn.astype(MXU_DTYPE)
    x_lo = (xn - x_hi.astype(F32)).astype(MXU_DTYPE)
    logits = _mm(x_hi, whi_ref[...]) + (_mm(x_lo, whi_ref[...]) + _mm(x_hi, wlo_ref[...])) + b_ref[...]

    lane = lax.broadcasted_iota(jnp.int32, logits.shape, 1)
    lane_f = lane.astype(F32)
    none = float(LANES)
    is_g = lane < N_GROUPS
    g_max = jnp.max(jnp.where(is_g, logits, -jnp.inf), axis=-1, keepdims=True)
    g_sum = jnp.sum(jnp.where(is_g, jnp.exp(logits - g_max), 0.0), axis=-1, keepdims=True)
    g_top = 1.0 / g_sum
    g_idx = jnp.min(jnp.where(is_g & (logits == g_max), lane_f, none), axis=-1, keepdims=True)
    first = N_GROUPS + EXPERTS_PER_GROUP * g_idx
    in_g = (lane_f >= first) & (lane_f < first + EXPERTS_PER_GROUP)
    e1 = jnp.max(jnp.where(in_g, logits, -jnp.inf), axis=-1, keepdims=True)
    i1 = jnp.min(jnp.where(in_g & (logits == e1), lane_f, none), axis=-1, keepdims=True)
    rest = in_g & (lane_f != i1)
    e2 = jnp.max(jnp.where(rest, logits, -jnp.inf), axis=-1, keepdims=True)
    i2 = jnp.min(jnp.where(rest & (logits == e2), lane_f, none), axis=-1, keepdims=True)
    ratio = jnp.exp(e2 - e1)
    w1 = g_top / (1.0 + ratio)
    w2 = g_top * ratio / (1.0 + ratio)

    onehot = jnp.where((lane_f == i1) | (lane_f == i2), 1.0, 0.0)
    tri = jnp.where(lax.broadcasted_iota(jnp.int32, (tm, tm), 0) >= lax.broadcasted_iota(jnp.int32, (tm, tm), 1),
                    1.0, 0.0).astype(MXU_DTYPE)
    incl = _mm(tri, onehot.astype(MXU_DTYPE))
    before = incl - onehot + carry_ref[...]
    carry_ref[...] = carry_ref[...] + incl[tm - 1:tm, :]
    rank1 = jnp.sum(jnp.where(lane_f == i1, before, 0.0), axis=-1, keepdims=True)
    rank2 = jnp.sum(jnp.where(lane_f == i2, before, 0.0), axis=-1, keepdims=True)

    ri = jnp.where(lane == 0, i1 - N_GROUPS,
                   jnp.where(lane == 1, i2 - N_GROUPS, jnp.where(lane == 2, rank1, jnp.where(lane == 3, rank2, 0.0))))
    ri_ref[...] = ri.astype(jnp.int32)
    rw_ref[...] = jnp.where(lane == 0, w1, jnp.where(lane == 1, w2, 0.0))
    cnt_ref[...] = jnp.broadcast_to(carry_ref[...], cnt_ref.shape).astype(jnp.int32)


def _router(h, g, w_hi, w_lo, bias):
    n = h.shape[0]
    tm = min(ROW_TILE, n)
    full = lambda a: pl.BlockSpec(a.shape, lambda i: (0,) * a.ndim)
    return pl.pallas_call(
        _router_kernel,
        grid=(n // tm,),
        in_specs=[pl.BlockSpec((tm, D_MODEL), lambda i: (i, 0)), full(g), full(w_hi), full(w_lo), full(bias)],
        out_specs=[pl.BlockSpec((tm, LANES), lambda i: (i, 0)), pl.BlockSpec((tm, LANES), lambda i: (i, 0)),
                   pl.BlockSpec((SUBLANES, LANES), lambda i: (i, 0)),
                   pl.BlockSpec((tm * ROW_CHUNKS, LANES), lambda i: (i, 0))],
        out_shape=[jax.ShapeDtypeStruct((n, LANES), jnp.int32), jax.ShapeDtypeStruct((n, LANES), F32),
                   jax.ShapeDtypeStruct((n // tm * SUBLANES, LANES), jnp.int32),
                   jax.ShapeDtypeStruct((n * ROW_CHUNKS, LANES), F32)],
        scratch_shapes=[pltpu.VMEM((1, LANES), F32)],
        compiler_params=_cparams(1),
        name="router",
    )(h, g, w_hi, w_lo, bias)


def _row_copy(src_ref, src_row, dst_ref, dst_row, sem):
    src = src_ref.at[pl.ds(pl.multiple_of(src_row * ROW_CHUNKS, ROW_CHUNKS), ROW_CHUNKS)]
    dst = dst_ref.at[pl.ds(pl.multiple_of(dst_row * ROW_CHUNKS, ROW_CHUNKS), ROW_CHUNKS)]
    return pltpu.make_async_copy(src, dst, sem)


def _to_token_tiles(ref, x):
    rows = x.shape[0]
    for c in range(ROW_CHUNKS):
        ref[pl.ds(c, rows, stride=ROW_CHUNKS), :] = x[:, c * LANES:(c + 1) * LANES]


def _from_token_tiles(ref):
    rows = ref.shape[0] // ROW_CHUNKS
    return jnp.concatenate([ref[pl.ds(c, rows, stride=ROW_CHUNKS), :] for c in range(ROW_CHUNKS)], axis=1)


def _expert_kernel(code_ref, te_ref, tv_ref, xn_hbm, wg_ref, wu_ref, wd_ref, out_hbm,
                   xbuf, ybuf, gather_sem, scatter_sem):
    j = pl.program_id(0)
    last = pl.num_programs(0) - 1
    tile_rows = EXPERT_TILE * ROW_CHUNKS

    def start_gather(tile, buf):
        def issue(r, _):
            token = lax.shift_right_logical(code_ref[tile * EXPERT_TILE + r], 1)
            _row_copy(xn_hbm, token, xbuf.at[buf], r, gather_sem.at[buf]).start()
            return 0
        lax.fori_loop(0, EXPERT_TILE, issue, 0, unroll=DMA_UNROLL)

    def start_scatter(tile, buf):
        def issue(r, _):
            _row_copy(ybuf.at[buf], r, out_hbm, code_ref[tile * EXPERT_TILE + r], scatter_sem.at[buf]).start()
            return 0
        def issue_group(g, _):
            for u in range(DMA_UNROLL):
                issue(g * DMA_UNROLL + u, 0)
            return 0
        groups = tv_ref[tile] // DMA_UNROLL
        lax.fori_loop(0, groups, issue_group, 0)
        lax.fori_loop(groups * DMA_UNROLL, tv_ref[tile], issue, 0)

    def wait_scatter(tile, buf):
        rows = tv_ref[tile] * ROW_CHUNKS
        pltpu.make_async_copy(ybuf.at[buf, pl.ds(0, rows)], out_hbm.at[pl.ds(0, rows)], scatter_sem.at[buf]).wait()

    @pl.when(j == 0)
    def _():
        start_gather(0, 0)

    def step(buf):
        @pl.when((j >= 2) & (tv_ref[jnp.maximum(j - 2, 0)] > 0))
        def _():
            wait_scatter(j - 2, buf)

        @pl.when(tv_ref[j] > 0)
        def _():
            pltpu.make_async_copy(xn_hbm.at[pl.ds(0, tile_rows)], xbuf.at[buf], gather_sem.at[buf]).wait()

            @pl.when((j < last) & (tv_ref[jnp.minimum(j + 1, last)] > 0))
            def _():
                start_gather(j + 1, 1 - buf)

            x = _from_token_tiles(xbuf.at[buf]).astype(MXU_DTYPE)
            hid = jax.nn.silu(_mm(x, wg_ref[0])) * _mm(x, wu_ref[0])
            _to_token_tiles(ybuf.at[buf], _mm(hid.astype(MXU_DTYPE), wd_ref[0]))
            start_scatter(j, buf)

    for buf in range(2):
        pl.when(j % 2 == buf)(functools.partial(step, buf))


def _experts(code, tile_expert, tile_valid, xn_tiles, wg, wu, wd):
    n_tiles = tile_expert.shape[0]
    grid_spec = pltpu.PrefetchScalarGridSpec(
        num_scalar_prefetch=3,
        grid=(n_tiles,),
        in_specs=[pl.BlockSpec(memory_space=pl.ANY),
                  pl.BlockSpec((1, D_MODEL, D_EXPERT), lambda j, code, te, tv: (te[j], 0, 0)),
                  pl.BlockSpec((1, D_MODEL, D_EXPERT), lambda j, code, te, tv: (te[j], 0, 0)),
                  pl.BlockSpec((1, D_EXPERT, D_MODEL), lambda j, code, te, tv: (te[j], 0, 0))],
        out_specs=pl.BlockSpec(memory_space=pl.ANY),
        scratch_shapes=[pltpu.VMEM((2, EXPERT_TILE * ROW_CHUNKS, LANES), F32),
                        pltpu.VMEM((2, EXPERT_TILE * ROW_CHUNKS, LANES), F32),
                        pltpu.SemaphoreType.DMA((2,)), pltpu.SemaphoreType.DMA((2,))],
    )
    return pl.pallas_call(
        _expert_kernel,
        grid_spec=grid_spec,
        out_shape=jax.ShapeDtypeStruct((2 * xn_tiles.shape[0], LANES), F32),
        compiler_params=_cparams(1),
        name="experts",
    )(code, tile_expert, tile_valid, xn_tiles, wg, wu, wd)


def _combine_kernel(h_ref, rw_ref, g_ref, y_ref, o_ref, *, normalize):
    tm = h_ref.shape[0]
    stride = 2 * ROW_CHUNKS
    y1 = jnp.concatenate([y_ref[pl.ds(c, tm, stride=stride), :] for c in range(ROW_CHUNKS)], axis=1)
    y2 = jnp.concatenate([y_ref[pl.ds(ROW_CHUNKS + c, tm, stride=stride), :] for c in range(ROW_CHUNKS)], axis=1)
    rw = rw_ref[...]
    out = h_ref[...] + (rw[:, 0:1] * y1 + rw[:, 1:2] * y2)
    o_ref[...] = _rms(out, g_ref[...]) if normalize else out


def _combine(h, rw, g_final, ys, normalize):
    n = h.shape[0]
    tm = min(ROW_TILE, n)
    return pl.pallas_call(
        functools.partial(_combine_kernel, normalize=normalize),
        grid=(n // tm,),
        in_specs=[pl.BlockSpec((tm, D_MODEL), lambda i: (i, 0)),
                  pl.BlockSpec((tm, LANES), lambda i: (i, 0)),
                  pl.BlockSpec(g_final.shape, lambda i: (0, 0)),
                  pl.BlockSpec((tm * 2 * ROW_CHUNKS, LANES), lambda i: (i, 0))],
        out_specs=pl.BlockSpec((tm, D_MODEL), lambda i: (i, 0)),
        out_shape=jax.ShapeDtypeStruct((n, D_MODEL), F32),
        compiler_params=_cparams(1),
        name="combine",
    )(h, rw, g_final, ys)


def _moe(h, g, w_group, b_group, w_expert, b_expert, w_gate, w_up, w_down, g_final, last):
    n = h.shape[0]
    w_r = jnp.zeros((D_MODEL, LANES), F32).at[:, :N_GROUPS].set(w_group)
    w_r = w_r.at[:, N_GROUPS:N_GROUPS + N_EXPERTS].set(w_expert)
    b_r = jnp.zeros((1, LANES), F32).at[0, :N_GROUPS].set(b_group)
    b_r = b_r.at[0, N_GROUPS:N_GROUPS + N_EXPERTS].set(b_expert)
    w_hi = w_r.astype(MXU_DTYPE)
    w_lo = (w_r - w_hi.astype(F32)).astype(MXU_DTYPE)
    ri, rw, cnt, xn_tiles = _router(h, g, w_hi, w_lo, b_r)

    counts = cnt[-1, N_GROUPS:N_GROUPS + N_EXPERTS]
    padded = (counts + EXPERT_TILE - 1) // EXPERT_TILE * EXPERT_TILE
    ends = jnp.cumsum(padded)
    starts = ends - padded
    experts = jnp.arange(N_EXPERTS, dtype=jnp.int32)
    seg_start = jnp.sum(jnp.where(ri[:, 0:2, None] == experts, starts, 0), axis=-1)
    pos = (seg_start + ri[:, 2:4]).reshape(-1).astype(jnp.int32)
    n_tiles = 2 * n // EXPERT_TILE + N_EXPERTS + 1
    code = jnp.zeros((n_tiles * EXPERT_TILE,), jnp.int32).at[pos].set(
        jnp.arange(2 * n, dtype=jnp.int32), unique_indices=True)
    tile_start = jnp.arange(n_tiles, dtype=jnp.int32) * EXPERT_TILE
    tile_expert = jnp.minimum(jnp.sum(tile_start[:, None] >= ends[None, :], axis=1), N_EXPERTS - 1).astype(jnp.int32)
    seg_end = jnp.sum(jnp.where(tile_expert[:, None] == experts, starts + counts, 0), axis=-1)
    tile_valid = jnp.clip(seg_end - tile_start, 0, EXPERT_TILE).astype(jnp.int32)

    ys = _experts(code, tile_expert, tile_valid, xn_tiles, w_gate.astype(MXU_DTYPE), w_up.astype(MXU_DTYPE),
                  w_down.astype(MXU_DTYPE))
    return _combine(h, rw, g_final, ys, normalize=last)


def _even_weights(w_in):
    gate_cols = w_in[:, C_GATE:C_GATE + 3 * NSA_HEADS].reshape(D_MODEL, NSA_KV_HEADS, HEADS_PER_GROUP, 3)
    gate_cols = jnp.pad(gate_cols, ((0, 0), (0, 0), (0, 0), (0, SUBLANES - 3)))
    gate_cols = gate_cols.reshape(D_MODEL, NSA_KV_HEADS, HEADS_PER_GROUP * SUBLANES)
    gate_cols = jnp.pad(gate_cols, ((0, 0), (0, 0), (0, LANES - HEADS_PER_GROUP * SUBLANES)))
    return jnp.concatenate([w_in[:, :C_GATE], gate_cols.reshape(D_MODEL, NSA_KV_HEADS * LANES)], axis=1)


def _block_diag_halves(w):
    blocks = w.shape[0] // 2
    out = jnp.zeros((2, blocks * w.shape[1], blocks * w.shape[2]), w.dtype)
    for j in range(2):
        for k in range(blocks):
            out = out.at[j, k * w.shape[1]:(k + 1) * w.shape[1], k * w.shape[2]:(k + 1) * w.shape[2]].set(
                w[j * blocks + k])
    return out


def _even_mixer(h, g, batch, cos_t, sin_t, w_in, w_out, conv_w, conv_b, w_r, b_r, w_i, b_i, lam,
                pos_k, w1_k, w2_k, pos_v, w1_v, w2_v):
    n = h.shape[0]
    t = n // batch
    xl, gg, qn, qr, kcv, ksr, vs, kwr, vw, gates = _even_proj(
        h, g, _even_weights(w_in).astype(MXU_DTYPE), cos_t, sin_t)

    y_lru = _lru(xl, gg, conv_w, conv_b[None, :], _block_diag_halves(w_r).astype(MXU_DTYPE),
                 _block_diag_halves(w_i).astype(MXU_DTYPE), b_r[None, :], b_i[None, :],
                 jax.nn.softplus(-lam)[None, :], batch)

    nchunk = t // CMP_STRIDE
    x16 = kcv.reshape(batch, nchunk, CMP_STRIDE, 2 * NSA_KV_HEADS, HEAD_DIM).transpose(0, 3, 1, 2, 4)
    x16 = x16.reshape(batch, 2 * NSA_KV_HEADS, nchunk, CMP_STRIDE * HEAD_DIM)
    half = CMP_STRIDE * HEAD_DIM
    pos = jnp.stack([pos_k.reshape(2, 1, half), pos_v.reshape(2, 1, half)])
    w1 = jnp.stack([w1_k.reshape(2, half, CMP_HIDDEN), w1_v.reshape(2, half, CMP_HIDDEN)]).astype(MXU_DTYPE)
    w2 = jnp.stack([w2_k, w2_v]).astype(MXU_DTYPE)
    kvc = _compress(x16, pos, w1, w2)

    vct = kvc[:, NSA_KV_HEADS:].transpose(0, 1, 3, 2)
    y_nsa = _nsa(qn, qr, kvc, vct, ksr, vs, kwr, vw, gates, batch)
    w_out = w_out.astype(MXU_DTYPE)
    return _out_proj(h, y_lru, y_nsa, w_out[:LRU_WIDTH], w_out[LRU_WIDTH:])


def _rope_tables(positions):
    inv = ROPE_THETA ** (-jnp.arange(0, 2 * ROT_HALF, 2, dtype=F32) / (2 * ROT_HALF))
    ang = positions.reshape(-1).astype(F32)[:, None] * inv
    cos, sin = jnp.cos(ang), jnp.sin(ang)
    rest = HEAD_DIM - 2 * ROT_HALF
    cos_h = jnp.concatenate([cos, cos, jnp.ones((cos.shape[0], rest), F32)], axis=1)
    sin_h = jnp.concatenate([-sin, sin, jnp.zeros((cos.shape[0], rest), F32)], axis=1)
    reps = LANES // HEAD_DIM
    return jnp.tile(cos_h, (1, reps)), jnp.tile(sin_h, (1, reps))


def kernel(x, mem, positions, norm_mix, norm_xattn, norm_mem, norm_ffn, norm_final, even_w_in, even_w_out, lru_conv_w, lru_conv_b, lru_w_r, lru_b_r, lru_w_i, lru_b_i, lru_lambda, nsa_cmp_pos_k, nsa_cmp_w1_k, nsa_cmp_w2_k, nsa_cmp_pos_v, nsa_cmp_w1_v, nsa_cmp_w2_v, odd_w_in, odd_conv_w, odd_w_out, xa_wq, xa_wk, xa_wv, xa_wo, moe_w_group, moe_b_group, moe_w_expert, moe_b_expert, moe_w_gate, moe_w_up, moe_w_down):
    batch, t, d = x.shape
    n = batch * t
    depth = norm_mix.shape[0]
    cos_t, sin_t = _rope_tables(positions)
    h = x.reshape(n, d)
    mem2 = mem.reshape(-1, d)
    for layer in range(depth):
        g_mix = norm_mix[layer][None, :]
        if layer % 2 == 0:
            e = layer // 2
            h = _even_mixer(h, g_mix, batch, cos_t, sin_t, even_w_in[e], even_w_out[e], lru_conv_w[e],
                            lru_conv_b[e], lru_w_r[e], lru_b_r[e], lru_w_i[e], lru_b_i[e], lru_lambda[e],
                            nsa_cmp_pos_k[e], nsa_cmp_w1_k[e], nsa_cmp_w2_k[e],
                            nsa_cmp_pos_v[e], nsa_cmp_w1_v[e], nsa_cmp_w2_v[e])
        else:
            o = layer // 2
            h = _short_conv(h, g_mix, odd_w_in[o].astype(MXU_DTYPE), odd_conv_w[o],
                            odd_w_out[o].astype(MXU_DTYPE), batch)
        wkv = jnp.concatenate([xa_wk[layer], xa_wv[layer]], axis=1).astype(MXU_DTYPE)
        kv = _mem_kv(mem2, norm_mem[layer][None, :], wkv)
        h = _xattn(h, norm_xattn[layer][None, :], xa_wq[layer].astype(MXU_DTYPE), kv,
                   xa_wo[layer].astype(MXU_DTYPE), batch)
        h = _moe(h, norm_ffn[layer][None, :], moe_w_group[layer], moe_b_group[layer], moe_w_expert[layer],
                 moe_b_expert[layer], moe_w_gate[layer], moe_w_up[layer], moe_w_down[layer],
                 norm_final[None, :], last=layer == depth - 1)
    return h.reshape(batch, t, d)
```

```python
import functools

import jax
import jax.numpy as jnp
import numpy as np
from jax import lax
from jax.experimental import pallas as pl
from jax.experimental.pallas import tpu as pltpu

F32 = jnp.float32
MXU_DTYPE = jnp.bfloat16

D_MODEL = 1024
LRU_WIDTH = 512
LRU_CONV = 4
LRU_C = 8.0
NSA_HEADS = 8
NSA_KV_HEADS = 2
HEADS_PER_GROUP = NSA_HEADS // NSA_KV_HEADS
HEAD_DIM = 64
CMP_STRIDE = 16
CMP_LEN = 32
CMP_HIDDEN = 128
SEL_BLOCK = 64
SEL_TOP = 16
WINDOW = 512
ROT_HALF = 8
ROPE_THETA = 500000.0
SC_CONV = 3
XA_HEADS = 4
XA_HEAD_DIM = 256
N_GROUPS = 4
EXPERTS_PER_GROUP = 4
N_EXPERTS = 16
D_EXPERT = 512
EPS = 1e-6
NEG = -1e30
LOG2E = 1.4426950408889634
FORCE = 1e9

LANES = 128
SUBLANES = 8
VMEM_LIMIT = 56 * 1024 * 1024

C_XL, C_GL, C_Q, C_KCV, C_KS, C_VS, C_KW, C_VW, C_GATE, C_END = (
    0, 512, 1024, 1536, 1792, 1920, 2048, 2176, 2304, 2560)

ROW_TILE = 512
NSA_TQ = 256
NSA_TK = 512
EXPERT_TILE = 256
DMA_UNROLL = 8
ROW_CHUNKS = D_MODEL // LANES


def _cparams(n_axes):
    return pltpu.CompilerParams(dimension_semantics=("arbitrary",) * n_axes,
                                vmem_limit_bytes=VMEM_LIMIT)


def _mm(a, b):
    return jnp.dot(a, b, preferred_element_type=F32)


def _mm_nt(a, b):
    return lax.dot_general(a, b, (((1,), (1,)), ((), ())), preferred_element_type=F32)


def _rms(x, g):
    return x * lax.rsqrt(jnp.mean(x * x, axis=-1, keepdims=True) + EPS) * g


def _shift_rows(xx, s, rows):
    if s == 0:
        return xx[SUBLANES:SUBLANES + rows]
    return pltpu.roll(xx, s, 0)[SUBLANES:SUBLANES + rows]


def _rope(x, cos_t, sin_t):
    width = x.shape[-1]
    lane = lax.broadcasted_iota(jnp.int32, x.shape, 1) % HEAD_DIM
    partner = jnp.where(lane < ROT_HALF, pltpu.roll(x, width - ROT_HALF, 1), pltpu.roll(x, ROT_HALF, 1))
    return x * cos_t + partner * sin_t


def _even_proj_kernel(h_ref, g_ref, w_ref, cos_ref, sin_ref,
                      xl_ref, gg_ref, qn_ref, qr_ref, kcv_ref, ksr_ref, vs_ref, kwr_ref, vw_ref,
                      gate_ref):
    xn = _rms(h_ref[...], g_ref[...]).astype(MXU_DTYPE)

    def proj(c0, c1):
        return _mm(xn, w_ref[:, c0:c1])

    cos_t = cos_ref[...]
    sin_t = sin_ref[...]
    xl_ref[...] = proj(C_XL, C_GL)
    gg_ref[...] = jax.nn.gelu(proj(C_GL, C_Q))
    q = proj(C_Q, C_KCV) * (HEAD_DIM ** -0.5)
    reps = (C_KCV - C_Q) // LANES
    cos_q = jnp.concatenate([cos_t] * reps, axis=1)
    sin_q = jnp.concatenate([sin_t] * reps, axis=1)
    _store_query_tiles(qn_ref, q, HEAD_DIM)
    _store_query_tiles(qr_ref, _rope(q, cos_q, sin_q) * LOG2E, HEAD_DIM)
    kcv_ref[...] = proj(C_KCV, C_KS)
    ksr_ref[...] = _rope(proj(C_KS, C_VS), cos_t, sin_t).astype(ksr_ref.dtype)
    vs_ref[...] = proj(C_VS, C_KW).T.astype(vs_ref.dtype)
    kwr_ref[...] = _rope(proj(C_KW, C_VW), cos_t, sin_t).astype(kwr_ref.dtype)
    vw_ref[...] = proj(C_VW, C_GATE).T.astype(vw_ref.dtype)
    _store_query_tiles(gate_ref, jax.nn.sigmoid(proj(C_GATE, C_END)), SUBLANES)


def _store_query_tiles(ref, x, per_head):
    group_width = x.shape[1] // NSA_KV_HEADS
    for grp in range(NSA_KV_HEADS):
        for j in range(x.shape[0] // NSA_TQ):
            blk = x[j * NSA_TQ:(j + 1) * NSA_TQ, grp * group_width:(grp + 1) * group_width].T
            ref[grp, j] = jnp.concatenate([blk[hh * per_head:(hh + 1) * per_head, :]
                                           for hh in range(HEADS_PER_GROUP)], axis=1).astype(ref.dtype)


def _even_proj(h, g, w, cos_t, sin_t):
    n = h.shape[0]
    tm = min(ROW_TILE, n)
    rows = HEADS_PER_GROUP * NSA_TQ
    row = lambda c: pl.BlockSpec((tm, c), lambda i: (i, 0))
    col = pl.BlockSpec((LANES, tm), lambda i: (0, i))
    full = lambda a: pl.BlockSpec(a.shape, lambda i: (0,) * a.ndim)
    tiles = lambda c: pl.BlockSpec((NSA_KV_HEADS, tm // NSA_TQ, c, rows), lambda i: (0, i, 0, 0))
    tiles_shape = lambda c, dt: jax.ShapeDtypeStruct((NSA_KV_HEADS, n // NSA_TQ, c, rows), dt)
    flat = lambda c, dt: jax.ShapeDtypeStruct((n, c), dt)
    return pl.pallas_call(
        _even_proj_kernel,
        grid=(n // tm,),
        in_specs=[row(D_MODEL), full(g), full(w), row(LANES), row(LANES)],
        out_specs=[row(512), row(512), tiles(HEAD_DIM), tiles(HEAD_DIM), row(256),
                   row(LANES), col, row(LANES), col, tiles(SUBLANES)],
        out_shape=[flat(512, F32), flat(512, F32), tiles_shape(HEAD_DIM, MXU_DTYPE), tiles_shape(HEAD_DIM, MXU_DTYPE),
                   flat(256, F32), flat(LANES, MXU_DTYPE), jax.ShapeDtypeStruct((LANES, n), MXU_DTYPE),
                   flat(LANES, MXU_DTYPE), jax.ShapeDtypeStruct((LANES, n), MXU_DTYPE),
                   tiles_shape(SUBLANES, F32)],
        compiler_params=_cparams(1),
        name="even_proj",
    )(h, g, w, cos_t, sin_t)


def _lru_kernel(xl_ref, gg_ref, cw_ref, cb_ref, wr_ref, wi_ref, br_ref, bi_ref, sp_ref,
                y_ref, tail_ref, h_ref, a_scr, u_scr):
    tt = xl_ref.shape[0]

    @pl.when(pl.program_id(1) == 0)
    def _():
        tail_ref[...] = jnp.zeros_like(tail_ref)
        h_ref[...] = jnp.zeros_like(h_ref)

    x = xl_ref[...]
    xx = jnp.concatenate([tail_ref[...], x], axis=0)
    tail_ref[...] = x[tt - SUBLANES:tt]
    xc = cb_ref[...] + sum(cw_ref[k:k + 1, :] * _shift_rows(xx, LRU_CONV - 1 - k, tt)
                           for k in range(LRU_CONV))
    xcb = xc.astype(MXU_DTYPE)
    half = LRU_WIDTH // 2
    r_lin = jnp.concatenate([_mm(xcb[:, j * half:(j + 1) * half], wr_ref[j]) for j in range(2)], axis=1)
    i_lin = jnp.concatenate([_mm(xcb[:, j * half:(j + 1) * half], wi_ref[j]) for j in range(2)], axis=1)
    r = jax.nn.sigmoid(r_lin + br_ref[...])
    i = jax.nn.sigmoid(i_lin + bi_ref[...])
    log_a = -LRU_C * r * sp_ref[...]
    a = jnp.exp(log_a)
    u = jnp.sqrt(jnp.tanh(-log_a) * (1.0 + a * a)) * (i * xc)

    r8 = lax.broadcasted_iota(jnp.int32, a.shape, 0) % SUBLANES
    for s in (1, 2, 4):
        keep = r8 >= s
        u = jnp.where(keep, a * pltpu.roll(u, s, 0) + u, u)
        a = jnp.where(keep, a * pltpu.roll(a, s, 0), a)
    a_scr[...] = a
    u_scr[...] = u

    def body(gidx, h):
        r0 = pl.multiple_of(gidx * SUBLANES, SUBLANES)
        out = a_scr[pl.ds(r0, SUBLANES), :] * h + u_scr[pl.ds(r0, SUBLANES), :]
        u_scr[pl.ds(r0, SUBLANES), :] = out
        return out[SUBLANES - 1:SUBLANES, :]

    h_ref[...] = lax.fori_loop(0, tt // SUBLANES, body, h_ref[...])
    y_ref[...] = (u_scr[...] * gg_ref[...]).astype(y_ref.dtype)


def _lru(xl, gg, cw, cb, wr, wi, br, bi, sp, batch):
    n = xl.shape[0]
    t = n // batch
    tt = min(ROW_TILE, t)
    nt = t // tt
    row = pl.BlockSpec((tt, LRU_WIDTH), lambda b, i: (b * nt + i, 0))
    full = lambda a: pl.BlockSpec(a.shape, lambda b, i: (0,) * a.ndim)
    return pl.pallas_call(
        _lru_kernel,
        grid=(batch, nt),
        in_specs=[row, row] + [full(a) for a in (cw, cb, wr, wi, br, bi, sp)],
        out_specs=row,
        out_shape=jax.ShapeDtypeStruct((n, LRU_WIDTH), MXU_DTYPE),
        scratch_shapes=[pltpu.VMEM((SUBLANES, LRU_WIDTH), F32), pltpu.VMEM((1, LRU_WIDTH), F32),
                        pltpu.VMEM((tt, LRU_WIDTH), F32), pltpu.VMEM((tt, LRU_WIDTH), F32)],
        compiler_params=_cparams(2),
        name="lru",
    )(xl, gg, cw, cb, wr, wi, br, bi, sp)


def _compress_kernel(x_ref, pos_ref, w1_ref, w2_ref, o_ref):
    x = x_ref[0, 0]
    nchunk = x.shape[0]
    lo = (x + pos_ref[0, 0]).astype(MXU_DTYPE)
    hi = (x + pos_ref[0, 1]).astype(MXU_DTYPE)
    p_lo = _mm(lo, w1_ref[0, 0])
    p_hi = _mm(hi, w1_ref[0, 1])
    a = p_lo + pltpu.roll(p_hi, nchunk - 1, 0)
    o_ref[0, 0] = _mm(jax.nn.gelu(a).astype(MXU_DTYPE), w2_ref[0]).astype(o_ref.dtype)


def _compress(x16, pos, w1, w2):
    b, _, nchunk, width = x16.shape
    return pl.pallas_call(
        _compress_kernel,
        grid=(b, 2 * NSA_KV_HEADS),
        in_specs=[pl.BlockSpec((1, 1, nchunk, width), lambda i, j: (i, j, 0, 0)),
                  pl.BlockSpec((1, 2, 1, width), lambda i, j: (j // NSA_KV_HEADS, 0, 0, 0)),
                  pl.BlockSpec((1, 2, width, CMP_HIDDEN), lambda i, j: (j // NSA_KV_HEADS, 0, 0, 0)),
                  pl.BlockSpec((1, CMP_HIDDEN, HEAD_DIM), lambda i, j: (j // NSA_KV_HEADS, 0, 0))],
        out_specs=pl.BlockSpec((1, 1, nchunk, HEAD_DIM), lambda i, j: (i, j, 0, 0)),
        out_shape=jax.ShapeDtypeStruct((b, 2 * NSA_KV_HEADS, nchunk, HEAD_DIM), MXU_DTYPE),
        compiler_params=_cparams(2),
        name="compress",
    )(x16, pos, w1, w2)


def _nsa_kernel(qn_ref, qr_ref, kc_ref, vct_ref, ks_ref, vst_ref, kw_ref, vwt_ref, gate_ref,
                y_ref, chosen_scr, acc_scr, accw_scr, sc_scr):
    rows = qn_ref.shape[3]
    tq = rows // HEADS_PER_GROUP
    t = ks_ref.shape[0]
    nc = kc_ref.shape[2]
    nsel = t // SEL_BLOCK
    n_top = min(SEL_TOP, nsel)
    tk = NSA_TK
    t0 = pl.program_id(2) * tq
    qn = qn_ref[0, 0]
    grp = pl.program_id(1)
    qr64 = qr_ref[0, 0]
    qr = jnp.concatenate([jnp.where(grp == gi, qr64, jnp.zeros_like(qr64)) for gi in range(NSA_KV_HEADS)], axis=0)

    s = _mm(kc_ref[0, 0], qn)
    tq_row = t0 + lax.broadcasted_iota(jnp.int32, (nc, rows), 1) % tq
    cmp_end = lax.broadcasted_iota(jnp.int32, (nc, rows), 0) * CMP_STRIDE + (CMP_LEN - 1)
    valid = cmp_end <= tq_row
    s = jnp.where(valid, s, NEG)
    e = jnp.where(valid, jnp.exp(s - jnp.max(s, axis=0, keepdims=True)), 0.0)
    den = jnp.sum(e, axis=0, keepdims=True)
    p = e * (1.0 / jnp.where(den > 0.0, den, 1.0))
    o_cmp = _mm(vct_ref[0, 0], p.astype(MXU_DTYPE))

    psum = p[:, 0:tq]
    for hh in range(1, HEADS_PER_GROUP):
        psum = psum + p[:, hh * tq:(hh + 1) * tq]
    p_hi = psum.astype(MXU_DTYPE)
    p_lo = (psum - p_hi.astype(F32)).astype(MXU_DTYPE)
    cj = lax.broadcasted_iota(jnp.int32, (nsel, nc), 0)
    cn = lax.broadcasted_iota(jnp.int32, (nsel, nc), 1)
    ratio = SEL_BLOCK // CMP_STRIDE
    cover = jnp.where((cn >= ratio * cj - (CMP_LEN // CMP_STRIDE - 1)) & (cn <= ratio * cj + ratio - 1)
                      & (cn < nc - 1), 1.0, 0.0).astype(MXU_DTYPE)
    imp = _mm(cover, p_hi) + _mm(cover, p_lo)

    blk = lax.broadcasted_iota(jnp.int32, (nsel, tq), 0)
    tq_col = t0 + lax.broadcasted_iota(jnp.int32, (nsel, tq), 1)
    cur = tq_col // SEL_BLOCK
    forced = (blk == 0) | (blk == cur) | (blk == cur - 1)
    causal = blk * SEL_BLOCK <= tq_col
    score0 = jnp.where(causal, jnp.where(forced, FORCE, imp), NEG)
    blk_f = blk.astype(F32)

    def pick_one(_, carry):
        score, chosen = carry
        best = jnp.max(score, axis=0, keepdims=True)
        first = jnp.min(jnp.where(score == best, blk_f, float(nsel)), axis=0, keepdims=True)
        hit = blk_f == first
        return jnp.where(hit, -jnp.inf, score), jnp.where(hit, 1.0, chosen)

    _, chosen = lax.fori_loop(0, n_top, pick_one, (score0, jnp.zeros((nsel, tq), F32)))
    chosen_scr[...] = chosen

    def scores(k_ref, kt, bias):
        k0 = pl.multiple_of(kt * tk, tk)
        return _mm(k_ref[pl.ds(k0, tk), :], qr) + jnp.concatenate([bias] * HEADS_PER_GROUP, axis=1)

    def soft(sc, m_old, l_old):
        m_new = jnp.maximum(m_old, jnp.max(sc, axis=0, keepdims=True))
        alpha = jnp.exp2(m_old - m_new)
        pe = jnp.exp2(sc - m_new)
        return m_new, alpha * l_old + jnp.sum(pe, axis=0, keepdims=True), alpha, pe.astype(MXU_DTYPE)

    def accumulate(vt_ref, acc_ref, kt, alpha, pe):
        k0 = pl.multiple_of(kt * tk, tk)
        acc_ref[...] = alpha * acc_ref[...] + _mm(vt_ref[:, pl.ds(k0, tk)], pe)

    expand = jnp.where(lax.broadcasted_iota(jnp.int32, (tk, SUBLANES), 0) // SEL_BLOCK
                       == lax.broadcasted_iota(jnp.int32, (tk, SUBLANES), 1), 1.0, 0.0)

    def key_minus_query(kt):
        return (kt * tk - t0 + lax.broadcasted_iota(jnp.int32, (tk, tq), 0)
                - lax.broadcasted_iota(jnp.int32, (tk, tq), 1))

    def sel_bias(kt):
        grp = pl.multiple_of(kt * SUBLANES, SUBLANES)
        bias = _mm(expand, (chosen_scr[pl.ds(grp, SUBLANES), :] - 1.0) * (-NEG))
        return jnp.where(key_minus_query(kt) <= 0, bias, NEG)

    def win_bias(kt, diagonal):
        d = key_minus_query(kt)
        inside = (d > -WINDOW) & (d <= 0) if diagonal else d > -WINDOW
        return jnp.where(inside, 0.0, NEG)

    def sel_only(kt, carry):
        m_s, l_s, m_w, l_w = carry
        sc_next = scores(ks_ref, kt + 1, sel_bias(kt + 1))
        m_s, l_s, alpha, pe = soft(sc_scr[...], m_s, l_s)
        accumulate(vst_ref, acc_scr, kt, alpha, pe)
        sc_scr[...] = sc_next
        return m_s, l_s, m_w, l_w

    def both(kt, carry, last=False):
        m_s, l_s, m_w, l_w = carry
        sc_w = scores(kw_ref, kt, win_bias(kt, last))
        if not last:
            sc_next = scores(ks_ref, kt + 1, sel_bias(kt + 1))
        m_s, l_s, alpha_s, pe_s = soft(sc_scr[...], m_s, l_s)
        m_w, l_w, alpha_w, pe_w = soft(sc_w, m_w, l_w)
        accumulate(vst_ref, acc_scr, kt, alpha_s, pe_s)
        accumulate(vwt_ref, accw_scr, kt, alpha_w, pe_w)
        if not last:
            sc_scr[...] = sc_next
        return m_s, l_s, m_w, l_w

    kt_last = (t0 + tq - 1) // tk
    win_lo = jnp.maximum(t0 - (WINDOW - 1), 0) // tk
    acc_scr[...] = jnp.zeros_like(acc_scr)
    accw_scr[...] = jnp.zeros_like(accw_scr)
    lowest = jnp.full((1, rows), NEG, F32)
    zero = jnp.zeros((1, rows), F32)
    sc_scr[...] = scores(ks_ref, 0, sel_bias(0))
    carry = lax.fori_loop(0, win_lo, sel_only, (lowest, zero, lowest, zero))
    carry = lax.fori_loop(win_lo, kt_last, both, carry)
    _, l_s, _, l_w = both(kt_last, carry, last=True)
    o_sel = acc_scr[...] * (1.0 / l_s)
    o_win = accw_scr[...] * (1.0 / l_w)

    gate = gate_ref[0, 0]
    y = gate[0:1] * o_cmp + gate[1:2] * o_sel + gate[2:3] * o_win
    y = jnp.concatenate([y[:, hh * tq:(hh + 1) * tq] for hh in range(HEADS_PER_GROUP)], axis=0)
    y_ref[...] = y.T.astype(y_ref.dtype)


def _nsa(qn_t, qr_t, kvc, vct, ks, vst, kw, vwt, gates_t, batch):
    _, n_tiles, _, rows = qn_t.shape
    tq = rows // HEADS_PER_GROUP
    nq = n_tiles // batch
    t = nq * tq
    nc = kvc.shape[2]
    assert NSA_TK == SEL_BLOCK * SUBLANES and t % NSA_TK == 0
    tile = lambda c: pl.BlockSpec((1, 1, c, rows), lambda i, g, j: (g, i * nq + j, 0, 0))
    k_spec = pl.BlockSpec((t, LANES), lambda i, g, j: (i, 0))
    vt_spec = pl.BlockSpec((HEAD_DIM, t), lambda i, g, j: (g, i))
    return pl.pallas_call(
        _nsa_kernel,
        grid=(batch, NSA_KV_HEADS, nq),
        in_specs=[tile(HEAD_DIM), tile(HEAD_DIM),
                  pl.BlockSpec((1, 1, nc, HEAD_DIM), lambda i, g, j: (i, g, 0, 0)),
                  pl.BlockSpec((1, 1, HEAD_DIM, nc), lambda i, g, j: (i, g, 0, 0)),
                  k_spec, vt_spec, k_spec, vt_spec, tile(SUBLANES)],
        out_specs=pl.BlockSpec((tq, HEADS_PER_GROUP * HEAD_DIM), lambda i, g, j: (i * nq + j, g)),
        out_shape=jax.ShapeDtypeStruct((batch * t, NSA_HEADS * HEAD_DIM), MXU_DTYPE),
        scratch_shapes=[pltpu.VMEM((t // SEL_BLOCK, tq), F32),
                        pltpu.VMEM((HEAD_DIM, rows), F32), pltpu.VMEM((HEAD_DIM, rows), F32),
                        pltpu.VMEM((NSA_TK, rows), F32)],
        compiler_params=_cparams(3),
        name="nsa",
    )(qn_t, qr_t, kvc, vct, ks, vst, kw, vwt, gates_t)


def _out_proj_kernel(h_ref, a_ref, b_ref, wa_ref, wb_ref, o_ref):
    o_ref[...] = h_ref[...] + _mm(a_ref[...], wa_ref[...]) + _mm(b_ref[...], wb_ref[...])


def _out_proj(h, a, b, wa, wb):
    n = h.shape[0]
    tm = min(ROW_TILE, n)
    full = lambda x: pl.BlockSpec(x.shape, lambda i: (0,) * x.ndim)
    return pl.pallas_call(
        _out_proj_kernel,
        grid=(n // tm,),
        in_specs=[pl.BlockSpec((tm, D_MODEL), lambda i: (i, 0)),
                  pl.BlockSpec((tm, a.shape[1]), lambda i: (i, 0)),
                  pl.BlockSpec((tm, b.shape[1]), lambda i: (i, 0)), full(wa), full(wb)],
        out_specs=pl.BlockSpec((tm, D_MODEL), lambda i: (i, 0)),
        out_shape=jax.ShapeDtypeStruct((n, D_MODEL), F32),
        compiler_params=_cparams(1),
        name="out_proj",
    )(h, a, b, wa, wb)


def _short_conv_kernel(h_ref, g_ref, win_ref, cw_ref, wout_ref, o_ref, tail_ref):
    tt = h_ref.shape[0]

    @pl.when(pl.program_id(1) == 0)
    def _():
        tail_ref[...] = jnp.zeros_like(tail_ref)

    h = h_ref[...]
    xn = _rms(h, g_ref[...]).astype(MXU_DTYPE)
    b_g = _mm(xn, win_ref[:, 0:D_MODEL])
    cv = _mm(xn, win_ref[:, D_MODEL:2 * D_MODEL]) * _mm(xn, win_ref[:, 2 * D_MODEL:3 * D_MODEL])
    xx = jnp.concatenate([tail_ref[...], cv], axis=0)
    tail_ref[...] = cv[tt - SUBLANES:tt]
    conv = sum(cw_ref[k:k + 1, :] * _shift_rows(xx, SC_CONV - 1 - k, tt) for k in range(SC_CONV))
    o_ref[...] = h + _mm((b_g * conv).astype(MXU_DTYPE), wout_ref[...])


def _short_conv(h, g, w_in, cw, w_out, batch):
    n = h.shape[0]
    t = n // batch
    tt = min(ROW_TILE, t)
    nt = t // tt
    row = pl.BlockSpec((tt, D_MODEL), lambda b, i: (b * nt + i, 0))
    full = lambda a: pl.BlockSpec(a.shape, lambda b, i: (0,) * a.ndim)
    return pl.pallas_call(
        _short_conv_kernel,
        grid=(batch, nt),
        in_specs=[row, full(g), full(w_in), full(cw), full(w_out)],
        out_specs=row,
        out_shape=jax.ShapeDtypeStruct((n, D_MODEL), F32),
        scratch_shapes=[pltpu.VMEM((SUBLANES, D_MODEL), F32)],
        compiler_params=_cparams(2),
        name="short_conv",
    )(h, g, w_in, cw, w_out)


def _mem_kv_kernel(m_ref, g_ref, w_ref, o_ref):
    o_ref[...] = _mm(_rms(m_ref[...], g_ref[...]).astype(MXU_DTYPE), w_ref[...]).astype(o_ref.dtype)


def _mem_kv(mem, g, wkv):
    n = mem.shape[0]
    tm = min(ROW_TILE, n)
    tn = 1024
    return pl.pallas_call(
        _mem_kv_kernel,
        grid=(n // tm, wkv.shape[1] // tn),
        in_specs=[pl.BlockSpec((tm, D_MODEL), lambda i, j: (i, 0)),
                  pl.BlockSpec(g.shape, lambda i, j: (0, 0)),
                  pl.BlockSpec((D_MODEL, tn), lambda i, j: (0, j))],
        out_specs=pl.BlockSpec((tm, tn), lambda i, j: (i, j)),
        out_shape=jax.ShapeDtypeStruct((n, wkv.shape[1]), MXU_DTYPE),
        compiler_params=_cparams(2),
        name="mem_kv",
    )(mem, g, wkv)


def _xattn_kernel(h_ref, g_ref, wq_ref, kv_ref, wo_ref, o_ref):
    h = h_ref[...]
    xn = _rms(h, g_ref[...]).astype(MXU_DTYPE)
    q = (_mm(xn, wq_ref[...]) * (XA_HEAD_DIM ** -0.5)).astype(MXU_DTYPE)
    width = XA_HEADS * XA_HEAD_DIM
    outs = []
    for hd in range(XA_HEADS):
        sl = slice(hd * XA_HEAD_DIM, (hd + 1) * XA_HEAD_DIM)
        s = _mm_nt(q[:, sl], kv_ref[:, sl])
        e = jnp.exp(s - jnp.max(s, axis=-1, keepdims=True))
        p = e / jnp.sum(e, axis=-1, keepdims=True)
        outs.append(_mm(p.astype(MXU_DTYPE), kv_ref[:, width + hd * XA_HEAD_DIM:width + (hd + 1) * XA_HEAD_DIM]))
    o = jnp.concatenate(outs, axis=1).astype(MXU_DTYPE)
    o_ref[...] = h + _mm(o, wo_ref[...])


def _xattn(h, g, wq, kv, wo, batch):
    n = h.shape[0]
    t = n // batch
    tm = min(ROW_TILE, t)
    nt = t // tm
    mlen = kv.shape[0] // batch
    full = lambda a: pl.BlockSpec(a.shape, lambda b, i: (0,) * a.ndim)
    row = pl.BlockSpec((tm, D_MODEL), lambda b, i: (b * nt + i, 0))
    return pl.pallas_call(
        _xattn_kernel,
        grid=(batch, nt),
        in_specs=[row, full(g), full(wq), pl.BlockSpec((mlen, kv.shape[1]), lambda b, i: (b, 0)), full(wo)],
        out_specs=row,
        out_shape=jax.ShapeDtypeStruct((n, D_MODEL), F32),
        compiler_params=_cparams(2),
        name="xattn",
    )(h, g, wq, kv, wo)


ROUTER_ROWS = 32


def _router_kernel(h_ref, g_ref, whi_ref, wlo_ref, b_ref, ri_ref, rw_ref, cnt_ref, carry_ref):
    tm = h_ref.shape[0]

    @pl.when(pl.program_id(0) == 0)
    def _():
        carry_ref[...] = jnp.zeros_like(carry_ref)

    xn = _rms(h_ref[...], g_ref[...])
    x_hi = xn.astype(MXU_DTYPE)
    x_lo = (xn - x_hi.astype(F32)).astype(MXU_DTYPE)
    logits = (_mm_nt(whi_ref[...], x_hi) + (_mm_nt(whi_ref[...], x_lo) + _mm_nt(wlo_ref[...], x_hi))
              + b_ref[:, 0:1])

    row = lax.broadcasted_iota(jnp.int32, logits.shape, 0)
    row_f = row.astype(F32)
    none = float(ROUTER_ROWS)
    is_g = row < N_GROUPS
    g_max = jnp.max(jnp.where(is_g, logits, -jnp.inf), axis=0, keepdims=True)
    g_sum = jnp.sum(jnp.where(is_g, jnp.exp(logits - g_max), 0.0), axis=0, keepdims=True)
    g_top = 1.0 / g_sum
    g_idx = jnp.min(jnp.where(is_g & (logits == g_max), row_f, none), axis=0, keepdims=True)
    first = N_GROUPS + EXPERTS_PER_GROUP * g_idx
    in_g = (row_f >= first) & (row_f < first + EXPERTS_PER_GROUP)
    e1 = jnp.max(jnp.where(in_g, logits, -jnp.inf), axis=0, keepdims=True)
    i1 = jnp.min(jnp.where(in_g & (logits == e1), row_f, none), axis=0, keepdims=True)
    rest = in_g & (row_f != i1)
    e2 = jnp.max(jnp.where(rest, logits, -jnp.inf), axis=0, keepdims=True)
    i2 = jnp.min(jnp.where(rest & (logits == e2), row_f, none), axis=0, keepdims=True)
    ratio = jnp.exp(e2 - e1)
    w1 = g_top / (1.0 + ratio)
    w2 = g_top * ratio / (1.0 + ratio)

    onehot = jnp.where((row_f == i1) | (row_f == i2), 1.0, 0.0)
    tri = jnp.where(lax.broadcasted_iota(jnp.int32, (tm, tm), 0) <= lax.broadcasted_iota(jnp.int32, (tm, tm), 1),
                    1.0, 0.0).astype(MXU_DTYPE)
    incl = _mm(onehot.astype(MXU_DTYPE), tri)
    before = incl - onehot + carry_ref[:, 0:1]
    carry_ref[...] = carry_ref[...] + incl[:, tm - 1:tm]
    rank1 = jnp.sum(jnp.where(row_f == i1, before, 0.0), axis=0, keepdims=True)
    rank2 = jnp.sum(jnp.where(row_f == i2, before, 0.0), axis=0, keepdims=True)

    out_row = lax.broadcasted_iota(jnp.int32, ri_ref.shape, 0)
    ri = jnp.where(out_row == 0, i1 - N_GROUPS,
                   jnp.where(out_row == 1, i2 - N_GROUPS, jnp.where(out_row == 2, rank1, jnp.where(out_row == 3, rank2, 0.0))))
    ri_ref[...] = ri.astype(jnp.int32)
    rw_ref[...] = jnp.where(out_row == 0, w1, jnp.where(out_row == 1, w2, 0.0))
    cnt_ref[...] = carry_ref[...].astype(jnp.int32)


def _router(h, g, w_hi, w_lo, bias):
    n = h.shape[0]
    tm = min(ROW_TILE, n)
    full = lambda a: pl.BlockSpec(a.shape, lambda i: (0,) * a.ndim)
    return pl.pallas_call(
        _router_kernel,
        grid=(n // tm,),
        in_specs=[pl.BlockSpec((tm, D_MODEL), lambda i: (i, 0)), full(g), full(w_hi), full(w_lo), full(bias)],
        out_specs=[pl.BlockSpec((SUBLANES, tm), lambda i: (0, i)), pl.BlockSpec((SUBLANES, tm), lambda i: (0, i)),
                   pl.BlockSpec((ROUTER_ROWS, LANES), lambda i: (i, 0))],
        out_shape=[jax.ShapeDtypeStruct((SUBLANES, n), jnp.int32), jax.ShapeDtypeStruct((SUBLANES, n), F32),
                   jax.ShapeDtypeStruct((n // tm * ROUTER_ROWS, LANES), jnp.int32)],
        scratch_shapes=[pltpu.VMEM((ROUTER_ROWS, LANES), F32)],
        compiler_params=_cparams(1),
        name="router",
    )(h, g, w_hi, w_lo, bias)


def _row_copy(src_ref, src_row, dst_ref, dst_row, sem):
    src = src_ref.at[pl.ds(pl.multiple_of(src_row * ROW_CHUNKS, ROW_CHUNKS), ROW_CHUNKS)]
    dst = dst_ref.at[pl.ds(pl.multiple_of(dst_row * ROW_CHUNKS, ROW_CHUNKS), ROW_CHUNKS)]
    return pltpu.make_async_copy(src, dst, sem)


def _to_token_tiles(ref, x):
    rows = x.shape[0]
    for c in range(ROW_CHUNKS):
        ref[pl.ds(c, rows, stride=ROW_CHUNKS), :] = x[:, c * LANES:(c + 1) * LANES]


def _from_token_tiles(ref):
    rows = ref.shape[0] // ROW_CHUNKS
    return jnp.concatenate([ref[pl.ds(c, rows, stride=ROW_CHUNKS), :] for c in range(ROW_CHUNKS)], axis=1)


def _dispatch_kernel(pos_ref, fill_ref, h_ref, g_ref, xs_ref, xn_scr, zero_scr, sem, fill_sem):
    tm = h_ref.shape[0]

    @pl.when(pl.program_id(0) == 0)
    def _():
        zero_scr[...] = jnp.zeros_like(zero_scr)
        for e in range(fill_ref.shape[0]):
            first = pl.multiple_of(fill_ref[e] * ROW_CHUNKS, ROW_CHUNKS)
            fill = pltpu.make_async_copy(zero_scr, xs_ref.at[pl.ds(first, EXPERT_TILE * ROW_CHUNKS)], fill_sem)
            fill.start()
            fill.wait()

    base = pl.program_id(0) * (2 * tm)
    _to_token_tiles(xn_scr, _rms(h_ref[...], g_ref[...]))

    def issue(r, _):
        _row_copy(xn_scr, r, xs_ref, pos_ref[base + 2 * r], sem).start()
        _row_copy(xn_scr, r, xs_ref, pos_ref[base + 2 * r + 1], sem).start()
        return 0

    lax.fori_loop(0, tm, issue, 0, unroll=DMA_UNROLL)
    for _ in range(2):
        pltpu.make_async_copy(xn_scr, xs_ref.at[pl.ds(0, tm * ROW_CHUNKS)], sem).wait()


def _dispatch(pos, fill_start, h, g, p_rows):
    n = h.shape[0]
    tm = min(ROW_TILE, n)
    grid_spec = pltpu.PrefetchScalarGridSpec(
        num_scalar_prefetch=2,
        grid=(n // tm,),
        in_specs=[pl.BlockSpec((tm, D_MODEL), lambda i, pos, fill: (i, 0)),
                  pl.BlockSpec(g.shape, lambda i, pos, fill: (0, 0))],
        out_specs=pl.BlockSpec(memory_space=pl.ANY),
        scratch_shapes=[pltpu.VMEM((tm * ROW_CHUNKS, LANES), F32),
                        pltpu.VMEM((EXPERT_TILE * ROW_CHUNKS, LANES), F32),
                        pltpu.SemaphoreType.DMA(()), pltpu.SemaphoreType.DMA(())],
    )
    return pl.pallas_call(
        _dispatch_kernel,
        grid_spec=grid_spec,
        out_shape=jax.ShapeDtypeStruct(((p_rows + EXPERT_TILE) * ROW_CHUNKS, LANES), F32),
        compiler_params=_cparams(1),
        name="dispatch",
    )(pos, fill_start, h, g)


def _expert_kernel(te_ref, ta_ref, tx_ref, x_ref, wg_ref, wu_ref, wd_ref, y_ref):
    j = pl.program_id(0)

    @pl.when(ta_ref[j] == 1)
    def _():
        x = _from_token_tiles(x_ref).astype(MXU_DTYPE)
        hid = jax.nn.silu(_mm(x, wg_ref[0])) * _mm(x, wu_ref[0])
        _to_token_tiles(y_ref, _mm(hid.astype(MXU_DTYPE), wd_ref[0]))

    @pl.when(ta_ref[j] == 0)
    def _():
        y_ref[...] = jnp.zeros_like(y_ref)


def _experts(tile_expert, tile_active, tile_x, xs, wg, wu, wd, p_rows):
    grid_spec = pltpu.PrefetchScalarGridSpec(
        num_scalar_prefetch=3,
        grid=(p_rows // EXPERT_TILE,),
        in_specs=[pl.BlockSpec((EXPERT_TILE * ROW_CHUNKS, LANES), lambda j, te, ta, tx: (tx[j], 0)),
                  pl.BlockSpec((1, D_MODEL, D_EXPERT), lambda j, te, ta, tx: (te[j], 0, 0)),
                  pl.BlockSpec((1, D_MODEL, D_EXPERT), lambda j, te, ta, tx: (te[j], 0, 0)),
                  pl.BlockSpec((1, D_EXPERT, D_MODEL), lambda j, te, ta, tx: (te[j], 0, 0))],
        out_specs=pl.BlockSpec((EXPERT_TILE * ROW_CHUNKS, LANES), lambda j, te, ta, tx: (j, 0)),
    )
    return pl.pallas_call(
        _expert_kernel,
        grid_spec=grid_spec,
        out_shape=jax.ShapeDtypeStruct((p_rows * ROW_CHUNKS, LANES), F32),
        compiler_params=_cparams(1),
        name="experts",
    )(tile_expert, tile_active, tile_x, xs, wg, wu, wd)


def _combine_kernel(pos_ref, h_ref, rw_ref, g_ref, ys_ref, o_ref, y1_scr, y2_scr, sem, *, normalize):
    tm = h_ref.shape[0]
    base = pl.program_id(0) * (2 * tm)

    def issue(r, _):
        _row_copy(ys_ref, pos_ref[base + 2 * r], y1_scr, r, sem).start()
        _row_copy(ys_ref, pos_ref[base + 2 * r + 1], y2_scr, r, sem).start()
        return 0

    lax.fori_loop(0, tm, issue, 0, unroll=DMA_UNROLL)
    for buf in (y1_scr, y2_scr):
        pltpu.make_async_copy(ys_ref.at[pl.ds(0, tm * ROW_CHUNKS)], buf, sem).wait()
    rw = rw_ref[...]
    out = h_ref[...] + (rw[:, 0:1] * _from_token_tiles(y1_scr) + rw[:, 1:2] * _from_token_tiles(y2_scr))
    o_ref[...] = _rms(out, g_ref[...]) if normalize else out


def _combine(pos, h, rw, g_final, ys, normalize):
    n = h.shape[0]
    tm = min(ROW_TILE, n)
    grid_spec = pltpu.PrefetchScalarGridSpec(
        num_scalar_prefetch=1,
        grid=(n // tm,),
        in_specs=[pl.BlockSpec((tm, D_MODEL), lambda i, pos: (i, 0)),
                  pl.BlockSpec((tm, LANES), lambda i, pos: (i, 0)),
                  pl.BlockSpec(g_final.shape, lambda i, pos: (0, 0)),
                  pl.BlockSpec(memory_space=pl.ANY)],
        out_specs=pl.BlockSpec((tm, D_MODEL), lambda i, pos: (i, 0)),
        scratch_shapes=[pltpu.VMEM((tm * ROW_CHUNKS, LANES), F32), pltpu.VMEM((tm * ROW_CHUNKS, LANES), F32),
                        pltpu.SemaphoreType.DMA(())],
    )
    return pl.pallas_call(
        functools.partial(_combine_kernel, normalize=normalize),
        grid_spec=grid_spec,
        out_shape=jax.ShapeDtypeStruct((n, D_MODEL), F32),
        compiler_params=_cparams(1),
        name="combine",
    )(pos, h, rw, g_final, ys)


def _moe(h, g, w_group, b_group, w_expert, b_expert, w_gate, w_up, w_down, g_final, last):
    n = h.shape[0]
    w_r = jnp.zeros((ROUTER_ROWS, D_MODEL), F32).at[:N_GROUPS].set(w_group.T)
    w_r = w_r.at[N_GROUPS:N_GROUPS + N_EXPERTS].set(w_expert.T)
    b_r = jnp.zeros((ROUTER_ROWS,), F32).at[:N_GROUPS].set(b_group).at[N_GROUPS:N_GROUPS + N_EXPERTS].set(b_expert)
    b_r = jnp.broadcast_to(b_r[:, None], (ROUTER_ROWS, LANES))
    w_hi = w_r.astype(MXU_DTYPE)
    w_lo = (w_r - w_hi.astype(F32)).astype(MXU_DTYPE)
    ri, rw_t, cnt = _router(h, g, w_hi, w_lo, b_r)
    rw = jnp.pad(rw_t[0:2].T, ((0, 0), (0, LANES - 2)))

    counts = cnt[-ROUTER_ROWS:, 0][N_GROUPS:N_GROUPS + N_EXPERTS]
    padded = (counts + EXPERT_TILE - 1) // EXPERT_TILE * EXPERT_TILE
    ends = jnp.cumsum(padded)
    starts = ends - padded
    experts = jnp.arange(N_EXPERTS, dtype=jnp.int32)
    seg_start = jnp.sum(jnp.where(ri[0:2, :, None] == experts, starts, 0), axis=-1)
    pos = (seg_start + ri[2:4]).T.reshape(-1).astype(jnp.int32)
    p_rows = 2 * n + N_EXPERTS * EXPERT_TILE
    tile_start = jnp.arange(p_rows // EXPERT_TILE, dtype=jnp.int32) * EXPERT_TILE
    tile_expert = jnp.minimum(jnp.sum(tile_start[:, None] >= ends[None, :], axis=1), N_EXPERTS - 1).astype(jnp.int32)
    tile_active = (tile_start < ends[-1]).astype(jnp.int32)
    tile_x = jnp.minimum(tile_start, ends[-1] - EXPERT_TILE) // EXPERT_TILE

    tail = jnp.minimum(ends[-1] + jnp.arange(N_EXPERTS + 1, dtype=jnp.int32) * EXPERT_TILE, p_rows)
    xs = _dispatch(pos, jnp.concatenate([starts + counts, tail]).astype(jnp.int32), h, g, p_rows)
    ys = _experts(tile_expert, tile_active, tile_x.astype(jnp.int32), xs, w_gate.astype(MXU_DTYPE),
                  w_up.astype(MXU_DTYPE), w_down.astype(MXU_DTYPE), p_rows)
    return _combine(pos, h, rw, g_final, ys, normalize=last)


def _even_weights(w_in):
    gate_cols = w_in[:, C_GATE:C_GATE + 3 * NSA_HEADS].reshape(D_MODEL, NSA_KV_HEADS, HEADS_PER_GROUP, 3)
    gate_cols = jnp.pad(gate_cols, ((0, 0), (0, 0), (0, 0), (0, SUBLANES - 3)))
    gate_cols = gate_cols.reshape(D_MODEL, NSA_KV_HEADS, HEADS_PER_GROUP * SUBLANES)
    gate_cols = jnp.pad(gate_cols, ((0, 0), (0, 0), (0, LANES - HEADS_PER_GROUP * SUBLANES)))
    return jnp.concatenate([w_in[:, :C_GATE], gate_cols.reshape(D_MODEL, NSA_KV_HEADS * LANES)], axis=1)


def _block_diag_halves(w):
    blocks = w.shape[0] // 2
    out = jnp.zeros((2, blocks * w.shape[1], blocks * w.shape[2]), w.dtype)
    for j in range(2):
        for k in range(blocks):
            out = out.at[j, k * w.shape[1]:(k + 1) * w.shape[1], k * w.shape[2]:(k + 1) * w.shape[2]].set(
                w[j * blocks + k])
    return out


def _even_mixer(h, g, batch, cos_t, sin_t, w_in, w_out, conv_w, conv_b, w_r, b_r, w_i, b_i, lam,
                pos_k, w1_k, w2_k, pos_v, w1_v, w2_v):
    n = h.shape[0]
    t = n // batch
    xl, gg, qn, qr, kcv, ksr, vs, kwr, vw, gates = _even_proj(
        h, g, _even_weights(w_in).astype(MXU_DTYPE), cos_t, sin_t)

    y_lru = _lru(xl, gg, conv_w, conv_b[None, :], _block_diag_halves(w_r).astype(MXU_DTYPE),
                 _block_diag_halves(w_i).astype(MXU_DTYPE), b_r[None, :], b_i[None, :],
                 jax.nn.softplus(-lam)[None, :], batch)

    nchunk = t // CMP_STRIDE
    x16 = kcv.reshape(batch, nchunk, CMP_STRIDE, 2 * NSA_KV_HEADS, HEAD_DIM).transpose(0, 3, 1, 2, 4)
    x16 = x16.reshape(batch, 2 * NSA_KV_HEADS, nchunk, CMP_STRIDE * HEAD_DIM)
    half = CMP_STRIDE * HEAD_DIM
    pos = jnp.stack([pos_k.reshape(2, 1, half), pos_v.reshape(2, 1, half)])
    w1 = jnp.stack([w1_k.reshape(2, half, CMP_HIDDEN), w1_v.reshape(2, half, CMP_HIDDEN)]).astype(MXU_DTYPE)
    w2 = jnp.stack([w2_k, w2_v]).astype(MXU_DTYPE)
    kvc = _compress(x16, pos, w1, w2)

    vct = kvc[:, NSA_KV_HEADS:].transpose(0, 1, 3, 2)
    y_nsa = _nsa(qn, qr, kvc, vct, ksr, vs, kwr, vw, gates, batch)
    w_out = w_out.astype(MXU_DTYPE)
    return _out_proj(h, y_lru, y_nsa, w_out[:LRU_WIDTH], w_out[LRU_WIDTH:])


def _rope_tables(positions):
    inv = ROPE_THETA ** (-jnp.arange(0, 2 * ROT_HALF, 2, dtype=F32) / (2 * ROT_HALF))
    ang = positions.reshape(-1).astype(F32)[:, None] * inv
    cos, sin = jnp.cos(ang), jnp.sin(ang)
    rest = HEAD_DIM - 2 * ROT_HALF
    cos_h = jnp.concatenate([cos, cos, jnp.ones((cos.shape[0], rest), F32)], axis=1)
    sin_h = jnp.concatenate([-sin, sin, jnp.zeros((cos.shape[0], rest), F32)], axis=1)
    reps = LANES // HEAD_DIM
    return jnp.tile(cos_h, (1, reps)), jnp.tile(sin_h, (1, reps))


def kernel(x, mem, positions, norm_mix, norm_xattn, norm_mem, norm_ffn, norm_final, even_w_in, even_w_out, lru_conv_w, lru_conv_b, lru_w_r, lru_b_r, lru_w_i, lru_b_i, lru_lambda, nsa_cmp_pos_k, nsa_cmp_w1_k, nsa_cmp_w2_k, nsa_cmp_pos_v, nsa_cmp_w1_v, nsa_cmp_w2_v, odd_w_in, odd_conv_w, odd_w_out, xa_wq, xa_wk, xa_wv, xa_wo, moe_w_group, moe_b_group, moe_w_expert, moe_b_expert, moe_w_gate, moe_w_up, moe_w_down):
    batch, t, d = x.shape
    n = batch * t
    depth = norm_mix.shape[0]
    cos_t, sin_t = _rope_tables(positions)
    h = x.reshape(n, d)
    mem2 = mem.reshape(-1, d)
    for layer in range(depth):
        g_mix = norm_mix[layer][None, :]
        if layer % 2 == 0:
            e = layer // 2
            h = _even_mixer(h, g_mix, batch, cos_t, sin_t, even_w_in[e], even_w_out[e], lru_conv_w[e],
                            lru_conv_b[e], lru_w_r[e], lru_b_r[e], lru_w_i[e], lru_b_i[e], lru_lambda[e],
                            nsa_cmp_pos_k[e], nsa_cmp_w1_k[e], nsa_cmp_w2_k[e],
                            nsa_cmp_pos_v[e], nsa_cmp_w1_v[e], nsa_cmp_w2_v[e])
        else:
            o = layer // 2
            h = _short_conv(h, g_mix, odd_w_in[o].astype(MXU_DTYPE), odd_conv_w[o],
                            odd_w_out[o].astype(MXU_DTYPE), batch)
        wkv = jnp.concatenate([xa_wk[layer], xa_wv[layer]], axis=1).astype(MXU_DTYPE)
        kv = _mem_kv(mem2, norm_mem[layer][None, :], wkv)
        h = _xattn(h, norm_xattn[layer][None, :], xa_wq[layer].astype(MXU_DTYPE), kv,
                   xa_wo[layer].astype(MXU_DTYPE), batch)
        h = _moe(h, norm_ffn[layer][None, :], moe_w_group[layer], moe_b_group[layer], moe_w_expert[layer],
                 moe_b_expert[layer], moe_w_gate[layer], moe_w_up[layer], moe_w_down[layer],
                 norm_final[None, :], last=layer == depth - 1)
    return h.reshape(batch, t, d)
```

```python
import functools

import jax
import jax.numpy as jnp
import numpy as np
from jax import lax
from jax.experimental import pallas as pl
from jax.experimental.pallas import tpu as pltpu

F32 = jnp.float32
MXU_DTYPE = jnp.bfloat16

D_MODEL = 1024
LRU_WIDTH = 512
LRU_CONV = 4
LRU_C = 8.0
NSA_HEADS = 8
NSA_KV_HEADS = 2
HEADS_PER_GROUP = NSA_HEADS // NSA_KV_HEADS
HEAD_DIM = 64
CMP_STRIDE = 16
CMP_LEN = 32
CMP_HIDDEN = 128
SEL_BLOCK = 64
SEL_TOP = 16
WINDOW = 512
ROT_HALF = 8
ROPE_THETA = 500000.0
SC_CONV = 3
XA_HEADS = 4
XA_HEAD_DIM = 256
N_GROUPS = 4
EXPERTS_PER_GROUP = 4
N_EXPERTS = 16
D_EXPERT = 512
EPS = 1e-6
NEG = -1e30
LOG2E = 1.4426950408889634
FORCE = 1e9

LANES = 128
SUBLANES = 8
VMEM_LIMIT = 56 * 1024 * 1024

C_XL, C_GL, C_Q, C_KCV, C_KS, C_VS, C_KW, C_VW, C_GATE, C_END = (
    0, 512, 1024, 1536, 1792, 1920, 2048, 2176, 2304, 2560)

ROW_TILE = 512
NSA_TQ = 512
NSA_TK = 512
EXPERT_TILE = 256
DMA_UNROLL = 8
ROW_CHUNKS = D_MODEL // LANES


def _cparams(n_axes):
    return pltpu.CompilerParams(dimension_semantics=("arbitrary",) * n_axes,
                                vmem_limit_bytes=VMEM_LIMIT)


def _mm(a, b):
    return jnp.dot(a, b, preferred_element_type=F32)


def _mm_nt(a, b):
    return lax.dot_general(a, b, (((1,), (1,)), ((), ())), preferred_element_type=F32)


def _rms(x, g):
    return x * lax.rsqrt(jnp.mean(x * x, axis=-1, keepdims=True) + EPS) * g


def _shift_rows(xx, s, rows):
    if s == 0:
        return xx[SUBLANES:SUBLANES + rows]
    return pltpu.roll(xx, s, 0)[SUBLANES:SUBLANES + rows]


def _rope(x, cos_t, sin_t):
    width = x.shape[-1]
    lane = lax.broadcasted_iota(jnp.int32, x.shape, 1) % HEAD_DIM
    partner = jnp.where(lane < ROT_HALF, pltpu.roll(x, width - ROT_HALF, 1), pltpu.roll(x, ROT_HALF, 1))
    return x * cos_t + partner * sin_t


def _even_proj_kernel(h_ref, g_ref, w_ref, cos_ref, sin_ref,
                      xl_ref, gg_ref, qn_ref, qr_ref, kcv_ref, ksr_ref, vs_ref, kwr_ref, vw_ref,
                      gate_ref):
    xn = _rms(h_ref[...], g_ref[...]).astype(MXU_DTYPE)

    def proj(c0, c1):
        return _mm(xn, w_ref[:, c0:c1])

    cos_t = cos_ref[...]
    sin_t = sin_ref[...]
    xl_ref[...] = proj(C_XL, C_GL)
    gg_ref[...] = jax.nn.gelu(proj(C_GL, C_Q))
    q = proj(C_Q, C_KCV) * (HEAD_DIM ** -0.5)
    reps = (C_KCV - C_Q) // LANES
    cos_q = jnp.concatenate([cos_t] * reps, axis=1)
    sin_q = jnp.concatenate([sin_t] * reps, axis=1)
    _store_query_tiles(qn_ref, q, HEAD_DIM)
    _store_query_tiles(qr_ref, _rope(q, cos_q, sin_q) * LOG2E, HEAD_DIM)
    kcv_ref[...] = proj(C_KCV, C_KS)
    ksr_ref[...] = _rope(proj(C_KS, C_VS), cos_t, sin_t).astype(ksr_ref.dtype)
    vs_ref[...] = proj(C_VS, C_KW).T.astype(vs_ref.dtype)
    kwr_ref[...] = _rope(proj(C_KW, C_VW), cos_t, sin_t).astype(kwr_ref.dtype)
    vw_ref[...] = proj(C_VW, C_GATE).T.astype(vw_ref.dtype)
    _store_query_tiles(gate_ref, jax.nn.sigmoid(proj(C_GATE, C_END)), SUBLANES)


def _store_query_tiles(ref, x, per_head):
    group_width = x.shape[1] // NSA_KV_HEADS
    for grp in range(NSA_KV_HEADS):
        for j in range(x.shape[0] // NSA_TQ):
            blk = x[j * NSA_TQ:(j + 1) * NSA_TQ, grp * group_width:(grp + 1) * group_width].T
            ref[grp, j] = jnp.concatenate([blk[hh * per_head:(hh + 1) * per_head, :]
                                           for hh in range(HEADS_PER_GROUP)], axis=1).astype(ref.dtype)


def _even_proj(h, g, w, cos_t, sin_t):
    n = h.shape[0]
    tm = min(ROW_TILE, n)
    rows = HEADS_PER_GROUP * NSA_TQ
    row = lambda c: pl.BlockSpec((tm, c), lambda i: (i, 0))
    col = pl.BlockSpec((LANES, tm), lambda i: (0, i))
    full = lambda a: pl.BlockSpec(a.shape, lambda i: (0,) * a.ndim)
    tiles = lambda c: pl.BlockSpec((NSA_KV_HEADS, tm // NSA_TQ, c, rows), lambda i: (0, i, 0, 0))
    tiles_shape = lambda c, dt: jax.ShapeDtypeStruct((NSA_KV_HEADS, n // NSA_TQ, c, rows), dt)
    flat = lambda c, dt: jax.ShapeDtypeStruct((n, c), dt)
    return pl.pallas_call(
        _even_proj_kernel,
        grid=(n // tm,),
        in_specs=[row(D_MODEL), full(g), full(w), row(LANES), row(LANES)],
        out_specs=[row(512), row(512), tiles(HEAD_DIM), tiles(HEAD_DIM), row(256),
                   row(LANES), col, row(LANES), col, tiles(SUBLANES)],
        out_shape=[flat(512, F32), flat(512, F32), tiles_shape(HEAD_DIM, MXU_DTYPE), tiles_shape(HEAD_DIM, MXU_DTYPE),
                   flat(256, F32), flat(LANES, MXU_DTYPE), jax.ShapeDtypeStruct((LANES, n), MXU_DTYPE),
                   flat(LANES, MXU_DTYPE), jax.ShapeDtypeStruct((LANES, n), MXU_DTYPE),
                   tiles_shape(SUBLANES, F32)],
        compiler_params=_cparams(1),
        name="even_proj",
    )(h, g, w, cos_t, sin_t)


def _lru_kernel(xl_ref, gg_ref, cw_ref, cb_ref, wr_ref, wi_ref, br_ref, bi_ref, sp_ref,
                y_ref, tail_ref, h_ref, a_scr, u_scr):
    tt = xl_ref.shape[0]

    @pl.when(pl.program_id(1) == 0)
    def _():
        tail_ref[...] = jnp.zeros_like(tail_ref)
        h_ref[...] = jnp.zeros_like(h_ref)

    x = xl_ref[...]
    xx = jnp.concatenate([tail_ref[...], x], axis=0)
    tail_ref[...] = x[tt - SUBLANES:tt]
    xc = cb_ref[...] + sum(cw_ref[k:k + 1, :] * _shift_rows(xx, LRU_CONV - 1 - k, tt)
                           for k in range(LRU_CONV))
    xcb = xc.astype(MXU_DTYPE)
    half = LRU_WIDTH // 2
    r_lin = jnp.concatenate([_mm(xcb[:, j * half:(j + 1) * half], wr_ref[j]) for j in range(2)], axis=1)
    i_lin = jnp.concatenate([_mm(xcb[:, j * half:(j + 1) * half], wi_ref[j]) for j in range(2)], axis=1)
    r = jax.nn.sigmoid(r_lin + br_ref[...])
    i = jax.nn.sigmoid(i_lin + bi_ref[...])
    log_a = -LRU_C * r * sp_ref[...]
    a = jnp.exp(log_a)
    u = jnp.sqrt(jnp.tanh(-log_a) * (1.0 + a * a)) * (i * xc)

    r8 = lax.broadcasted_iota(jnp.int32, a.shape, 0) % SUBLANES
    for s in (1, 2, 4):
        keep = r8 >= s
        u = jnp.where(keep, a * pltpu.roll(u, s, 0) + u, u)
        a = jnp.where(keep, a * pltpu.roll(a, s, 0), a)
    a_scr[...] = a
    u_scr[...] = u

    def body(gidx, h):
        r0 = pl.multiple_of(gidx * SUBLANES, SUBLANES)
        out = a_scr[pl.ds(r0, SUBLANES), :] * h + u_scr[pl.ds(r0, SUBLANES), :]
        u_scr[pl.ds(r0, SUBLANES), :] = out
        return out[SUBLANES - 1:SUBLANES, :]

    h_ref[...] = lax.fori_loop(0, tt // SUBLANES, body, h_ref[...])
    y_ref[...] = (u_scr[...] * gg_ref[...]).astype(y_ref.dtype)


def _lru(xl, gg, cw, cb, wr, wi, br, bi, sp, batch):
    n = xl.shape[0]
    t = n // batch
    tt = min(ROW_TILE, t)
    nt = t // tt
    row = pl.BlockSpec((tt, LRU_WIDTH), lambda b, i: (b * nt + i, 0))
    full = lambda a: pl.BlockSpec(a.shape, lambda b, i: (0,) * a.ndim)
    return pl.pallas_call(
        _lru_kernel,
        grid=(batch, nt),
        in_specs=[row, row] + [full(a) for a in (cw, cb, wr, wi, br, bi, sp)],
        out_specs=row,
        out_shape=jax.ShapeDtypeStruct((n, LRU_WIDTH), MXU_DTYPE),
        scratch_shapes=[pltpu.VMEM((SUBLANES, LRU_WIDTH), F32), pltpu.VMEM((1, LRU_WIDTH), F32),
                        pltpu.VMEM((tt, LRU_WIDTH), F32), pltpu.VMEM((tt, LRU_WIDTH), F32)],
        compiler_params=_cparams(2),
        name="lru",
    )(xl, gg, cw, cb, wr, wi, br, bi, sp)


def _compress_kernel(x_ref, pos_ref, w1_ref, w2_ref, o_ref):
    x = x_ref[0, 0]
    nchunk = x.shape[0]
    lo = (x + pos_ref[0, 0]).astype(MXU_DTYPE)
    hi = (x + pos_ref[0, 1]).astype(MXU_DTYPE)
    p_lo = _mm(lo, w1_ref[0, 0])
    p_hi = _mm(hi, w1_ref[0, 1])
    a = p_lo + pltpu.roll(p_hi, nchunk - 1, 0)
    o_ref[0, 0] = _mm(jax.nn.gelu(a).astype(MXU_DTYPE), w2_ref[0]).astype(o_ref.dtype)


def _compress(x16, pos, w1, w2):
    b, _, nchunk, width = x16.shape
    return pl.pallas_call(
        _compress_kernel,
        grid=(b, 2 * NSA_KV_HEADS),
        in_specs=[pl.BlockSpec((1, 1, nchunk, width), lambda i, j: (i, j, 0, 0)),
                  pl.BlockSpec((1, 2, 1, width), lambda i, j: (j // NSA_KV_HEADS, 0, 0, 0)),
                  pl.BlockSpec((1, 2, width, CMP_HIDDEN), lambda i, j: (j // NSA_KV_HEADS, 0, 0, 0)),
                  pl.BlockSpec((1, CMP_HIDDEN, HEAD_DIM), lambda i, j: (j // NSA_KV_HEADS, 0, 0))],
        out_specs=pl.BlockSpec((1, 1, nchunk, HEAD_DIM), lambda i, j: (i, j, 0, 0)),
        out_shape=jax.ShapeDtypeStruct((b, 2 * NSA_KV_HEADS, nchunk, HEAD_DIM), MXU_DTYPE),
        compiler_params=_cparams(2),
        name="compress",
    )(x16, pos, w1, w2)


def _nsa_kernel(qn_ref, qr_ref, kc_ref, vct_ref, ks_ref, vst_ref, kw_ref, vwt_ref, gate_ref,
                y_ref, chosen_scr, acc_scr, accw_scr, sc_scr):
    rows = qn_ref.shape[3]
    tq = rows // HEADS_PER_GROUP
    t = ks_ref.shape[0]
    nc = kc_ref.shape[2]
    nsel = t // SEL_BLOCK
    n_top = min(SEL_TOP, nsel)
    tk = NSA_TK
    t0 = pl.program_id(2) * tq
    qn = qn_ref[0, 0]
    grp = pl.program_id(1)
    qr64 = qr_ref[0, 0]
    qr = jnp.concatenate([jnp.where(grp == gi, qr64, jnp.zeros_like(qr64)) for gi in range(NSA_KV_HEADS)], axis=0)

    s = _mm(kc_ref[0, 0], qn)
    tq_row = t0 + lax.broadcasted_iota(jnp.int32, (nc, rows), 1) % tq
    cmp_end = lax.broadcasted_iota(jnp.int32, (nc, rows), 0) * CMP_STRIDE + (CMP_LEN - 1)
    valid = cmp_end <= tq_row
    s = jnp.where(valid, s, NEG)
    e = jnp.where(valid, jnp.exp(s - jnp.max(s, axis=0, keepdims=True)), 0.0)
    den = jnp.sum(e, axis=0, keepdims=True)
    p = e * (1.0 / jnp.where(den > 0.0, den, 1.0))
    o_cmp = _mm(vct_ref[0, 0], p.astype(MXU_DTYPE))

    psum = p[:, 0:tq]
    for hh in range(1, HEADS_PER_GROUP):
        psum = psum + p[:, hh * tq:(hh + 1) * tq]
    p_hi = psum.astype(MXU_DTYPE)
    p_lo = (psum - p_hi.astype(F32)).astype(MXU_DTYPE)
    cj = lax.broadcasted_iota(jnp.int32, (nsel, nc), 0)
    cn = lax.broadcasted_iota(jnp.int32, (nsel, nc), 1)
    ratio = SEL_BLOCK // CMP_STRIDE
    cover = jnp.where((cn >= ratio * cj - (CMP_LEN // CMP_STRIDE - 1)) & (cn <= ratio * cj + ratio - 1)
                      & (cn < nc - 1), 1.0, 0.0).astype(MXU_DTYPE)
    imp = _mm(cover, p_hi) + _mm(cover, p_lo)

    blk = lax.broadcasted_iota(jnp.int32, (nsel, tq), 0)
    tq_col = t0 + lax.broadcasted_iota(jnp.int32, (nsel, tq), 1)
    cur = tq_col // SEL_BLOCK
    forced = (blk == 0) | (blk == cur) | (blk == cur - 1)
    causal = blk * SEL_BLOCK <= tq_col
    score0 = jnp.where(causal, jnp.where(forced, FORCE, imp), NEG)
    blk_f = blk.astype(F32)

    def pick_one(_, carry):
        score, chosen = carry
        best = jnp.max(score, axis=0, keepdims=True)
        first = jnp.min(jnp.where(score == best, blk_f, float(nsel)), axis=0, keepdims=True)
        hit = blk_f == first
        return jnp.where(hit, -jnp.inf, score), jnp.where(hit, 1.0, chosen)

    _, chosen = lax.fori_loop(0, n_top, pick_one, (score0, jnp.zeros((nsel, tq), F32)))
    chosen_scr[...] = chosen

    def scores(k_ref, kt, bias):
        k0 = pl.multiple_of(kt * tk, tk)
        return _mm(k_ref[pl.ds(k0, tk), :], qr) + jnp.concatenate([bias] * HEADS_PER_GROUP, axis=1)

    def soft(sc, m_old, l_old):
        m_new = jnp.maximum(m_old, jnp.max(sc, axis=0, keepdims=True))
        alpha = jnp.exp2(m_old - m_new)
        pe = jnp.exp2(sc - m_new)
        return m_new, alpha * l_old + jnp.sum(pe, axis=0, keepdims=True), alpha, pe.astype(MXU_DTYPE)

    def accumulate(vt_ref, acc_ref, kt, alpha, pe):
        k0 = pl.multiple_of(kt * tk, tk)
        acc_ref[...] = alpha * acc_ref[...] + _mm(vt_ref[:, pl.ds(k0, tk)], pe)

    expand = jnp.where(lax.broadcasted_iota(jnp.int32, (tk, SUBLANES), 0) // SEL_BLOCK
                       == lax.broadcasted_iota(jnp.int32, (tk, SUBLANES), 1), 1.0, 0.0)

    def key_minus_query(kt):
        return (kt * tk - t0 + lax.broadcasted_iota(jnp.int32, (tk, tq), 0)
                - lax.broadcasted_iota(jnp.int32, (tk, tq), 1))

    def sel_bias(kt, causal=True):
        grp = pl.multiple_of(kt * SUBLANES, SUBLANES)
        bias = _mm(expand, (chosen_scr[pl.ds(grp, SUBLANES), :] - 1.0) * (-NEG))
        return jnp.where(key_minus_query(kt) <= 0, bias, NEG) if causal else bias

    def win_bias(kt, diagonal):
        d = key_minus_query(kt)
        inside = (d > -WINDOW) & (d <= 0) if diagonal else d > -WINDOW
        return jnp.where(inside, 0.0, NEG)

    def sel_only(kt, carry):
        m_s, l_s, m_w, l_w = carry
        sc_next = scores(ks_ref, kt + 1, sel_bias(kt + 1, causal=False))
        m_s, l_s, alpha, pe = soft(sc_scr[...], m_s, l_s)
        accumulate(vst_ref, acc_scr, kt, alpha, pe)
        sc_scr[...] = sc_next
        return m_s, l_s, m_w, l_w

    def both(kt, carry, last=False):
        m_s, l_s, m_w, l_w = carry
        sc_w = scores(kw_ref, kt, win_bias(kt, last))
        if not last:
            sc_next = scores(ks_ref, kt + 1, sel_bias(kt + 1))
        m_s, l_s, alpha_s, pe_s = soft(sc_scr[...], m_s, l_s)
        m_w, l_w, alpha_w, pe_w = soft(sc_w, m_w, l_w)
        accumulate(vst_ref, acc_scr, kt, alpha_s, pe_s)
        accumulate(vwt_ref, accw_scr, kt, alpha_w, pe_w)
        if not last:
            sc_scr[...] = sc_next
        return m_s, l_s, m_w, l_w

    kt_last = (t0 + tq - 1) // tk
    win_lo = jnp.maximum(t0 - (WINDOW - 1), 0) // tk
    acc_scr[...] = jnp.zeros_like(acc_scr)
    accw_scr[...] = jnp.zeros_like(accw_scr)
    lowest = jnp.full((1, rows), NEG, F32)
    zero = jnp.zeros((1, rows), F32)
    sc_scr[...] = scores(ks_ref, 0, sel_bias(0))
    carry = lax.fori_loop(0, win_lo, sel_only, (lowest, zero, lowest, zero))
    carry = lax.fori_loop(win_lo, kt_last, both, carry)
    _, l_s, _, l_w = both(kt_last, carry, last=True)
    o_sel = acc_scr[...] * (1.0 / l_s)
    o_win = accw_scr[...] * (1.0 / l_w)

    gate = gate_ref[0, 0]
    y = gate[0:1] * o_cmp + gate[1:2] * o_sel + gate[2:3] * o_win
    y = jnp.concatenate([y[:, hh * tq:(hh + 1) * tq] for hh in range(HEADS_PER_GROUP)], axis=0)
    y_ref[...] = y.T.astype(y_ref.dtype)


def _nsa(qn_t, qr_t, kvc, vct, ks, vst, kw, vwt, gates_t, batch):
    _, n_tiles, _, rows = qn_t.shape
    tq = rows // HEADS_PER_GROUP
    nq = n_tiles // batch
    t = nq * tq
    nc = kvc.shape[2]
    assert NSA_TK == SEL_BLOCK * SUBLANES and t % NSA_TK == 0
    tile = lambda c: pl.BlockSpec((1, 1, c, rows), lambda i, g, j: (g, i * nq + j, 0, 0))
    k_spec = pl.BlockSpec((t, LANES), lambda i, g, j: (i, 0))
    vt_spec = pl.BlockSpec((HEAD_DIM, t), lambda i, g, j: (g, i))
    return pl.pallas_call(
        _nsa_kernel,
        grid=(batch, NSA_KV_HEADS, nq),
        in_specs=[tile(HEAD_DIM), tile(HEAD_DIM),
                  pl.BlockSpec((1, 1, nc, HEAD_DIM), lambda i, g, j: (i, g, 0, 0)),
                  pl.BlockSpec((1, 1, HEAD_DIM, nc), lambda i, g, j: (i, g, 0, 0)),
                  k_spec, vt_spec, k_spec, vt_spec, tile(SUBLANES)],
        out_specs=pl.BlockSpec((tq, HEADS_PER_GROUP * HEAD_DIM), lambda i, g, j: (i * nq + j, g)),
        out_shape=jax.ShapeDtypeStruct((batch * t, NSA_HEADS * HEAD_DIM), MXU_DTYPE),
        scratch_shapes=[pltpu.VMEM((t // SEL_BLOCK, tq), F32),
                        pltpu.VMEM((HEAD_DIM, rows), F32), pltpu.VMEM((HEAD_DIM, rows), F32),
                        pltpu.VMEM((NSA_TK, rows), F32)],
        compiler_params=_cparams(3),
        name="nsa",
    )(qn_t, qr_t, kvc, vct, ks, vst, kw, vwt, gates_t)


def _out_proj_kernel(h_ref, a_ref, b_ref, wa_ref, wb_ref, o_ref):
    o_ref[...] = h_ref[...] + _mm(a_ref[...], wa_ref[...]) + _mm(b_ref[...], wb_ref[...])


def _out_proj(h, a, b, wa, wb):
    n = h.shape[0]
    tm = min(ROW_TILE, n)
    full = lambda x: pl.BlockSpec(x.shape, lambda i: (0,) * x.ndim)
    return pl.pallas_call(
        _out_proj_kernel,
        grid=(n // tm,),
        in_specs=[pl.BlockSpec((tm, D_MODEL), lambda i: (i, 0)),
                  pl.BlockSpec((tm, a.shape[1]), lambda i: (i, 0)),
                  pl.BlockSpec((tm, b.shape[1]), lambda i: (i, 0)), full(wa), full(wb)],
        out_specs=pl.BlockSpec((tm, D_MODEL), lambda i: (i, 0)),
        out_shape=jax.ShapeDtypeStruct((n, D_MODEL), F32),
        compiler_params=_cparams(1),
        name="out_proj",
    )(h, a, b, wa, wb)


def _short_conv_kernel(h_ref, g_ref, win_ref, cw_ref, wout_ref, o_ref, tail_ref):
    tt = h_ref.shape[0]

    @pl.when(pl.program_id(1) == 0)
    def _():
        tail_ref[...] = jnp.zeros_like(tail_ref)

    h = h_ref[...]
    xn = _rms(h, g_ref[...]).astype(MXU_DTYPE)
    b_g = _mm(xn, win_ref[:, 0:D_MODEL])
    cv = _mm(xn, win_ref[:, D_MODEL:2 * D_MODEL]) * _mm(xn, win_ref[:, 2 * D_MODEL:3 * D_MODEL])
    xx = jnp.concatenate([tail_ref[...], cv], axis=0)
    tail_ref[...] = cv[tt - SUBLANES:tt]
    conv = sum(cw_ref[k:k + 1, :] * _shift_rows(xx, SC_CONV - 1 - k, tt) for k in range(SC_CONV))
    o_ref[...] = h + _mm((b_g * conv).astype(MXU_DTYPE), wout_ref[...])


def _short_conv(h, g, w_in, cw, w_out, batch):
    n = h.shape[0]
    t = n // batch
    tt = min(ROW_TILE, t)
    nt = t // tt
    row = pl.BlockSpec((tt, D_MODEL), lambda b, i: (b * nt + i, 0))
    full = lambda a: pl.BlockSpec(a.shape, lambda b, i: (0,) * a.ndim)
    return pl.pallas_call(
        _short_conv_kernel,
        grid=(batch, nt),
        in_specs=[row, full(g), full(w_in), full(cw), full(w_out)],
        out_specs=row,
        out_shape=jax.ShapeDtypeStruct((n, D_MODEL), F32),
        scratch_shapes=[pltpu.VMEM((SUBLANES, D_MODEL), F32)],
        compiler_params=_cparams(2),
        name="short_conv",
    )(h, g, w_in, cw, w_out)


def _mem_kv_kernel(m_ref, g_ref, w_ref, o_ref):
    o_ref[...] = _mm(_rms(m_ref[...], g_ref[...]).astype(MXU_DTYPE), w_ref[...]).astype(o_ref.dtype)


def _mem_kv(mem, g, wkv):
    n = mem.shape[0]
    tm = min(ROW_TILE, n)
    tn = 1024
    return pl.pallas_call(
        _mem_kv_kernel,
        grid=(n // tm, wkv.shape[1] // tn),
        in_specs=[pl.BlockSpec((tm, D_MODEL), lambda i, j: (i, 0)),
                  pl.BlockSpec(g.shape, lambda i, j: (0, 0)),
                  pl.BlockSpec((D_MODEL, tn), lambda i, j: (0, j))],
        out_specs=pl.BlockSpec((tm, tn), lambda i, j: (i, j)),
        out_shape=jax.ShapeDtypeStruct((n, wkv.shape[1]), MXU_DTYPE),
        compiler_params=_cparams(2),
        name="mem_kv",
    )(mem, g, wkv)


def _xattn_kernel(h_ref, g_ref, wq_ref, kv_ref, wo_ref, o_ref):
    h = h_ref[...]
    xn = _rms(h, g_ref[...]).astype(MXU_DTYPE)
    q = (_mm(xn, wq_ref[...]) * (XA_HEAD_DIM ** -0.5)).astype(MXU_DTYPE)
    width = XA_HEADS * XA_HEAD_DIM
    outs = []
    for hd in range(XA_HEADS):
        sl = slice(hd * XA_HEAD_DIM, (hd + 1) * XA_HEAD_DIM)
        s = _mm_nt(q[:, sl], kv_ref[:, sl])
        e = jnp.exp(s - jnp.max(s, axis=-1, keepdims=True))
        p = e / jnp.sum(e, axis=-1, keepdims=True)
        outs.append(_mm(p.astype(MXU_DTYPE), kv_ref[:, width + hd * XA_HEAD_DIM:width + (hd + 1) * XA_HEAD_DIM]))
    o = jnp.concatenate(outs, axis=1).astype(MXU_DTYPE)
    o_ref[...] = h + _mm(o, wo_ref[...])


def _xattn(h, g, wq, kv, wo, batch):
    n = h.shape[0]
    t = n // batch
    tm = min(ROW_TILE, t)
    nt = t // tm
    mlen = kv.shape[0] // batch
    full = lambda a: pl.BlockSpec(a.shape, lambda b, i: (0,) * a.ndim)
    row = pl.BlockSpec((tm, D_MODEL), lambda b, i: (b * nt + i, 0))
    return pl.pallas_call(
        _xattn_kernel,
        grid=(batch, nt),
        in_specs=[row, full(g), full(wq), pl.BlockSpec((mlen, kv.shape[1]), lambda b, i: (b, 0)), full(wo)],
        out_specs=row,
        out_shape=jax.ShapeDtypeStruct((n, D_MODEL), F32),
        compiler_params=_cparams(2),
        name="xattn",
    )(h, g, wq, kv, wo)


ROUTER_ROWS = 32


def _router_kernel(h_ref, g_ref, whi_ref, wlo_ref, b_ref, ri_ref, rw_ref, cnt_ref, carry_ref):
    tm = h_ref.shape[0]

    @pl.when(pl.program_id(0) == 0)
    def _():
        carry_ref[...] = jnp.zeros_like(carry_ref)

    xn = _rms(h_ref[...], g_ref[...])
    x_hi = xn.astype(MXU_DTYPE)
    x_lo = (xn - x_hi.astype(F32)).astype(MXU_DTYPE)
    logits = (_mm_nt(whi_ref[...], x_hi) + (_mm_nt(whi_ref[...], x_lo) + _mm_nt(wlo_ref[...], x_hi))
              + b_ref[:, 0:1])

    row = lax.broadcasted_iota(jnp.int32, logits.shape, 0)
    row_f = row.astype(F32)
    none = float(ROUTER_ROWS)
    is_g = row < N_GROUPS
    g_max = jnp.max(jnp.where(is_g, logits, -jnp.inf), axis=0, keepdims=True)
    g_sum = jnp.sum(jnp.where(is_g, jnp.exp(logits - g_max), 0.0), axis=0, keepdims=True)
    g_top = 1.0 / g_sum
    g_idx = jnp.min(jnp.where(is_g & (logits == g_max), row_f, none), axis=0, keepdims=True)
    first = N_GROUPS + EXPERTS_PER_GROUP * g_idx
    in_g = (row_f >= first) & (row_f < first + EXPERTS_PER_GROUP)
    e1 = jnp.max(jnp.where(in_g, logits, -jnp.inf), axis=0, keepdims=True)
    i1 = jnp.min(jnp.where(in_g & (logits == e1), row_f, none), axis=0, keepdims=True)
    rest = in_g & (row_f != i1)
    e2 = jnp.max(jnp.where(rest, logits, -jnp.inf), axis=0, keepdims=True)
    i2 = jnp.min(jnp.where(rest & (logits == e2), row_f, none), axis=0, keepdims=True)
    ratio = jnp.exp(e2 - e1)
    w1 = g_top / (1.0 + ratio)
    w2 = g_top * ratio / (1.0 + ratio)

    onehot = jnp.where((row_f == i1) | (row_f == i2), 1.0, 0.0)
    tri = jnp.where(lax.broadcasted_iota(jnp.int32, (tm, tm), 0) <= lax.broadcasted_iota(jnp.int32, (tm, tm), 1),
                    1.0, 0.0).astype(MXU_DTYPE)
    incl = _mm(onehot.astype(MXU_DTYPE), tri)
    before = incl - onehot + carry_ref[:, 0:1]
    carry_ref[...] = carry_ref[...] + incl[:, tm - 1:tm]
    rank1 = jnp.sum(jnp.where(row_f == i1, before, 0.0), axis=0, keepdims=True)
    rank2 = jnp.sum(jnp.where(row_f == i2, before, 0.0), axis=0, keepdims=True)

    out_row = lax.broadcasted_iota(jnp.int32, ri_ref.shape, 0)
    ri = jnp.where(out_row == 0, i1 - N_GROUPS,
                   jnp.where(out_row == 1, i2 - N_GROUPS, jnp.where(out_row == 2, rank1, jnp.where(out_row == 3, rank2, 0.0))))
    ri_ref[...] = ri.astype(jnp.int32)
    rw_ref[...] = jnp.where(out_row == 0, w1, jnp.where(out_row == 1, w2, 0.0))
    cnt_ref[...] = carry_ref[...].astype(jnp.int32)


def _router(h, g, w_hi, w_lo, bias):
    n = h.shape[0]
    tm = min(ROW_TILE, n)
    full = lambda a: pl.BlockSpec(a.shape, lambda i: (0,) * a.ndim)
    return pl.pallas_call(
        _router_kernel,
        grid=(n // tm,),
        in_specs=[pl.BlockSpec((tm, D_MODEL), lambda i: (i, 0)), full(g), full(w_hi), full(w_lo), full(bias)],
        out_specs=[pl.BlockSpec((SUBLANES, tm), lambda i: (0, i)), pl.BlockSpec((SUBLANES, tm), lambda i: (0, i)),
                   pl.BlockSpec((ROUTER_ROWS, LANES), lambda i: (i, 0))],
        out_shape=[jax.ShapeDtypeStruct((SUBLANES, n), jnp.int32), jax.ShapeDtypeStruct((SUBLANES, n), F32),
                   jax.ShapeDtypeStruct((n // tm * ROUTER_ROWS, LANES), jnp.int32)],
        scratch_shapes=[pltpu.VMEM((ROUTER_ROWS, LANES), F32)],
        compiler_params=_cparams(1),
        name="router",
    )(h, g, w_hi, w_lo, bias)


def _row_copy(src_ref, src_row, dst_ref, dst_row, sem):
    src = src_ref.at[pl.ds(pl.multiple_of(src_row * ROW_CHUNKS, ROW_CHUNKS), ROW_CHUNKS)]
    dst = dst_ref.at[pl.ds(pl.multiple_of(dst_row * ROW_CHUNKS, ROW_CHUNKS), ROW_CHUNKS)]
    return pltpu.make_async_copy(src, dst, sem)


def _to_token_tiles(ref, x):
    rows = x.shape[0]
    for c in range(ROW_CHUNKS):
        ref[pl.ds(c, rows, stride=ROW_CHUNKS), :] = x[:, c * LANES:(c + 1) * LANES]


def _from_token_tiles(ref):
    rows = ref.shape[0] // ROW_CHUNKS
    return jnp.concatenate([ref[pl.ds(c, rows, stride=ROW_CHUNKS), :] for c in range(ROW_CHUNKS)], axis=1)


def _dispatch_kernel(pos_ref, fill_ref, h_ref, g_ref, xs_ref, xn_scr, zero_scr, sem, fill_sem):
    tm = h_ref.shape[0]

    @pl.when(pl.program_id(0) == 0)
    def _():
        zero_scr[...] = jnp.zeros_like(zero_scr)
        for e in range(fill_ref.shape[0]):
            first = pl.multiple_of(fill_ref[e] * ROW_CHUNKS, ROW_CHUNKS)
            fill = pltpu.make_async_copy(zero_scr, xs_ref.at[pl.ds(first, EXPERT_TILE * ROW_CHUNKS)], fill_sem)
            fill.start()
            fill.wait()

    base = pl.program_id(0) * (2 * tm)
    _to_token_tiles(xn_scr, _rms(h_ref[...], g_ref[...]))

    def issue(r, _):
        _row_copy(xn_scr, r, xs_ref, pos_ref[base + 2 * r], sem).start()
        _row_copy(xn_scr, r, xs_ref, pos_ref[base + 2 * r + 1], sem).start()
        return 0

    lax.fori_loop(0, tm, issue, 0, unroll=DMA_UNROLL)
    for _ in range(2):
        pltpu.make_async_copy(xn_scr, xs_ref.at[pl.ds(0, tm * ROW_CHUNKS)], sem).wait()


def _dispatch(pos, fill_start, h, g, p_rows):
    n = h.shape[0]
    tm = min(ROW_TILE, n)
    grid_spec = pltpu.PrefetchScalarGridSpec(
        num_scalar_prefetch=2,
        grid=(n // tm,),
        in_specs=[pl.BlockSpec((tm, D_MODEL), lambda i, pos, fill: (i, 0)),
                  pl.BlockSpec(g.shape, lambda i, pos, fill: (0, 0))],
        out_specs=pl.BlockSpec(memory_space=pl.ANY),
        scratch_shapes=[pltpu.VMEM((tm * ROW_CHUNKS, LANES), F32),
                        pltpu.VMEM((EXPERT_TILE * ROW_CHUNKS, LANES), F32),
                        pltpu.SemaphoreType.DMA(()), pltpu.SemaphoreType.DMA(())],
    )
    return pl.pallas_call(
        _dispatch_kernel,
        grid_spec=grid_spec,
        out_shape=jax.ShapeDtypeStruct(((p_rows + EXPERT_TILE) * ROW_CHUNKS, LANES), F32),
        compiler_params=_cparams(1),
        name="dispatch",
    )(pos, fill_start, h, g)


def _expert_kernel(te_ref, ta_ref, tx_ref, x_ref, wg_ref, wu_ref, wd_ref, y_ref):
    j = pl.program_id(0)

    @pl.when(ta_ref[j] == 1)
    def _():
        x = _from_token_tiles(x_ref).astype(MXU_DTYPE)
        hid = jax.nn.silu(_mm(x, wg_ref[0])) * _mm(x, wu_ref[0])
        _to_token_tiles(y_ref, _mm(hid.astype(MXU_DTYPE), wd_ref[0]))

    @pl.when(ta_ref[j] == 0)
    def _():
        y_ref[...] = jnp.zeros_like(y_ref)


def _experts(tile_expert, tile_active, tile_x, xs, wg, wu, wd, p_rows):
    grid_spec = pltpu.PrefetchScalarGridSpec(
        num_scalar_prefetch=3,
        grid=(p_rows // EXPERT_TILE,),
        in_specs=[pl.BlockSpec((EXPERT_TILE * ROW_CHUNKS, LANES), lambda j, te, ta, tx: (tx[j], 0)),
                  pl.BlockSpec((1, D_MODEL, D_EXPERT), lambda j, te, ta, tx: (te[j], 0, 0)),
                  pl.BlockSpec((1, D_MODEL, D_EXPERT), lambda j, te, ta, tx: (te[j], 0, 0)),
                  pl.BlockSpec((1, D_EXPERT, D_MODEL), lambda j, te, ta, tx: (te[j], 0, 0))],
        out_specs=pl.BlockSpec((EXPERT_TILE * ROW_CHUNKS, LANES), lambda j, te, ta, tx: (j, 0)),
    )
    return pl.pallas_call(
        _expert_kernel,
        grid_spec=grid_spec,
        out_shape=jax.ShapeDtypeStruct((p_rows * ROW_CHUNKS, LANES), F32),
        compiler_params=_cparams(1),
        name="experts",
    )(tile_expert, tile_active, tile_x, xs, wg, wu, wd)


def _combine_kernel(pos_ref, h_ref, rw_ref, g_ref, ys_ref, o_ref, y1_scr, y2_scr, sem, *, normalize):
    tm = h_ref.shape[0]
    base = pl.program_id(0) * (2 * tm)

    def issue(r, _):
        _row_copy(ys_ref, pos_ref[base + 2 * r], y1_scr, r, sem).start()
        _row_copy(ys_ref, pos_ref[base + 2 * r + 1], y2_scr, r, sem).start()
        return 0

    lax.fori_loop(0, tm, issue, 0, unroll=DMA_UNROLL)
    for buf in (y1_scr, y2_scr):
        pltpu.make_async_copy(ys_ref.at[pl.ds(0, tm * ROW_CHUNKS)], buf, sem).wait()
    rw = rw_ref[...]
    out = h_ref[...] + (rw[:, 0:1] * _from_token_tiles(y1_scr) + rw[:, 1:2] * _from_token_tiles(y2_scr))
    o_ref[...] = _rms(out, g_ref[...]) if normalize else out


def _combine(pos, h, rw, g_final, ys, normalize):
    n = h.shape[0]
    tm = min(ROW_TILE, n)
    grid_spec = pltpu.PrefetchScalarGridSpec(
        num_scalar_prefetch=1,
        grid=(n // tm,),
        in_specs=[pl.BlockSpec((tm, D_MODEL), lambda i, pos: (i, 0)),
                  pl.BlockSpec((tm, LANES), lambda i, pos: (i, 0)),
                  pl.BlockSpec(g_final.shape, lambda i, pos: (0, 0)),
                  pl.BlockSpec(memory_space=pl.ANY)],
        out_specs=pl.BlockSpec((tm, D_MODEL), lambda i, pos: (i, 0)),
        scratch_shapes=[pltpu.VMEM((tm * ROW_CHUNKS, LANES), F32), pltpu.VMEM((tm * ROW_CHUNKS, LANES), F32),
                        pltpu.SemaphoreType.DMA(())],
    )
    return pl.pallas_call(
        functools.partial(_combine_kernel, normalize=normalize),
        grid_spec=grid_spec,
        out_shape=jax.ShapeDtypeStruct((n, D_MODEL), F32),
        compiler_params=_cparams(1),
        name="combine",
    )(pos, h, rw, g_final, ys)


def _moe(h, g, w_group, b_group, w_expert, b_expert, w_gate, w_up, w_down, g_final, last):
    n = h.shape[0]
    w_r = jnp.zeros((ROUTER_ROWS, D_MODEL), F32).at[:N_GROUPS].set(w_group.T)
    w_r = w_r.at[N_GROUPS:N_GROUPS + N_EXPERTS].set(w_expert.T)
    b_r = jnp.zeros((ROUTER_ROWS,), F32).at[:N_GROUPS].set(b_group).at[N_GROUPS:N_GROUPS + N_EXPERTS].set(b_expert)
    b_r = jnp.broadcast_to(b_r[:, None], (ROUTER_ROWS, LANES))
    w_hi = w_r.astype(MXU_DTYPE)
    w_lo = (w_r - w_hi.astype(F32)).astype(MXU_DTYPE)
    ri, rw_t, cnt = _router(h, g, w_hi, w_lo, b_r)
    rw = jnp.pad(rw_t[0:2].T, ((0, 0), (0, LANES - 2)))

    counts = cnt[-ROUTER_ROWS:, 0][N_GROUPS:N_GROUPS + N_EXPERTS]
    padded = (counts + EXPERT_TILE - 1) // EXPERT_TILE * EXPERT_TILE
    ends = jnp.cumsum(padded)
    starts = ends - padded
    experts = jnp.arange(N_EXPERTS, dtype=jnp.int32)
    seg_start = jnp.sum(jnp.where(ri[0:2, :, None] == experts, starts, 0), axis=-1)
    pos = (seg_start + ri[2:4]).T.reshape(-1).astype(jnp.int32)
    p_rows = 2 * n + N_EXPERTS * EXPERT_TILE
    tile_start = jnp.arange(p_rows // EXPERT_TILE, dtype=jnp.int32) * EXPERT_TILE
    tile_expert = jnp.minimum(jnp.sum(tile_start[:, None] >= ends[None, :], axis=1), N_EXPERTS - 1).astype(jnp.int32)
    tile_active = (tile_start < ends[-1]).astype(jnp.int32)
    tile_x = jnp.minimum(tile_start, ends[-1] - EXPERT_TILE) // EXPERT_TILE

    tail = jnp.minimum(ends[-1] + jnp.arange(N_EXPERTS + 1, dtype=jnp.int32) * EXPERT_TILE, p_rows)
    xs = _dispatch(pos, jnp.concatenate([starts + counts, tail]).astype(jnp.int32), h, g, p_rows)
    ys = _experts(tile_expert, tile_active, tile_x.astype(jnp.int32), xs, w_gate.astype(MXU_DTYPE),
                  w_up.astype(MXU_DTYPE), w_down.astype(MXU_DTYPE), p_rows)
    return _combine(pos, h, rw, g_final, ys, normalize=last)


def _even_weights(w_in):
    gate_cols = w_in[:, C_GATE:C_GATE + 3 * NSA_HEADS].reshape(D_MODEL, NSA_KV_HEADS, HEADS_PER_GROUP, 3)
    gate_cols = jnp.pad(gate_cols, ((0, 0), (0, 0), (0, 0), (0, SUBLANES - 3)))
    gate_cols = gate_cols.reshape(D_MODEL, NSA_KV_HEADS, HEADS_PER_GROUP * SUBLANES)
    gate_cols = jnp.pad(gate_cols, ((0, 0), (0, 0), (0, LANES - HEADS_PER_GROUP * SUBLANES)))
    return jnp.concatenate([w_in[:, :C_GATE], gate_cols.reshape(D_MODEL, NSA_KV_HEADS * LANES)], axis=1)


def _block_diag_halves(w):
    blocks = w.shape[0] // 2
    out = jnp.zeros((2, blocks * w.shape[1], blocks * w.shape[2]), w.dtype)
    for j in range(2):
        for k in range(blocks):
            out = out.at[j, k * w.shape[1]:(k + 1) * w.shape[1], k * w.shape[2]:(k + 1) * w.shape[2]].set(
                w[j * blocks + k])
    return out


def _even_mixer(h, g, batch, cos_t, sin_t, w_in, w_out, conv_w, conv_b, w_r, b_r, w_i, b_i, lam,
                pos_k, w1_k, w2_k, pos_v, w1_v, w2_v):
    n = h.shape[0]
    t = n // batch
    xl, gg, qn, qr, kcv, ksr, vs, kwr, vw, gates = _even_proj(
        h, g, _even_weights(w_in).astype(MXU_DTYPE), cos_t, sin_t)

    y_lru = _lru(xl, gg, conv_w, conv_b[None, :], _block_diag_halves(w_r).astype(MXU_DTYPE),
                 _block_diag_halves(w_i).astype(MXU_DTYPE), b_r[None, :], b_i[None, :],
                 jax.nn.softplus(-lam)[None, :], batch)

    nchunk = t // CMP_STRIDE
    x16 = kcv.reshape(batch, nchunk, CMP_STRIDE, 2 * NSA_KV_HEADS, HEAD_DIM).transpose(0, 3, 1, 2, 4)
    x16 = x16.reshape(batch, 2 * NSA_KV_HEADS, nchunk, CMP_STRIDE * HEAD_DIM)
    half = CMP_STRIDE * HEAD_DIM
    pos = jnp.stack([pos_k.reshape(2, 1, half), pos_v.reshape(2, 1, half)])
    w1 = jnp.stack([w1_k.reshape(2, half, CMP_HIDDEN), w1_v.reshape(2, half, CMP_HIDDEN)]).astype(MXU_DTYPE)
    w2 = jnp.stack([w2_k, w2_v]).astype(MXU_DTYPE)
    kvc = _compress(x16, pos, w1, w2)

    vct = kvc[:, NSA_KV_HEADS:].transpose(0, 1, 3, 2)
    y_nsa = _nsa(qn, qr, kvc, vct, ksr, vs, kwr, vw, gates, batch)
    w_out = w_out.astype(MXU_DTYPE)
    return _out_proj(h, y_lru, y_nsa, w_out[:LRU_WIDTH], w_out[LRU_WIDTH:])


def _rope_tables(positions):
    inv = ROPE_THETA ** (-jnp.arange(0, 2 * ROT_HALF, 2, dtype=F32) / (2 * ROT_HALF))
    ang = positions.reshape(-1).astype(F32)[:, None] * inv
    cos, sin = jnp.cos(ang), jnp.sin(ang)
    rest = HEAD_DIM - 2 * ROT_HALF
    cos_h = jnp.concatenate([cos, cos, jnp.ones((cos.shape[0], rest), F32)], axis=1)
    sin_h = jnp.concatenate([-sin, sin, jnp.zeros((cos.shape[0], rest), F32)], axis=1)
    reps = LANES // HEAD_DIM
    return jnp.tile(cos_h, (1, reps)), jnp.tile(sin_h, (1, reps))


def kernel(x, mem, positions, norm_mix, norm_xattn, norm_mem, norm_ffn, norm_final, even_w_in, even_w_out, lru_conv_w, lru_conv_b, lru_w_r, lru_b_r, lru_w_i, lru_b_i, lru_lambda, nsa_cmp_pos_k, nsa_cmp_w1_k, nsa_cmp_w2_k, nsa_cmp_pos_v, nsa_cmp_w1_v, nsa_cmp_w2_v, odd_w_in, odd_conv_w, odd_w_out, xa_wq, xa_wk, xa_wv, xa_wo, moe_w_group, moe_b_group, moe_w_expert, moe_b_expert, moe_w_gate, moe_w_up, moe_w_down):
    batch, t, d = x.shape
    n = batch * t
    depth = norm_mix.shape[0]
    cos_t, sin_t = _rope_tables(positions)
    h = x.reshape(n, d)
    mem2 = mem.reshape(-1, d)
    for layer in range(depth):
        g_mix = norm_mix[layer][None, :]
        if layer % 2 == 0:
            e = layer // 2
            h = _even_mixer(h, g_mix, batch, cos_t, sin_t, even_w_in[e], even_w_out[e], lru_conv_w[e],
                            lru_conv_b[e], lru_w_r[e], lru_b_r[e], lru_w_i[e], lru_b_i[e], lru_lambda[e],
                            nsa_cmp_pos_k[e], nsa_cmp_w1_k[e], nsa_cmp_w2_k[e],
                            nsa_cmp_pos_v[e], nsa_cmp_w1_v[e], nsa_cmp_w2_v[e])
        else:
            o = layer // 2
            h = _short_conv(h, g_mix, odd_w_in[o].astype(MXU_DTYPE), odd_conv_w[o],
                            odd_w_out[o].astype(MXU_DTYPE), batch)
        wkv = jnp.concatenate([xa_wk[layer], xa_wv[layer]], axis=1).astype(MXU_DTYPE)
        kv = _mem_kv(mem2, norm_mem[layer][None, :], wkv)
        h = _xattn(h, norm_xattn[layer][None, :], xa_wq[layer].astype(MXU_DTYPE), kv,
                   xa_wo[layer].astype(MXU_DTYPE), batch)
        h = _moe(h, norm_ffn[layer][None, :], moe_w_group[layer], moe_b_group[layer], moe_w_expert[layer],
                 moe_b_expert[layer], moe_w_gate[layer], moe_w_up[layer], moe_w_down[layer],
                 norm_final[None, :], last=layer == depth - 1)
    return h.reshape(batch, t, d)
```

```python
import functools

import jax
import jax.numpy as jnp
import numpy as np
from jax import lax
from jax.experimental import pallas as pl
from jax.experimental.pallas import tpu as pltpu

F32 = jnp.float32
MXU_DTYPE = jnp.bfloat16

D_MODEL = 1024
LRU_WIDTH = 512
LRU_CONV = 4
LRU_C = 8.0
NSA_HEADS = 8
NSA_KV_HEADS = 2
HEADS_PER_GROUP = NSA_HEADS // NSA_KV_HEADS
HEAD_DIM = 64
CMP_STRIDE = 16
CMP_LEN = 32
CMP_HIDDEN = 128
SEL_BLOCK = 64
SEL_TOP = 16
WINDOW = 512
ROT_HALF = 8
ROPE_THETA = 500000.0
SC_CONV = 3
XA_HEADS = 4
XA_HEAD_DIM = 256
N_GROUPS = 4
EXPERTS_PER_GROUP = 4
N_EXPERTS = 16
D_EXPERT = 512
EPS = 1e-6
NEG = -1e30
LOG2E = 1.4426950408889634
FORCE = 1e9

LANES = 128
SUBLANES = 8
VMEM_LIMIT = 56 * 1024 * 1024

C_XL, C_GL, C_Q, C_KCV, C_KS, C_VS, C_KW, C_VW, C_GATE, C_END = (
    0, 512, 1024, 1536, 1792, 1920, 2048, 2176, 2304, 2560)

ROW_TILE = 1024
NSA_TQ = 512
NSA_TK = 512
EXPERT_TILE = 256
DMA_UNROLL = 8
ROW_CHUNKS = D_MODEL // LANES


def _cparams(n_axes):
    return pltpu.CompilerParams(dimension_semantics=("arbitrary",) * n_axes,
                                vmem_limit_bytes=VMEM_LIMIT)


def _mm(a, b):
    return jnp.dot(a, b, preferred_element_type=F32)


def _mm_nt(a, b):
    return lax.dot_general(a, b, (((1,), (1,)), ((), ())), preferred_element_type=F32)


def _rms(x, g):
    return x * lax.rsqrt(jnp.mean(x * x, axis=-1, keepdims=True) + EPS) * g


def _shift_rows(xx, s, rows):
    if s == 0:
        return xx[SUBLANES:SUBLANES + rows]
    return pltpu.roll(xx, s, 0)[SUBLANES:SUBLANES + rows]


def _rope(x, cos_t, sin_t):
    width = x.shape[-1]
    lane = lax.broadcasted_iota(jnp.int32, x.shape, 1) % HEAD_DIM
    partner = jnp.where(lane < ROT_HALF, pltpu.roll(x, width - ROT_HALF, 1), pltpu.roll(x, ROT_HALF, 1))
    return x * cos_t + partner * sin_t


def _even_proj_kernel(h_ref, g_ref, w_ref, cos_ref, sin_ref,
                      xl_ref, gg_ref, qn_ref, qr_ref, kcv_ref, ksr_ref, vs_ref, kwr_ref, vw_ref,
                      gate_ref):
    xn = _rms(h_ref[...], g_ref[...]).astype(MXU_DTYPE)

    def proj(c0, c1):
        return _mm(xn, w_ref[:, c0:c1])

    cos_t = cos_ref[...]
    sin_t = sin_ref[...]
    xl_ref[...] = proj(C_XL, C_GL)
    gg_ref[...] = jax.nn.gelu(proj(C_GL, C_Q))
    q = proj(C_Q, C_KCV) * (HEAD_DIM ** -0.5)
    reps = (C_KCV - C_Q) // LANES
    cos_q = jnp.concatenate([cos_t] * reps, axis=1)
    sin_q = jnp.concatenate([sin_t] * reps, axis=1)
    _store_query_tiles(qn_ref, q, HEAD_DIM)
    _store_query_tiles(qr_ref, _rope(q, cos_q, sin_q) * LOG2E, HEAD_DIM)
    kcv_ref[...] = proj(C_KCV, C_KS)
    ksr_ref[...] = _rope(proj(C_KS, C_VS), cos_t, sin_t).astype(ksr_ref.dtype)
    vs_ref[...] = proj(C_VS, C_KW).T.astype(vs_ref.dtype)
    kwr_ref[...] = _rope(proj(C_KW, C_VW), cos_t, sin_t).astype(kwr_ref.dtype)
    vw_ref[...] = proj(C_VW, C_GATE).T.astype(vw_ref.dtype)
    _store_query_tiles(gate_ref, jax.nn.sigmoid(proj(C_GATE, C_END)), SUBLANES)


def _store_query_tiles(ref, x, per_head):
    group_width = x.shape[1] // NSA_KV_HEADS
    for grp in range(NSA_KV_HEADS):
        for j in range(x.shape[0] // NSA_TQ):
            blk = x[j * NSA_TQ:(j + 1) * NSA_TQ, grp * group_width:(grp + 1) * group_width].T
            ref[grp, j] = jnp.concatenate([blk[hh * per_head:(hh + 1) * per_head, :]
                                           for hh in range(HEADS_PER_GROUP)], axis=1).astype(ref.dtype)


def _even_proj(h, g, w, cos_t, sin_t):
    n = h.shape[0]
    tm = min(ROW_TILE, n)
    rows = HEADS_PER_GROUP * NSA_TQ
    row = lambda c: pl.BlockSpec((tm, c), lambda i: (i, 0))
    col = pl.BlockSpec((LANES, tm), lambda i: (0, i))
    full = lambda a: pl.BlockSpec(a.shape, lambda i: (0,) * a.ndim)
    tiles = lambda c: pl.BlockSpec((NSA_KV_HEADS, tm // NSA_TQ, c, rows), lambda i: (0, i, 0, 0))
    tiles_shape = lambda c, dt: jax.ShapeDtypeStruct((NSA_KV_HEADS, n // NSA_TQ, c, rows), dt)
    flat = lambda c, dt: jax.ShapeDtypeStruct((n, c), dt)
    return pl.pallas_call(
        _even_proj_kernel,
        grid=(n // tm,),
        in_specs=[row(D_MODEL), full(g), full(w), row(LANES), row(LANES)],
        out_specs=[row(512), row(512), tiles(HEAD_DIM), tiles(HEAD_DIM), row(256),
                   row(LANES), col, row(LANES), col, tiles(SUBLANES)],
        out_shape=[flat(512, F32), flat(512, F32), tiles_shape(HEAD_DIM, MXU_DTYPE), tiles_shape(HEAD_DIM, MXU_DTYPE),
                   flat(256, F32), flat(LANES, MXU_DTYPE), jax.ShapeDtypeStruct((LANES, n), MXU_DTYPE),
                   flat(LANES, MXU_DTYPE), jax.ShapeDtypeStruct((LANES, n), MXU_DTYPE),
                   tiles_shape(SUBLANES, F32)],
        compiler_params=_cparams(1),
        name="even_proj",
    )(h, g, w, cos_t, sin_t)


def _lru_kernel(xl_ref, gg_ref, cw_ref, cb_ref, wr_ref, wi_ref, br_ref, bi_ref, sp_ref,
                y_ref, tail_ref, h_ref, a_scr, u_scr):
    tt = xl_ref.shape[0]

    @pl.when(pl.program_id(1) == 0)
    def _():
        tail_ref[...] = jnp.zeros_like(tail_ref)
        h_ref[...] = jnp.zeros_like(h_ref)

    x = xl_ref[...]
    xx = jnp.concatenate([tail_ref[...], x], axis=0)
    tail_ref[...] = x[tt - SUBLANES:tt]
    xc = cb_ref[...] + sum(cw_ref[k:k + 1, :] * _shift_rows(xx, LRU_CONV - 1 - k, tt)
                           for k in range(LRU_CONV))
    xcb = xc.astype(MXU_DTYPE)
    half = LRU_WIDTH // 2
    r_lin = jnp.concatenate([_mm(xcb[:, j * half:(j + 1) * half], wr_ref[j]) for j in range(2)], axis=1)
    i_lin = jnp.concatenate([_mm(xcb[:, j * half:(j + 1) * half], wi_ref[j]) for j in range(2)], axis=1)
    r = jax.nn.sigmoid(r_lin + br_ref[...])
    i = jax.nn.sigmoid(i_lin + bi_ref[...])
    log_a = -LRU_C * r * sp_ref[...]
    a = jnp.exp(log_a)
    u = jnp.sqrt(jnp.tanh(-log_a) * (1.0 + a * a)) * (i * xc)

    r8 = lax.broadcasted_iota(jnp.int32, a.shape, 0) % SUBLANES
    for s in (1, 2, 4):
        keep = r8 >= s
        u = jnp.where(keep, a * pltpu.roll(u, s, 0) + u, u)
        a = jnp.where(keep, a * pltpu.roll(a, s, 0), a)
    a_scr[...] = a
    u_scr[...] = u

    def body(gidx, h):
        r0 = pl.multiple_of(gidx * SUBLANES, SUBLANES)
        out = a_scr[pl.ds(r0, SUBLANES), :] * h + u_scr[pl.ds(r0, SUBLANES), :]
        u_scr[pl.ds(r0, SUBLANES), :] = out
        return out[SUBLANES - 1:SUBLANES, :]

    h_ref[...] = lax.fori_loop(0, tt // SUBLANES, body, h_ref[...])
    y_ref[...] = (u_scr[...] * gg_ref[...]).astype(y_ref.dtype)


def _lru(xl, gg, cw, cb, wr, wi, br, bi, sp, batch):
    n = xl.shape[0]
    t = n // batch
    tt = min(ROW_TILE, t)
    nt = t // tt
    row = pl.BlockSpec((tt, LRU_WIDTH), lambda b, i: (b * nt + i, 0))
    full = lambda a: pl.BlockSpec(a.shape, lambda b, i: (0,) * a.ndim)
    return pl.pallas_call(
        _lru_kernel,
        grid=(batch, nt),
        in_specs=[row, row] + [full(a) for a in (cw, cb, wr, wi, br, bi, sp)],
        out_specs=row,
        out_shape=jax.ShapeDtypeStruct((n, LRU_WIDTH), MXU_DTYPE),
        scratch_shapes=[pltpu.VMEM((SUBLANES, LRU_WIDTH), F32), pltpu.VMEM((1, LRU_WIDTH), F32),
                        pltpu.VMEM((tt, LRU_WIDTH), F32), pltpu.VMEM((tt, LRU_WIDTH), F32)],
        compiler_params=_cparams(2),
        name="lru",
    )(xl, gg, cw, cb, wr, wi, br, bi, sp)


def _compress_kernel(x_ref, pos_ref, w1_ref, w2_ref, o_ref):
    x = x_ref[0, 0]
    nchunk = x.shape[0]
    lo = (x + pos_ref[0, 0]).astype(MXU_DTYPE)
    hi = (x + pos_ref[0, 1]).astype(MXU_DTYPE)
    p_lo = _mm(lo, w1_ref[0, 0])
    p_hi = _mm(hi, w1_ref[0, 1])
    a = p_lo + pltpu.roll(p_hi, nchunk - 1, 0)
    o_ref[0, 0] = _mm(jax.nn.gelu(a).astype(MXU_DTYPE), w2_ref[0]).astype(o_ref.dtype)


def _compress(x16, pos, w1, w2):
    b, _, nchunk, width = x16.shape
    return pl.pallas_call(
        _compress_kernel,
        grid=(b, 2 * NSA_KV_HEADS),
        in_specs=[pl.BlockSpec((1, 1, nchunk, width), lambda i, j: (i, j, 0, 0)),
                  pl.BlockSpec((1, 2, 1, width), lambda i, j: (j // NSA_KV_HEADS, 0, 0, 0)),
                  pl.BlockSpec((1, 2, width, CMP_HIDDEN), lambda i, j: (j // NSA_KV_HEADS, 0, 0, 0)),
                  pl.BlockSpec((1, CMP_HIDDEN, HEAD_DIM), lambda i, j: (j // NSA_KV_HEADS, 0, 0))],
        out_specs=pl.BlockSpec((1, 1, nchunk, HEAD_DIM), lambda i, j: (i, j, 0, 0)),
        out_shape=jax.ShapeDtypeStruct((b, 2 * NSA_KV_HEADS, nchunk, HEAD_DIM), MXU_DTYPE),
        compiler_params=_cparams(2),
        name="compress",
    )(x16, pos, w1, w2)


def _nsa_kernel(qn_ref, qr_ref, kc_ref, vct_ref, ks_ref, vst_ref, kw_ref, vwt_ref, gate_ref,
                y_ref, chosen_scr, acc_scr, accw_scr, sc_scr):
    rows = qn_ref.shape[3]
    tq = rows // HEADS_PER_GROUP
    t = ks_ref.shape[0]
    nc = kc_ref.shape[2]
    nsel = t // SEL_BLOCK
    n_top = min(SEL_TOP, nsel)
    tk = NSA_TK
    t0 = pl.program_id(2) * tq
    qn = qn_ref[0, 0]
    grp = pl.program_id(1)
    qr64 = qr_ref[0, 0]
    qr = jnp.concatenate([jnp.where(grp == gi, qr64, jnp.zeros_like(qr64)) for gi in range(NSA_KV_HEADS)], axis=0)

    s = _mm(kc_ref[0, 0], qn)
    tq_row = t0 + lax.broadcasted_iota(jnp.int32, (nc, rows), 1) % tq
    cmp_end = lax.broadcasted_iota(jnp.int32, (nc, rows), 0) * CMP_STRIDE + (CMP_LEN - 1)
    valid = cmp_end <= tq_row
    s = jnp.where(valid, s, NEG)
    e = jnp.where(valid, jnp.exp(s - jnp.max(s, axis=0, keepdims=True)), 0.0)
    den = jnp.sum(e, axis=0, keepdims=True)
    p = e * (1.0 / jnp.where(den > 0.0, den, 1.0))
    o_cmp = _mm(vct_ref[0, 0], p.astype(MXU_DTYPE))

    psum = p[:, 0:tq]
    for hh in range(1, HEADS_PER_GROUP):
        psum = psum + p[:, hh * tq:(hh + 1) * tq]
    p_hi = psum.astype(MXU_DTYPE)
    p_lo = (psum - p_hi.astype(F32)).astype(MXU_DTYPE)
    cj = lax.broadcasted_iota(jnp.int32, (nsel, nc), 0)
    cn = lax.broadcasted_iota(jnp.int32, (nsel, nc), 1)
    ratio = SEL_BLOCK // CMP_STRIDE
    cover = jnp.where((cn >= ratio * cj - (CMP_LEN // CMP_STRIDE - 1)) & (cn <= ratio * cj + ratio - 1)
                      & (cn < nc - 1), 1.0, 0.0).astype(MXU_DTYPE)
    imp = _mm(cover, p_hi) + _mm(cover, p_lo)

    blk = lax.broadcasted_iota(jnp.int32, (nsel, tq), 0)
    tq_col = t0 + lax.broadcasted_iota(jnp.int32, (nsel, tq), 1)
    cur = tq_col // SEL_BLOCK
    forced = (blk == 0) | (blk == cur) | (blk == cur - 1)
    causal = blk * SEL_BLOCK <= tq_col
    score0 = jnp.where(causal, jnp.where(forced, FORCE, imp), NEG)
    blk_f = blk.astype(F32)

    def pick_one(_, carry):
        score, chosen = carry
        best = jnp.max(score, axis=0, keepdims=True)
        first = jnp.min(jnp.where(score == best, blk_f, float(nsel)), axis=0, keepdims=True)
        hit = blk_f == first
        return jnp.where(hit, -jnp.inf, score), jnp.where(hit, 1.0, chosen)

    _, chosen = lax.fori_loop(0, n_top, pick_one, (score0, jnp.zeros((nsel, tq), F32)))
    chosen_scr[...] = chosen

    def scores(k_ref, kt, bias):
        k0 = pl.multiple_of(kt * tk, tk)
        return _mm(k_ref[pl.ds(k0, tk), :], qr) + jnp.concatenate([bias] * HEADS_PER_GROUP, axis=1)

    def soft(sc, m_old, l_old):
        m_new = jnp.maximum(m_old, jnp.max(sc, axis=0, keepdims=True))
        alpha = jnp.exp2(m_old - m_new)
        pe = jnp.exp2(sc - m_new)
        return m_new, alpha * l_old + jnp.sum(pe, axis=0, keepdims=True), alpha, pe.astype(MXU_DTYPE)

    def accumulate(vt_ref, acc_ref, kt, alpha, pe):
        k0 = pl.multiple_of(kt * tk, tk)
        acc_ref[...] = alpha * acc_ref[...] + _mm(vt_ref[:, pl.ds(k0, tk)], pe)

    expand = jnp.where(lax.broadcasted_iota(jnp.int32, (tk, SUBLANES), 0) // SEL_BLOCK
                       == lax.broadcasted_iota(jnp.int32, (tk, SUBLANES), 1), 1.0, 0.0)

    def key_minus_query(kt):
        return (kt * tk - t0 + lax.broadcasted_iota(jnp.int32, (tk, tq), 0)
                - lax.broadcasted_iota(jnp.int32, (tk, tq), 1))

    def sel_bias(kt, causal=True):
        grp = pl.multiple_of(kt * SUBLANES, SUBLANES)
        bias = _mm(expand, (chosen_scr[pl.ds(grp, SUBLANES), :] - 1.0) * (-NEG))
        return jnp.where(key_minus_query(kt) <= 0, bias, NEG) if causal else bias

    def win_bias(kt, diagonal):
        d = key_minus_query(kt)
        inside = (d > -WINDOW) & (d <= 0) if diagonal else d > -WINDOW
        return jnp.where(inside, 0.0, NEG)

    def sel_only(kt, carry):
        m_s, l_s, m_w, l_w = carry
        sc_next = scores(ks_ref, kt + 1, sel_bias(kt + 1, causal=False))
        m_s, l_s, alpha, pe = soft(sc_scr[...], m_s, l_s)
        accumulate(vst_ref, acc_scr, kt, alpha, pe)
        sc_scr[...] = sc_next
        return m_s, l_s, m_w, l_w

    def both(kt, carry, last=False):
        m_s, l_s, m_w, l_w = carry
        sc_w = scores(kw_ref, kt, win_bias(kt, last))
        if not last:
            sc_next = scores(ks_ref, kt + 1, sel_bias(kt + 1))
        m_s, l_s, alpha_s, pe_s = soft(sc_scr[...], m_s, l_s)
        m_w, l_w, alpha_w, pe_w = soft(sc_w, m_w, l_w)
        accumulate(vst_ref, acc_scr, kt, alpha_s, pe_s)
        accumulate(vwt_ref, accw_scr, kt, alpha_w, pe_w)
        if not last:
            sc_scr[...] = sc_next
        return m_s, l_s, m_w, l_w

    kt_last = (t0 + tq - 1) // tk
    win_lo = jnp.maximum(t0 - (WINDOW - 1), 0) // tk
    acc_scr[...] = jnp.zeros_like(acc_scr)
    accw_scr[...] = jnp.zeros_like(accw_scr)
    lowest = jnp.full((1, rows), NEG, F32)
    zero = jnp.zeros((1, rows), F32)
    sc_scr[...] = scores(ks_ref, 0, sel_bias(0))
    carry = lax.fori_loop(0, win_lo, sel_only, (lowest, zero, lowest, zero))
    carry = lax.fori_loop(win_lo, kt_last, both, carry)
    _, l_s, _, l_w = both(kt_last, carry, last=True)
    o_sel = acc_scr[...] * (1.0 / l_s)
    o_win = accw_scr[...] * (1.0 / l_w)

    gate = gate_ref[0, 0]
    y = gate[0:1] * o_cmp + gate[1:2] * o_sel + gate[2:3] * o_win
    y = jnp.concatenate([y[:, hh * tq:(hh + 1) * tq] for hh in range(HEADS_PER_GROUP)], axis=0)
    y_ref[...] = y.T.astype(y_ref.dtype)


def _nsa(qn_t, qr_t, kvc, vct, ks, vst, kw, vwt, gates_t, batch):
    _, n_tiles, _, rows = qn_t.shape
    tq = rows // HEADS_PER_GROUP
    nq = n_tiles // batch
    t = nq * tq
    nc = kvc.shape[2]
    assert NSA_TK == SEL_BLOCK * SUBLANES and t % NSA_TK == 0
    tile = lambda c: pl.BlockSpec((1, 1, c, rows), lambda i, g, j: (g, i * nq + j, 0, 0))
    k_spec = pl.BlockSpec((t, LANES), lambda i, g, j: (i, 0))
    vt_spec = pl.BlockSpec((HEAD_DIM, t), lambda i, g, j: (g, i))
    return pl.pallas_call(
        _nsa_kernel,
        grid=(batch, NSA_KV_HEADS, nq),
        in_specs=[tile(HEAD_DIM), tile(HEAD_DIM),
                  pl.BlockSpec((1, 1, nc, HEAD_DIM), lambda i, g, j: (i, g, 0, 0)),
                  pl.BlockSpec((1, 1, HEAD_DIM, nc), lambda i, g, j: (i, g, 0, 0)),
                  k_spec, vt_spec, k_spec, vt_spec, tile(SUBLANES)],
        out_specs=pl.BlockSpec((tq, HEADS_PER_GROUP * HEAD_DIM), lambda i, g, j: (i * nq + j, g)),
        out_shape=jax.ShapeDtypeStruct((batch * t, NSA_HEADS * HEAD_DIM), MXU_DTYPE),
        scratch_shapes=[pltpu.VMEM((t // SEL_BLOCK, tq), F32),
                        pltpu.VMEM((HEAD_DIM, rows), F32), pltpu.VMEM((HEAD_DIM, rows), F32),
                        pltpu.VMEM((NSA_TK, rows), F32)],
        compiler_params=_cparams(3),
        name="nsa",
    )(qn_t, qr_t, kvc, vct, ks, vst, kw, vwt, gates_t)


def _out_proj_kernel(h_ref, a_ref, b_ref, wa_ref, wb_ref, o_ref):
    o_ref[...] = h_ref[...] + _mm(a_ref[...], wa_ref[...]) + _mm(b_ref[...], wb_ref[...])


def _out_proj(h, a, b, wa, wb):
    n = h.shape[0]
    tm = min(ROW_TILE, n)
    full = lambda x: pl.BlockSpec(x.shape, lambda i: (0,) * x.ndim)
    return pl.pallas_call(
        _out_proj_kernel,
        grid=(n // tm,),
        in_specs=[pl.BlockSpec((tm, D_MODEL), lambda i: (i, 0)),
                  pl.BlockSpec((tm, a.shape[1]), lambda i: (i, 0)),
                  pl.BlockSpec((tm, b.shape[1]), lambda i: (i, 0)), full(wa), full(wb)],
        out_specs=pl.BlockSpec((tm, D_MODEL), lambda i: (i, 0)),
        out_shape=jax.ShapeDtypeStruct((n, D_MODEL), F32),
        compiler_params=_cparams(1),
        name="out_proj",
    )(h, a, b, wa, wb)


def _short_conv_kernel(h_ref, g_ref, win_ref, cw_ref, wout_ref, o_ref, tail_ref):
    tt = h_ref.shape[0]

    @pl.when(pl.program_id(1) == 0)
    def _():
        tail_ref[...] = jnp.zeros_like(tail_ref)

    h = h_ref[...]
    xn = _rms(h, g_ref[...]).astype(MXU_DTYPE)
    b_g = _mm(xn, win_ref[:, 0:D_MODEL])
    cv = _mm(xn, win_ref[:, D_MODEL:2 * D_MODEL]) * _mm(xn, win_ref[:, 2 * D_MODEL:3 * D_MODEL])
    xx = jnp.concatenate([tail_ref[...], cv], axis=0)
    tail_ref[...] = cv[tt - SUBLANES:tt]
    conv = sum(cw_ref[k:k + 1, :] * _shift_rows(xx, SC_CONV - 1 - k, tt) for k in range(SC_CONV))
    o_ref[...] = h + _mm((b_g * conv).astype(MXU_DTYPE), wout_ref[...])


def _short_conv(h, g, w_in, cw, w_out, batch):
    n = h.shape[0]
    t = n // batch
    tt = min(ROW_TILE, t)
    nt = t // tt
    row = pl.BlockSpec((tt, D_MODEL), lambda b, i: (b * nt + i, 0))
    full = lambda a: pl.BlockSpec(a.shape, lambda b, i: (0,) * a.ndim)
    return pl.pallas_call(
        _short_conv_kernel,
        grid=(batch, nt),
        in_specs=[row, full(g), full(w_in), full(cw), full(w_out)],
        out_specs=row,
        out_shape=jax.ShapeDtypeStruct((n, D_MODEL), F32),
        scratch_shapes=[pltpu.VMEM((SUBLANES, D_MODEL), F32)],
        compiler_params=_cparams(2),
        name="short_conv",
    )(h, g, w_in, cw, w_out)


def _mem_kv_kernel(m_ref, g_ref, w_ref, o_ref):
    o_ref[...] = _mm(_rms(m_ref[...], g_ref[...]).astype(MXU_DTYPE), w_ref[...]).astype(o_ref.dtype)


def _mem_kv(mem, g, wkv):
    n = mem.shape[0]
    tm = min(ROW_TILE, n)
    tn = 1024
    return pl.pallas_call(
        _mem_kv_kernel,
        grid=(n // tm, wkv.shape[1] // tn),
        in_specs=[pl.BlockSpec((tm, D_MODEL), lambda i, j: (i, 0)),
                  pl.BlockSpec(g.shape, lambda i, j: (0, 0)),
                  pl.BlockSpec((D_MODEL, tn), lambda i, j: (0, j))],
        out_specs=pl.BlockSpec((tm, tn), lambda i, j: (i, j)),
        out_shape=jax.ShapeDtypeStruct((n, wkv.shape[1]), MXU_DTYPE),
        compiler_params=_cparams(2),
        name="mem_kv",
    )(mem, g, wkv)


def _xattn_kernel(h_ref, g_ref, wq_ref, kv_ref, wo_ref, o_ref):
    h = h_ref[...]
    xn = _rms(h, g_ref[...]).astype(MXU_DTYPE)
    q = (_mm(xn, wq_ref[...]) * (XA_HEAD_DIM ** -0.5)).astype(MXU_DTYPE)
    width = XA_HEADS * XA_HEAD_DIM
    outs = []
    for hd in range(XA_HEADS):
        sl = slice(hd * XA_HEAD_DIM, (hd + 1) * XA_HEAD_DIM)
        s = _mm_nt(q[:, sl], kv_ref[:, sl])
        e = jnp.exp(s - jnp.max(s, axis=-1, keepdims=True))
        p = e / jnp.sum(e, axis=-1, keepdims=True)
        outs.append(_mm(p.astype(MXU_DTYPE), kv_ref[:, width + hd * XA_HEAD_DIM:width + (hd + 1) * XA_HEAD_DIM]))
    o = jnp.concatenate(outs, axis=1).astype(MXU_DTYPE)
    o_ref[...] = h + _mm(o, wo_ref[...])


def _xattn(h, g, wq, kv, wo, batch):
    n = h.shape[0]
    t = n // batch
    tm = min(ROW_TILE, t)
    nt = t // tm
    mlen = kv.shape[0] // batch
    full = lambda a: pl.BlockSpec(a.shape, lambda b, i: (0,) * a.ndim)
    row = pl.BlockSpec((tm, D_MODEL), lambda b, i: (b * nt + i, 0))
    return pl.pallas_call(
        _xattn_kernel,
        grid=(batch, nt),
        in_specs=[row, full(g), full(wq), pl.BlockSpec((mlen, kv.shape[1]), lambda b, i: (b, 0)), full(wo)],
        out_specs=row,
        out_shape=jax.ShapeDtypeStruct((n, D_MODEL), F32),
        compiler_params=_cparams(2),
        name="xattn",
    )(h, g, wq, kv, wo)


ROUTER_ROWS = 32


def _router_kernel(h_ref, g_ref, whi_ref, wlo_ref, b_ref, ri_ref, rw_ref, cnt_ref, carry_ref):
    tm = h_ref.shape[0]

    @pl.when(pl.program_id(0) == 0)
    def _():
        carry_ref[...] = jnp.zeros_like(carry_ref)

    xn = _rms(h_ref[...], g_ref[...])
    x_hi = xn.astype(MXU_DTYPE)
    x_lo = (xn - x_hi.astype(F32)).astype(MXU_DTYPE)
    logits = (_mm_nt(whi_ref[...], x_hi) + (_mm_nt(whi_ref[...], x_lo) + _mm_nt(wlo_ref[...], x_hi))
              + b_ref[:, 0:1])

    row = lax.broadcasted_iota(jnp.int32, logits.shape, 0)
    row_f = row.astype(F32)
    none = float(ROUTER_ROWS)
    is_g = row < N_GROUPS
    g_max = jnp.max(jnp.where(is_g, logits, -jnp.inf), axis=0, keepdims=True)
    g_sum = jnp.sum(jnp.where(is_g, jnp.exp(logits - g_max), 0.0), axis=0, keepdims=True)
    g_top = 1.0 / g_sum
    g_idx = jnp.min(jnp.where(is_g & (logits == g_max), row_f, none), axis=0, keepdims=True)
    first = N_GROUPS + EXPERTS_PER_GROUP * g_idx
    in_g = (row_f >= first) & (row_f < first + EXPERTS_PER_GROUP)
    e1 = jnp.max(jnp.where(in_g, logits, -jnp.inf), axis=0, keepdims=True)
    i1 = jnp.min(jnp.where(in_g & (logits == e1), row_f, none), axis=0, keepdims=True)
    rest = in_g & (row_f != i1)
    e2 = jnp.max(jnp.where(rest, logits, -jnp.inf), axis=0, keepdims=True)
    i2 = jnp.min(jnp.where(rest & (logits == e2), row_f, none), axis=0, keepdims=True)
    ratio = jnp.exp(e2 - e1)
    w1 = g_top / (1.0 + ratio)
    w2 = g_top * ratio / (1.0 + ratio)

    onehot = jnp.where((row_f == i1) | (row_f == i2), 1.0, 0.0)
    tri = jnp.where(lax.broadcasted_iota(jnp.int32, (tm, tm), 0) <= lax.broadcasted_iota(jnp.int32, (tm, tm), 1),
                    1.0, 0.0).astype(MXU_DTYPE)
    incl = _mm(onehot.astype(MXU_DTYPE), tri)
    before = incl - onehot + carry_ref[:, 0:1]
    carry_ref[...] = carry_ref[...] + incl[:, tm - 1:tm]
    rank1 = jnp.sum(jnp.where(row_f == i1, before, 0.0), axis=0, keepdims=True)
    rank2 = jnp.sum(jnp.where(row_f == i2, before, 0.0), axis=0, keepdims=True)

    out_row = lax.broadcasted_iota(jnp.int32, ri_ref.shape, 0)
    ri = jnp.where(out_row == 0, i1 - N_GROUPS,
                   jnp.where(out_row == 1, i2 - N_GROUPS, jnp.where(out_row == 2, rank1, jnp.where(out_row == 3, rank2, 0.0))))
    ri_ref[...] = ri.astype(jnp.int32)
    rw_ref[...] = jnp.where(out_row == 0, w1, jnp.where(out_row == 1, w2, 0.0))
    cnt_ref[...] = carry_ref[...].astype(jnp.int32)


def _router(h, g, w_hi, w_lo, bias):
    n = h.shape[0]
    tm = min(ROW_TILE, n)
    full = lambda a: pl.BlockSpec(a.shape, lambda i: (0,) * a.ndim)
    return pl.pallas_call(
        _router_kernel,
        grid=(n // tm,),
        in_specs=[pl.BlockSpec((tm, D_MODEL), lambda i: (i, 0)), full(g), full(w_hi), full(w_lo), full(bias)],
        out_specs=[pl.BlockSpec((SUBLANES, tm), lambda i: (0, i)), pl.BlockSpec((SUBLANES, tm), lambda i: (0, i)),
                   pl.BlockSpec((ROUTER_ROWS, LANES), lambda i: (i, 0))],
        out_shape=[jax.ShapeDtypeStruct((SUBLANES, n), jnp.int32), jax.ShapeDtypeStruct((SUBLANES, n), F32),
                   jax.ShapeDtypeStruct((n // tm * ROUTER_ROWS, LANES), jnp.int32)],
        scratch_shapes=[pltpu.VMEM((ROUTER_ROWS, LANES), F32)],
        compiler_params=_cparams(1),
        name="router",
    )(h, g, w_hi, w_lo, bias)


def _row_copy(src_ref, src_row, dst_ref, dst_row, sem):
    src = src_ref.at[pl.ds(pl.multiple_of(src_row * ROW_CHUNKS, ROW_CHUNKS), ROW_CHUNKS)]
    dst = dst_ref.at[pl.ds(pl.multiple_of(dst_row * ROW_CHUNKS, ROW_CHUNKS), ROW_CHUNKS)]
    return pltpu.make_async_copy(src, dst, sem)


def _to_token_tiles(ref, x):
    rows = x.shape[0]
    for c in range(ROW_CHUNKS):
        ref[pl.ds(c, rows, stride=ROW_CHUNKS), :] = x[:, c * LANES:(c + 1) * LANES]


def _from_token_tiles(ref):
    rows = ref.shape[0] // ROW_CHUNKS
    return jnp.concatenate([ref[pl.ds(c, rows, stride=ROW_CHUNKS), :] for c in range(ROW_CHUNKS)], axis=1)


def _dispatch_kernel(pos_ref, fill_ref, h_ref, g_ref, xs_ref, xn_scr, zero_scr, sem, fill_sem):
    tm = h_ref.shape[0]

    @pl.when(pl.program_id(0) == 0)
    def _():
        zero_scr[...] = jnp.zeros_like(zero_scr)
        for e in range(fill_ref.shape[0]):
            first = pl.multiple_of(fill_ref[e] * ROW_CHUNKS, ROW_CHUNKS)
            fill = pltpu.make_async_copy(zero_scr, xs_ref.at[pl.ds(first, EXPERT_TILE * ROW_CHUNKS)], fill_sem)
            fill.start()
            fill.wait()

    base = pl.program_id(0) * (2 * tm)
    _to_token_tiles(xn_scr, _rms(h_ref[...], g_ref[...]))

    def issue(r, _):
        _row_copy(xn_scr, r, xs_ref, pos_ref[base + 2 * r], sem).start()
        _row_copy(xn_scr, r, xs_ref, pos_ref[base + 2 * r + 1], sem).start()
        return 0

    lax.fori_loop(0, tm, issue, 0, unroll=DMA_UNROLL)
    for _ in range(2):
        pltpu.make_async_copy(xn_scr, xs_ref.at[pl.ds(0, tm * ROW_CHUNKS)], sem).wait()


def _dispatch(pos, fill_start, h, g, p_rows):
    n = h.shape[0]
    tm = min(ROW_TILE, n)
    grid_spec = pltpu.PrefetchScalarGridSpec(
        num_scalar_prefetch=2,
        grid=(n // tm,),
        in_specs=[pl.BlockSpec((tm, D_MODEL), lambda i, pos, fill: (i, 0)),
                  pl.BlockSpec(g.shape, lambda i, pos, fill: (0, 0))],
        out_specs=pl.BlockSpec(memory_space=pl.ANY),
        scratch_shapes=[pltpu.VMEM((tm * ROW_CHUNKS, LANES), F32),
                        pltpu.VMEM((EXPERT_TILE * ROW_CHUNKS, LANES), F32),
                        pltpu.SemaphoreType.DMA(()), pltpu.SemaphoreType.DMA(())],
    )
    return pl.pallas_call(
        _dispatch_kernel,
        grid_spec=grid_spec,
        out_shape=jax.ShapeDtypeStruct(((p_rows + EXPERT_TILE) * ROW_CHUNKS, LANES), F32),
        compiler_params=_cparams(1),
        name="dispatch",
    )(pos, fill_start, h, g)


def _expert_kernel(te_ref, ta_ref, tx_ref, x_ref, wg_ref, wu_ref, wd_ref, y_ref):
    j = pl.program_id(0)

    @pl.when(ta_ref[j] == 1)
    def _():
        x = _from_token_tiles(x_ref).astype(MXU_DTYPE)
        hid = jax.nn.silu(_mm(x, wg_ref[0])) * _mm(x, wu_ref[0])
        _to_token_tiles(y_ref, _mm(hid.astype(MXU_DTYPE), wd_ref[0]))

    @pl.when(ta_ref[j] == 0)
    def _():
        y_ref[...] = jnp.zeros_like(y_ref)


def _experts(tile_expert, tile_active, tile_x, xs, wg, wu, wd, p_rows):
    grid_spec = pltpu.PrefetchScalarGridSpec(
        num_scalar_prefetch=3,
        grid=(p_rows // EXPERT_TILE,),
        in_specs=[pl.BlockSpec((EXPERT_TILE * ROW_CHUNKS, LANES), lambda j, te, ta, tx: (tx[j], 0)),
                  pl.BlockSpec((1, D_MODEL, D_EXPERT), lambda j, te, ta, tx: (te[j], 0, 0)),
                  pl.BlockSpec((1, D_MODEL, D_EXPERT), lambda j, te, ta, tx: (te[j], 0, 0)),
                  pl.BlockSpec((1, D_EXPERT, D_MODEL), lambda j, te, ta, tx: (te[j], 0, 0))],
        out_specs=pl.BlockSpec((EXPERT_TILE * ROW_CHUNKS, LANES), lambda j, te, ta, tx: (j, 0)),
    )
    return pl.pallas_call(
        _expert_kernel,
        grid_spec=grid_spec,
        out_shape=jax.ShapeDtypeStruct((p_rows * ROW_CHUNKS, LANES), F32),
        compiler_params=_cparams(1),
        name="experts",
    )(tile_expert, tile_active, tile_x, xs, wg, wu, wd)


def _combine_kernel(pos_ref, h_ref, rw_ref, g_ref, ys_ref, o_ref, y1_scr, y2_scr, sem, *, normalize):
    tm = h_ref.shape[0]
    base = pl.program_id(0) * (2 * tm)

    def issue(r, _):
        _row_copy(ys_ref, pos_ref[base + 2 * r], y1_scr, r, sem).start()
        _row_copy(ys_ref, pos_ref[base + 2 * r + 1], y2_scr, r, sem).start()
        return 0

    lax.fori_loop(0, tm, issue, 0, unroll=DMA_UNROLL)
    for buf in (y1_scr, y2_scr):
        pltpu.make_async_copy(ys_ref.at[pl.ds(0, tm * ROW_CHUNKS)], buf, sem).wait()
    rw = rw_ref[...]
    out = h_ref[...] + (rw[:, 0:1] * _from_token_tiles(y1_scr) + rw[:, 1:2] * _from_token_tiles(y2_scr))
    o_ref[...] = _rms(out, g_ref[...]) if normalize else out


def _combine(pos, h, rw, g_final, ys, normalize):
    n = h.shape[0]
    tm = min(ROW_TILE, n)
    grid_spec = pltpu.PrefetchScalarGridSpec(
        num_scalar_prefetch=1,
        grid=(n // tm,),
        in_specs=[pl.BlockSpec((tm, D_MODEL), lambda i, pos: (i, 0)),
                  pl.BlockSpec((tm, LANES), lambda i, pos: (i, 0)),
                  pl.BlockSpec(g_final.shape, lambda i, pos: (0, 0)),
                  pl.BlockSpec(memory_space=pl.ANY)],
        out_specs=pl.BlockSpec((tm, D_MODEL), lambda i, pos: (i, 0)),
        scratch_shapes=[pltpu.VMEM((tm * ROW_CHUNKS, LANES), F32), pltpu.VMEM((tm * ROW_CHUNKS, LANES), F32),
                        pltpu.SemaphoreType.DMA(())],
    )
    return pl.pallas_call(
        functools.partial(_combine_kernel, normalize=normalize),
        grid_spec=grid_spec,
        out_shape=jax.ShapeDtypeStruct((n, D_MODEL), F32),
        compiler_params=_cparams(1),
        name="combine",
    )(pos, h, rw, g_final, ys)


def _moe(h, g, w_group, b_group, w_expert, b_expert, w_gate, w_up, w_down, g_final, last):
    n = h.shape[0]
    w_r = jnp.zeros((ROUTER_ROWS, D_MODEL), F32).at[:N_GROUPS].set(w_group.T)
    w_r = w_r.at[N_GROUPS:N_GROUPS + N_EXPERTS].set(w_expert.T)
    b_r = jnp.zeros((ROUTER_ROWS,), F32).at[:N_GROUPS].set(b_group).at[N_GROUPS:N_GROUPS + N_EXPERTS].set(b_expert)
    b_r = jnp.broadcast_to(b_r[:, None], (ROUTER_ROWS, LANES))
    w_hi = w_r.astype(MXU_DTYPE)
    w_lo = (w_r - w_hi.astype(F32)).astype(MXU_DTYPE)
    ri, rw_t, cnt = _router(h, g, w_hi, w_lo, b_r)
    rw = jnp.pad(rw_t[0:2].T, ((0, 0), (0, LANES - 2)))

    counts = cnt[-ROUTER_ROWS:, 0][N_GROUPS:N_GROUPS + N_EXPERTS]
    padded = (counts + EXPERT_TILE - 1) // EXPERT_TILE * EXPERT_TILE
    ends = jnp.cumsum(padded)
    starts = ends - padded
    experts = jnp.arange(N_EXPERTS, dtype=jnp.int32)
    seg_start = jnp.sum(jnp.where(ri[0:2, :, None] == experts, starts, 0), axis=-1)
    pos = (seg_start + ri[2:4]).T.reshape(-1).astype(jnp.int32)
    p_rows = 2 * n + N_EXPERTS * EXPERT_TILE
    tile_start = jnp.arange(p_rows // EXPERT_TILE, dtype=jnp.int32) * EXPERT_TILE
    tile_expert = jnp.minimum(jnp.sum(tile_start[:, None] >= ends[None, :], axis=1), N_EXPERTS - 1).astype(jnp.int32)
    tile_active = (tile_start < ends[-1]).astype(jnp.int32)
    tile_x = jnp.minimum(tile_start, ends[-1] - EXPERT_TILE) // EXPERT_TILE

    tail = jnp.minimum(ends[-1] + jnp.arange(N_EXPERTS + 1, dtype=jnp.int32) * EXPERT_TILE, p_rows)
    xs = _dispatch(pos, jnp.concatenate([starts + counts, tail]).astype(jnp.int32), h, g, p_rows)
    ys = _experts(tile_expert, tile_active, tile_x.astype(jnp.int32), xs, w_gate.astype(MXU_DTYPE),
                  w_up.astype(MXU_DTYPE), w_down.astype(MXU_DTYPE), p_rows)
    return _combine(pos, h, rw, g_final, ys, normalize=last)


def _even_weights(w_in):
    gate_cols = w_in[:, C_GATE:C_GATE + 3 * NSA_HEADS].reshape(D_MODEL, NSA_KV_HEADS, HEADS_PER_GROUP, 3)
    gate_cols = jnp.pad(gate_cols, ((0, 0), (0, 0), (0, 0), (0, SUBLANES - 3)))
    gate_cols = gate_cols.reshape(D_MODEL, NSA_KV_HEADS, HEADS_PER_GROUP * SUBLANES)
    gate_cols = jnp.pad(gate_cols, ((0, 0), (0, 0), (0, LANES - HEADS_PER_GROUP * SUBLANES)))
    return jnp.concatenate([w_in[:, :C_GATE], gate_cols.reshape(D_MODEL, NSA_KV_HEADS * LANES)], axis=1)


def _block_diag_halves(w):
    blocks = w.shape[0] // 2
    out = jnp.zeros((2, blocks * w.shape[1], blocks * w.shape[2]), w.dtype)
    for j in range(2):
        for k in range(blocks):
            out = out.at[j, k * w.shape[1]:(k + 1) * w.shape[1], k * w.shape[2]:(k + 1) * w.shape[2]].set(
                w[j * blocks + k])
    return out


def _even_mixer(h, g, batch, cos_t, sin_t, w_in, w_out, conv_w, conv_b, w_r, b_r, w_i, b_i, lam,
                pos_k, w1_k, w2_k, pos_v, w1_v, w2_v):
    n = h.shape[0]
    t = n // batch
    xl, gg, qn, qr, kcv, ksr, vs, kwr, vw, gates = _even_proj(
        h, g, _even_weights(w_in).astype(MXU_DTYPE), cos_t, sin_t)

    y_lru = _lru(xl, gg, conv_w, conv_b[None, :], _block_diag_halves(w_r).astype(MXU_DTYPE),
                 _block_diag_halves(w_i).astype(MXU_DTYPE), b_r[None, :], b_i[None, :],
                 jax.nn.softplus(-lam)[None, :], batch)

    nchunk = t // CMP_STRIDE
    x16 = kcv.reshape(batch, nchunk, CMP_STRIDE, 2 * NSA_KV_HEADS, HEAD_DIM).transpose(0, 3, 1, 2, 4)
    x16 = x16.reshape(batch, 2 * NSA_KV_HEADS, nchunk, CMP_STRIDE * HEAD_DIM)
    half = CMP_STRIDE * HEAD_DIM
    pos = jnp.stack([pos_k.reshape(2, 1, half), pos_v.reshape(2, 1, half)])
    w1 = jnp.stack([w1_k.reshape(2, half, CMP_HIDDEN), w1_v.reshape(2, half, CMP_HIDDEN)]).astype(MXU_DTYPE)
    w2 = jnp.stack([w2_k, w2_v]).astype(MXU_DTYPE)
    kvc = _compress(x16, pos, w1, w2)

    vct = kvc[:, NSA_KV_HEADS:].transpose(0, 1, 3, 2)
    y_nsa = _nsa(qn, qr, kvc, vct, ksr, vs, kwr, vw, gates, batch)
    w_out = w_out.astype(MXU_DTYPE)
    return _out_proj(h, y_lru, y_nsa, w_out[:LRU_WIDTH], w_out[LRU_WIDTH:])


def _rope_tables(positions):
    inv = ROPE_THETA ** (-jnp.arange(0, 2 * ROT_HALF, 2, dtype=F32) / (2 * ROT_HALF))
    ang = positions.reshape(-1).astype(F32)[:, None] * inv
    cos, sin = jnp.cos(ang), jnp.sin(ang)
    rest = HEAD_DIM - 2 * ROT_HALF
    cos_h = jnp.concatenate([cos, cos, jnp.ones((cos.shape[0], rest), F32)], axis=1)
    sin_h = jnp.concatenate([-sin, sin, jnp.zeros((cos.shape[0], rest), F32)], axis=1)
    reps = LANES // HEAD_DIM
    return jnp.tile(cos_h, (1, reps)), jnp.tile(sin_h, (1, reps))


def kernel(x, mem, positions, norm_mix, norm_xattn, norm_mem, norm_ffn, norm_final, even_w_in, even_w_out, lru_conv_w, lru_conv_b, lru_w_r, lru_b_r, lru_w_i, lru_b_i, lru_lambda, nsa_cmp_pos_k, nsa_cmp_w1_k, nsa_cmp_w2_k, nsa_cmp_pos_v, nsa_cmp_w1_v, nsa_cmp_w2_v, odd_w_in, odd_conv_w, odd_w_out, xa_wq, xa_wk, xa_wv, xa_wo, moe_w_group, moe_b_group, moe_w_expert, moe_b_expert, moe_w_gate, moe_w_up, moe_w_down):
    batch, t, d = x.shape
    n = batch * t
    depth = norm_mix.shape[0]
    cos_t, sin_t = _rope_tables(positions)
    h = x.reshape(n, d)
    mem2 = mem.reshape(-1, d)
    for layer in range(depth):
        g_mix = norm_mix[layer][None, :]
        if layer % 2 == 0:
            e = layer // 2
            h = _even_mixer(h, g_mix, batch, cos_t, sin_t, even_w_in[e], even_w_out[e], lru_conv_w[e],
                            lru_conv_b[e], lru_w_r[e], lru_b_r[e], lru_w_i[e], lru_b_i[e], lru_lambda[e],
                            nsa_cmp_pos_k[e], nsa_cmp_w1_k[e], nsa_cmp_w2_k[e],
                            nsa_cmp_pos_v[e], nsa_cmp_w1_v[e], nsa_cmp_w2_v[e])
        else:
            o = layer // 2
            h = _short_conv(h, g_mix, odd_w_in[o].astype(MXU_DTYPE), odd_conv_w[o],
                            odd_w_out[o].astype(MXU_DTYPE), batch)
        wkv = jnp.concatenate([xa_wk[layer], xa_wv[layer]], axis=1).astype(MXU_DTYPE)
        kv = _mem_kv(mem2, norm_mem[layer][None, :], wkv)
        h = _xattn(h, norm_xattn[layer][None, :], xa_wq[layer].astype(MXU_DTYPE), kv,
                   xa_wo[layer].astype(MXU_DTYPE), batch)
        h = _moe(h, norm_ffn[layer][None, :], moe_w_group[layer], moe_b_group[layer], moe_w_expert[layer],
                 moe_b_expert[layer], moe_w_gate[layer], moe_w_up[layer], moe_w_down[layer],
                 norm_final[None, :], last=layer == depth - 1)
    return h.reshape(batch, t, d)
```

```python
import functools

import jax
import jax.numpy as jnp
import numpy as np
from jax import lax
from jax.experimental import pallas as pl
from jax.experimental.pallas import tpu as pltpu

F32 = jnp.float32
MXU_DTYPE = jnp.bfloat16

D_MODEL = 1024
LRU_WIDTH = 512
LRU_CONV = 4
LRU_C = 8.0
NSA_HEADS = 8
NSA_KV_HEADS = 2
HEADS_PER_GROUP = NSA_HEADS // NSA_KV_HEADS
HEAD_DIM = 64
CMP_STRIDE = 16
CMP_LEN = 32
CMP_HIDDEN = 128
SEL_BLOCK = 64
SEL_TOP = 16
WINDOW = 512
ROT_HALF = 8
ROPE_THETA = 500000.0
SC_CONV = 3
XA_HEADS = 4
XA_HEAD_DIM = 256
N_GROUPS = 4
EXPERTS_PER_GROUP = 4
N_EXPERTS = 16
D_EXPERT = 512
EPS = 1e-6
NEG = -1e30
LOG2E = 1.4426950408889634
FORCE = 1e9

LANES = 128
SUBLANES = 8
VMEM_LIMIT = 56 * 1024 * 1024

C_XL, C_GL, C_Q, C_KCV, C_KS, C_VS, C_KW, C_VW, C_GATE, C_END = (
    0, 512, 1024, 1536, 1792, 1920, 2048, 2176, 2304, 2560)

ROW_TILE = 1024
NSA_TQ = 512
NSA_TK = 512
EXPERT_TILE = 256
DMA_UNROLL = 8
ROW_CHUNKS = D_MODEL // LANES


def _cparams(n_axes):
    return pltpu.CompilerParams(dimension_semantics=("arbitrary",) * n_axes,
                                vmem_limit_bytes=VMEM_LIMIT)


def _mm(a, b):
    return jnp.dot(a, b, preferred_element_type=F32)


def _mm_nt(a, b):
    return lax.dot_general(a, b, (((1,), (1,)), ((), ())), preferred_element_type=F32)


def _rms(x, g):
    return x * lax.rsqrt(jnp.mean(x * x, axis=-1, keepdims=True) + EPS) * g


def _shift_rows(xx, s, rows):
    if s == 0:
        return xx[SUBLANES:SUBLANES + rows]
    return pltpu.roll(xx, s, 0)[SUBLANES:SUBLANES + rows]


def _rope(x, cos_t, sin_t):
    width = x.shape[-1]
    lane = lax.broadcasted_iota(jnp.int32, x.shape, 1) % HEAD_DIM
    partner = jnp.where(lane < ROT_HALF, pltpu.roll(x, width - ROT_HALF, 1), pltpu.roll(x, ROT_HALF, 1))
    return x * cos_t + partner * sin_t


def _even_proj_kernel(h_ref, g_ref, w_ref, cos_ref, sin_ref,
                      xl_ref, gg_ref, qn_ref, qr_ref, kcv_ref, ksr_ref, vs_ref, kwr_ref, vw_ref,
                      gate_ref):
    xn = _rms(h_ref[...], g_ref[...]).astype(MXU_DTYPE)

    def proj(c0, c1):
        return _mm(xn, w_ref[:, c0:c1])

    cos_t = cos_ref[...]
    sin_t = sin_ref[...]
    xl_ref[...] = proj(C_XL, C_GL)
    gg_ref[...] = jax.nn.gelu(proj(C_GL, C_Q))
    q = proj(C_Q, C_KCV) * (HEAD_DIM ** -0.5)
    reps = (C_KCV - C_Q) // LANES
    cos_q = jnp.concatenate([cos_t] * reps, axis=1)
    sin_q = jnp.concatenate([sin_t] * reps, axis=1)
    _store_query_tiles(qn_ref, q, HEAD_DIM)
    _store_query_tiles(qr_ref, _rope(q, cos_q, sin_q) * LOG2E, HEAD_DIM)
    kcv_ref[...] = proj(C_KCV, C_KS)
    ksr_ref[...] = _rope(proj(C_KS, C_VS), cos_t, sin_t).astype(ksr_ref.dtype)
    vs_ref[...] = proj(C_VS, C_KW).T.astype(vs_ref.dtype)
    kwr_ref[...] = _rope(proj(C_KW, C_VW), cos_t, sin_t).astype(kwr_ref.dtype)
    vw_ref[...] = proj(C_VW, C_GATE).T.astype(vw_ref.dtype)
    _store_query_tiles(gate_ref, jax.nn.sigmoid(proj(C_GATE, C_END)), SUBLANES)


def _store_query_tiles(ref, x, per_head):
    group_width = x.shape[1] // NSA_KV_HEADS
    for grp in range(NSA_KV_HEADS):
        for j in range(x.shape[0] // NSA_TQ):
            blk = x[j * NSA_TQ:(j + 1) * NSA_TQ, grp * group_width:(grp + 1) * group_width].T
            ref[grp, j] = jnp.concatenate([blk[hh * per_head:(hh + 1) * per_head, :]
                                           for hh in range(HEADS_PER_GROUP)], axis=1).astype(ref.dtype)


def _even_proj(h, g, w, cos_t, sin_t):
    n = h.shape[0]
    tm = min(ROW_TILE, n)
    rows = HEADS_PER_GROUP * NSA_TQ
    row = lambda c: pl.BlockSpec((tm, c), lambda i: (i, 0))
    col = pl.BlockSpec((LANES, tm), lambda i: (0, i))
    full = lambda a: pl.BlockSpec(a.shape, lambda i: (0,) * a.ndim)
    tiles = lambda c: pl.BlockSpec((NSA_KV_HEADS, tm // NSA_TQ, c, rows), lambda i: (0, i, 0, 0))
    tiles_shape = lambda c, dt: jax.ShapeDtypeStruct((NSA_KV_HEADS, n // NSA_TQ, c, rows), dt)
    flat = lambda c, dt: jax.ShapeDtypeStruct((n, c), dt)
    return pl.pallas_call(
        _even_proj_kernel,
        grid=(n // tm,),
        in_specs=[row(D_MODEL), full(g), full(w), row(LANES), row(LANES)],
        out_specs=[row(512), row(512), tiles(HEAD_DIM), tiles(HEAD_DIM), row(256),
                   row(LANES), col, row(LANES), col, tiles(SUBLANES)],
        out_shape=[flat(512, F32), flat(512, F32), tiles_shape(HEAD_DIM, MXU_DTYPE), tiles_shape(HEAD_DIM, MXU_DTYPE),
                   flat(256, F32), flat(LANES, MXU_DTYPE), jax.ShapeDtypeStruct((LANES, n), MXU_DTYPE),
                   flat(LANES, MXU_DTYPE), jax.ShapeDtypeStruct((LANES, n), MXU_DTYPE),
                   tiles_shape(SUBLANES, F32)],
        compiler_params=_cparams(1),
        name="even_proj",
    )(h, g, w, cos_t, sin_t)


def _lru_kernel(xl_ref, gg_ref, cw_ref, cb_ref, wr_ref, wi_ref, br_ref, bi_ref, sp_ref,
                y_ref, tail_ref, h_ref, a_scr, u_scr):
    tt = xl_ref.shape[0]

    @pl.when(pl.program_id(1) == 0)
    def _():
        tail_ref[...] = jnp.zeros_like(tail_ref)
        h_ref[...] = jnp.zeros_like(h_ref)

    x = xl_ref[...]
    xx = jnp.concatenate([tail_ref[...], x], axis=0)
    tail_ref[...] = x[tt - SUBLANES:tt]
    xc = cb_ref[...] + sum(cw_ref[k:k + 1, :] * _shift_rows(xx, LRU_CONV - 1 - k, tt)
                           for k in range(LRU_CONV))
    xcb = xc.astype(MXU_DTYPE)
    half = LRU_WIDTH // 2
    r_lin = jnp.concatenate([_mm(xcb[:, j * half:(j + 1) * half], wr_ref[j]) for j in range(2)], axis=1)
    i_lin = jnp.concatenate([_mm(xcb[:, j * half:(j + 1) * half], wi_ref[j]) for j in range(2)], axis=1)
    r = jax.nn.sigmoid(r_lin + br_ref[...])
    i = jax.nn.sigmoid(i_lin + bi_ref[...])
    log_a = -LRU_C * r * sp_ref[...]
    a = jnp.exp(log_a)
    u = jnp.sqrt(jnp.tanh(-log_a) * (1.0 + a * a)) * (i * xc)

    r8 = lax.broadcasted_iota(jnp.int32, a.shape, 0) % SUBLANES
    for s in (1, 2, 4):
        keep = r8 >= s
        u = jnp.where(keep, a * pltpu.roll(u, s, 0) + u, u)
        a = jnp.where(keep, a * pltpu.roll(a, s, 0), a)
    a_scr[...] = a
    u_scr[...] = u

    def body(gidx, h):
        r0 = pl.multiple_of(gidx * SUBLANES, SUBLANES)
        out = a_scr[pl.ds(r0, SUBLANES), :] * h + u_scr[pl.ds(r0, SUBLANES), :]
        u_scr[pl.ds(r0, SUBLANES), :] = out
        return out[SUBLANES - 1:SUBLANES, :]

    h_ref[...] = lax.fori_loop(0, tt // SUBLANES, body, h_ref[...])
    y_ref[...] = (u_scr[...] * gg_ref[...]).astype(y_ref.dtype)


def _lru(xl, gg, cw, cb, wr, wi, br, bi, sp, batch):
    n = xl.shape[0]
    t = n // batch
    tt = min(ROW_TILE, t)
    nt = t // tt
    row = pl.BlockSpec((tt, LRU_WIDTH), lambda b, i: (b * nt + i, 0))
    full = lambda a: pl.BlockSpec(a.shape, lambda b, i: (0,) * a.ndim)
    return pl.pallas_call(
        _lru_kernel,
        grid=(batch, nt),
        in_specs=[row, row] + [full(a) for a in (cw, cb, wr, wi, br, bi, sp)],
        out_specs=row,
        out_shape=jax.ShapeDtypeStruct((n, LRU_WIDTH), MXU_DTYPE),
        scratch_shapes=[pltpu.VMEM((SUBLANES, LRU_WIDTH), F32), pltpu.VMEM((1, LRU_WIDTH), F32),
                        pltpu.VMEM((tt, LRU_WIDTH), F32), pltpu.VMEM((tt, LRU_WIDTH), F32)],
        compiler_params=_cparams(2),
        name="lru",
    )(xl, gg, cw, cb, wr, wi, br, bi, sp)


def _compress_kernel(x_ref, pos_ref, w1_ref, w2_ref, o_ref):
    x = x_ref[0, 0]
    nchunk = x.shape[0]
    lo = (x + pos_ref[0, 0]).astype(MXU_DTYPE)
    hi = (x + pos_ref[0, 1]).astype(MXU_DTYPE)
    p_lo = _mm(lo, w1_ref[0, 0])
    p_hi = _mm(hi, w1_ref[0, 1])
    a = p_lo + pltpu.roll(p_hi, nchunk - 1, 0)
    o_ref[0, 0] = _mm(jax.nn.gelu(a).astype(MXU_DTYPE), w2_ref[0]).astype(o_ref.dtype)


def _compress(x16, pos, w1, w2):
    b, _, nchunk, width = x16.shape
    return pl.pallas_call(
        _compress_kernel,
        grid=(b, 2 * NSA_KV_HEADS),
        in_specs=[pl.BlockSpec((1, 1, nchunk, width), lambda i, j: (i, j, 0, 0)),
                  pl.BlockSpec((1, 2, 1, width), lambda i, j: (j // NSA_KV_HEADS, 0, 0, 0)),
                  pl.BlockSpec((1, 2, width, CMP_HIDDEN), lambda i, j: (j // NSA_KV_HEADS, 0, 0, 0)),
                  pl.BlockSpec((1, CMP_HIDDEN, HEAD_DIM), lambda i, j: (j // NSA_KV_HEADS, 0, 0))],
        out_specs=pl.BlockSpec((1, 1, nchunk, HEAD_DIM), lambda i, j: (i, j, 0, 0)),
        out_shape=jax.ShapeDtypeStruct((b, 2 * NSA_KV_HEADS, nchunk, HEAD_DIM), MXU_DTYPE),
        compiler_params=_cparams(2),
        name="compress",
    )(x16, pos, w1, w2)


def _nsa_kernel(qn_ref, qr_ref, kc_ref, vct_ref, ks_ref, vst_ref, kw_ref, vwt_ref, gate_ref,
                y_ref, chosen_scr, acc_scr, accw_scr, sc_scr):
    rows = qn_ref.shape[3]
    tq = rows // HEADS_PER_GROUP
    t = ks_ref.shape[0]
    nc = kc_ref.shape[2]
    nsel = t // SEL_BLOCK
    n_top = min(SEL_TOP, nsel)
    tk = NSA_TK
    t0 = pl.program_id(2) * tq
    qn = qn_ref[0, 0]
    grp = pl.program_id(1)
    qr64 = qr_ref[0, 0]
    qr = jnp.concatenate([jnp.where(grp == gi, qr64, jnp.zeros_like(qr64)) for gi in range(NSA_KV_HEADS)], axis=0)

    s = _mm(kc_ref[0, 0], qn)
    tq_row = t0 + lax.broadcasted_iota(jnp.int32, (nc, rows), 1) % tq
    cmp_end = lax.broadcasted_iota(jnp.int32, (nc, rows), 0) * CMP_STRIDE + (CMP_LEN - 1)
    valid = cmp_end <= tq_row
    s = jnp.where(valid, s, NEG)
    e = jnp.where(valid, jnp.exp(s - jnp.max(s, axis=0, keepdims=True)), 0.0)
    den = jnp.sum(e, axis=0, keepdims=True)
    p = e * (1.0 / jnp.where(den > 0.0, den, 1.0))
    o_cmp = _mm(vct_ref[0, 0], p.astype(MXU_DTYPE))

    psum = p[:, 0:tq]
    for hh in range(1, HEADS_PER_GROUP):
        psum = psum + p[:, hh * tq:(hh + 1) * tq]
    p_hi = psum.astype(MXU_DTYPE)
    p_lo = (psum - p_hi.astype(F32)).astype(MXU_DTYPE)
    cj = lax.broadcasted_iota(jnp.int32, (nsel, nc), 0)
    cn = lax.broadcasted_iota(jnp.int32, (nsel, nc), 1)
    ratio = SEL_BLOCK // CMP_STRIDE
    cover = jnp.where((cn >= ratio * cj - (CMP_LEN // CMP_STRIDE - 1)) & (cn <= ratio * cj + ratio - 1)
                      & (cn < nc - 1), 1.0, 0.0).astype(MXU_DTYPE)
    imp = _mm(cover, p_hi) + _mm(cover, p_lo)

    blk = lax.broadcasted_iota(jnp.int32, (nsel, tq), 0)
    tq_col = t0 + lax.broadcasted_iota(jnp.int32, (nsel, tq), 1)
    cur = tq_col // SEL_BLOCK
    forced = (blk == 0) | (blk == cur) | (blk == cur - 1)
    causal = blk * SEL_BLOCK <= tq_col
    score0 = jnp.where(causal, jnp.where(forced, FORCE, imp), NEG)
    blk_f = blk.astype(F32)

    def pick_one(_, carry):
        score, chosen = carry
        best = jnp.max(score, axis=0, keepdims=True)
        first = jnp.min(jnp.where(score == best, blk_f, float(nsel)), axis=0, keepdims=True)
        hit = blk_f == first
        return jnp.where(hit, -jnp.inf, score), jnp.where(hit, 1.0, chosen)

    _, chosen = lax.fori_loop(0, n_top, pick_one, (score0, jnp.zeros((nsel, tq), F32)))
    chosen_scr[...] = chosen

    def scores(k_ref, kt, bias):
        k0 = pl.multiple_of(kt * tk, tk)
        return _mm(k_ref[pl.ds(k0, tk), :], qr) + jnp.concatenate([bias] * HEADS_PER_GROUP, axis=1)

    def soft(sc, m_old, l_old):
        m_new = jnp.maximum(m_old, jnp.max(sc, axis=0, keepdims=True))
        alpha = jnp.exp2(m_old - m_new)
        pe = jnp.exp2(sc - m_new)
        return m_new, alpha * l_old + jnp.sum(pe, axis=0, keepdims=True), alpha, pe.astype(MXU_DTYPE)

    def accumulate(vt_ref, acc_ref, kt, alpha, pe):
        k0 = pl.multiple_of(kt * tk, tk)
        acc_ref[...] = alpha * acc_ref[...] + _mm(vt_ref[:, pl.ds(k0, tk)], pe)

    expand = jnp.where(lax.broadcasted_iota(jnp.int32, (tk, SUBLANES), 0) // SEL_BLOCK
                       == lax.broadcasted_iota(jnp.int32, (tk, SUBLANES), 1), 1.0, 0.0)

    def key_minus_query(kt):
        return (kt * tk - t0 + lax.broadcasted_iota(jnp.int32, (tk, tq), 0)
                - lax.broadcasted_iota(jnp.int32, (tk, tq), 1))

    def sel_bias(kt, causal=True):
        grp = pl.multiple_of(kt * SUBLANES, SUBLANES)
        bias = _mm(expand, (chosen_scr[pl.ds(grp, SUBLANES), :] - 1.0) * (-NEG))
        return jnp.where(key_minus_query(kt) <= 0, bias, NEG) if causal else bias

    def win_bias(kt, diagonal):
        d = key_minus_query(kt)
        inside = (d > -WINDOW) & (d <= 0) if diagonal else d > -WINDOW
        return jnp.where(inside, 0.0, NEG)

    def sel_only(kt, carry):
        m_s, l_s, m_w, l_w = carry
        sc_next = scores(ks_ref, kt + 1, sel_bias(kt + 1, causal=False))
        m_s, l_s, alpha, pe = soft(sc_scr[...], m_s, l_s)
        accumulate(vst_ref, acc_scr, kt, alpha, pe)
        sc_scr[...] = sc_next
        return m_s, l_s, m_w, l_w

    def both(kt, carry, last=False):
        m_s, l_s, m_w, l_w = carry
        sc_w = scores(kw_ref, kt, win_bias(kt, last))
        if not last:
            sc_next = scores(ks_ref, kt + 1, sel_bias(kt + 1))
        m_s, l_s, alpha_s, pe_s = soft(sc_scr[...], m_s, l_s)
        m_w, l_w, alpha_w, pe_w = soft(sc_w, m_w, l_w)
        accumulate(vst_ref, acc_scr, kt, alpha_s, pe_s)
        accumulate(vwt_ref, accw_scr, kt, alpha_w, pe_w)
        if not last:
            sc_scr[...] = sc_next
        return m_s, l_s, m_w, l_w

    kt_last = (t0 + tq - 1) // tk
    win_lo = jnp.maximum(t0 - (WINDOW - 1), 0) // tk
    acc_scr[...] = jnp.zeros_like(acc_scr)
    accw_scr[...] = jnp.zeros_like(accw_scr)
    lowest = jnp.full((1, rows), NEG, F32)
    zero = jnp.zeros((1, rows), F32)
    sc_scr[...] = scores(ks_ref, 0, sel_bias(0))
    carry = lax.fori_loop(0, win_lo, sel_only, (lowest, zero, lowest, zero))
    carry = lax.fori_loop(win_lo, kt_last, both, carry)
    _, l_s, _, l_w = both(kt_last, carry, last=True)
    o_sel = acc_scr[...] * (1.0 / l_s)
    o_win = accw_scr[...] * (1.0 / l_w)

    gate = gate_ref[0, 0]
    y = gate[0:1] * o_cmp + gate[1:2] * o_sel + gate[2:3] * o_win
    y = jnp.concatenate([y[:, hh * tq:(hh + 1) * tq] for hh in range(HEADS_PER_GROUP)], axis=0)
    y_ref[...] = y.T.astype(y_ref.dtype)


def _nsa(qn_t, qr_t, kvc, vct, ks, vst, kw, vwt, gates_t, batch):
    _, n_tiles, _, rows = qn_t.shape
    tq = rows // HEADS_PER_GROUP
    nq = n_tiles // batch
    t = nq * tq
    nc = kvc.shape[2]
    assert NSA_TK == SEL_BLOCK * SUBLANES and t % NSA_TK == 0
    tile = lambda c: pl.BlockSpec((1, 1, c, rows), lambda i, g, j: (g, i * nq + j, 0, 0))
    k_spec = pl.BlockSpec((t, LANES), lambda i, g, j: (i, 0))
    vt_spec = pl.BlockSpec((HEAD_DIM, t), lambda i, g, j: (g, i))
    return pl.pallas_call(
        _nsa_kernel,
        grid=(batch, NSA_KV_HEADS, nq),
        in_specs=[tile(HEAD_DIM), tile(HEAD_DIM),
                  pl.BlockSpec((1, 1, nc, HEAD_DIM), lambda i, g, j: (i, g, 0, 0)),
                  pl.BlockSpec((1, 1, HEAD_DIM, nc), lambda i, g, j: (i, g, 0, 0)),
                  k_spec, vt_spec, k_spec, vt_spec, tile(SUBLANES)],
        out_specs=pl.BlockSpec((tq, HEADS_PER_GROUP * HEAD_DIM), lambda i, g, j: (i * nq + j, g)),
        out_shape=jax.ShapeDtypeStruct((batch * t, NSA_HEADS * HEAD_DIM), MXU_DTYPE),
        scratch_shapes=[pltpu.VMEM((t // SEL_BLOCK, tq), F32),
                        pltpu.VMEM((HEAD_DIM, rows), F32), pltpu.VMEM((HEAD_DIM, rows), F32),
                        pltpu.VMEM((NSA_TK, rows), F32)],
        compiler_params=_cparams(3),
        name="nsa",
    )(qn_t, qr_t, kvc, vct, ks, vst, kw, vwt, gates_t)


def _out_proj_kernel(h_ref, a_ref, b_ref, wa_ref, wb_ref, o_ref):
    o_ref[...] = h_ref[...] + _mm(a_ref[...], wa_ref[...]) + _mm(b_ref[...], wb_ref[...])


def _out_proj(h, a, b, wa, wb):
    n = h.shape[0]
    tm = min(ROW_TILE, n)
    full = lambda x: pl.BlockSpec(x.shape, lambda i: (0,) * x.ndim)
    return pl.pallas_call(
        _out_proj_kernel,
        grid=(n // tm,),
        in_specs=[pl.BlockSpec((tm, D_MODEL), lambda i: (i, 0)),
                  pl.BlockSpec((tm, a.shape[1]), lambda i: (i, 0)),
                  pl.BlockSpec((tm, b.shape[1]), lambda i: (i, 0)), full(wa), full(wb)],
        out_specs=pl.BlockSpec((tm, D_MODEL), lambda i: (i, 0)),
        out_shape=jax.ShapeDtypeStruct((n, D_MODEL), F32),
        compiler_params=_cparams(1),
        name="out_proj",
    )(h, a, b, wa, wb)


def _short_conv_kernel(h_ref, g_ref, win_ref, cw_ref, wout_ref, o_ref, tail_ref):
    tt = h_ref.shape[0]

    @pl.when(pl.program_id(1) == 0)
    def _():
        tail_ref[...] = jnp.zeros_like(tail_ref)

    h = h_ref[...]
    xn = _rms(h, g_ref[...]).astype(MXU_DTYPE)
    b_g = _mm(xn, win_ref[:, 0:D_MODEL])
    cv = _mm(xn, win_ref[:, D_MODEL:2 * D_MODEL]) * _mm(xn, win_ref[:, 2 * D_MODEL:3 * D_MODEL])
    xx = jnp.concatenate([tail_ref[...], cv], axis=0)
    tail_ref[...] = cv[tt - SUBLANES:tt]
    conv = sum(cw_ref[k:k + 1, :] * _shift_rows(xx, SC_CONV - 1 - k, tt) for k in range(SC_CONV))
    o_ref[...] = h + _mm((b_g * conv).astype(MXU_DTYPE), wout_ref[...])


def _short_conv(h, g, w_in, cw, w_out, batch):
    n = h.shape[0]
    t = n // batch
    tt = min(ROW_TILE, t)
    nt = t // tt
    row = pl.BlockSpec((tt, D_MODEL), lambda b, i: (b * nt + i, 0))
    full = lambda a: pl.BlockSpec(a.shape, lambda b, i: (0,) * a.ndim)
    return pl.pallas_call(
        _short_conv_kernel,
        grid=(batch, nt),
        in_specs=[row, full(g), full(w_in), full(cw), full(w_out)],
        out_specs=row,
        out_shape=jax.ShapeDtypeStruct((n, D_MODEL), F32),
        scratch_shapes=[pltpu.VMEM((SUBLANES, D_MODEL), F32)],
        compiler_params=_cparams(2),
        name="short_conv",
    )(h, g, w_in, cw, w_out)


def _mem_kv_kernel(m_ref, g_ref, w_ref, o_ref):
    o_ref[...] = _mm(_rms(m_ref[...], g_ref[...]).astype(MXU_DTYPE), w_ref[...]).astype(o_ref.dtype)


def _mem_kv(mem, g, wkv):
    n = mem.shape[0]
    tm = min(ROW_TILE, n)
    tn = 1024
    return pl.pallas_call(
        _mem_kv_kernel,
        grid=(n // tm, wkv.shape[1] // tn),
        in_specs=[pl.BlockSpec((tm, D_MODEL), lambda i, j: (i, 0)),
                  pl.BlockSpec(g.shape, lambda i, j: (0, 0)),
                  pl.BlockSpec((D_MODEL, tn), lambda i, j: (0, j))],
        out_specs=pl.BlockSpec((tm, tn), lambda i, j: (i, j)),
        out_shape=jax.ShapeDtypeStruct((n, wkv.shape[1]), MXU_DTYPE),
        compiler_params=_cparams(2),
        name="mem_kv",
    )(mem, g, wkv)


def _xattn_kernel(h_ref, g_ref, wq_ref, kv_ref, wo_ref, o_ref):
    h = h_ref[...]
    xn = _rms(h, g_ref[...]).astype(MXU_DTYPE)
    q = (_mm(xn, wq_ref[...]) * (XA_HEAD_DIM ** -0.5)).astype(MXU_DTYPE)
    width = XA_HEADS * XA_HEAD_DIM
    outs = []
    for hd in range(XA_HEADS):
        sl = slice(hd * XA_HEAD_DIM, (hd + 1) * XA_HEAD_DIM)
        s = _mm_nt(q[:, sl], kv_ref[:, sl])
        e = jnp.exp(s - jnp.max(s, axis=-1, keepdims=True))
        p = e / jnp.sum(e, axis=-1, keepdims=True)
        outs.append(_mm(p.astype(MXU_DTYPE), kv_ref[:, width + hd * XA_HEAD_DIM:width + (hd + 1) * XA_HEAD_DIM]))
    o = jnp.concatenate(outs, axis=1).astype(MXU_DTYPE)
    o_ref[...] = h + _mm(o, wo_ref[...])


def _xattn(h, g, wq, kv, wo, batch):
    n = h.shape[0]
    t = n // batch
    tm = min(ROW_TILE, t)
    nt = t // tm
    mlen = kv.shape[0] // batch
    full = lambda a: pl.BlockSpec(a.shape, lambda b, i: (0,) * a.ndim)
    row = pl.BlockSpec((tm, D_MODEL), lambda b, i: (b * nt + i, 0))
    return pl.pallas_call(
        _xattn_kernel,
        grid=(batch, nt),
        in_specs=[row, full(g), full(wq), pl.BlockSpec((mlen, kv.shape[1]), lambda b, i: (b, 0)), full(wo)],
        out_specs=row,
        out_shape=jax.ShapeDtypeStruct((n, D_MODEL), F32),
        compiler_params=_cparams(2),
        name="xattn",
    )(h, g, wq, kv, wo)


ROUTER_ROWS = 32


def _router_kernel(h_ref, g_ref, whi_ref, wlo_ref, b_ref, ri_ref, rw_ref, cnt_ref, carry_ref):
    tm = h_ref.shape[0]

    @pl.when(pl.program_id(0) == 0)
    def _():
        carry_ref[...] = jnp.zeros_like(carry_ref)

    xn = _rms(h_ref[...], g_ref[...])
    x_hi = xn.astype(MXU_DTYPE)
    x_lo = (xn - x_hi.astype(F32)).astype(MXU_DTYPE)
    logits = (_mm_nt(whi_ref[...], x_hi) + (_mm_nt(whi_ref[...], x_lo) + _mm_nt(wlo_ref[...], x_hi))
              + b_ref[:, 0:1])

    row = lax.broadcasted_iota(jnp.int32, logits.shape, 0)
    row_f = row.astype(F32)
    none = float(ROUTER_ROWS)
    is_g = row < N_GROUPS
    g_max = jnp.max(jnp.where(is_g, logits, -jnp.inf), axis=0, keepdims=True)
    g_sum = jnp.sum(jnp.where(is_g, jnp.exp(logits - g_max), 0.0), axis=0, keepdims=True)
    g_top = 1.0 / g_sum
    g_idx = jnp.min(jnp.where(is_g & (logits == g_max), row_f, none), axis=0, keepdims=True)
    first = N_GROUPS + EXPERTS_PER_GROUP * g_idx
    in_g = (row_f >= first) & (row_f < first + EXPERTS_PER_GROUP)
    e1 = jnp.max(jnp.where(in_g, logits, -jnp.inf), axis=0, keepdims=True)
    i1 = jnp.min(jnp.where(in_g & (logits == e1), row_f, none), axis=0, keepdims=True)
    rest = in_g & (row_f != i1)
    e2 = jnp.max(jnp.where(rest, logits, -jnp.inf), axis=0, keepdims=True)
    i2 = jnp.min(jnp.where(rest & (logits == e2), row_f, none), axis=0, keepdims=True)
    ratio = jnp.exp(e2 - e1)
    w1 = g_top / (1.0 + ratio)
    w2 = g_top * ratio / (1.0 + ratio)

    onehot = jnp.where((row_f == i1) | (row_f == i2), 1.0, 0.0)
    tri = jnp.where(lax.broadcasted_iota(jnp.int32, (tm, tm), 0) <= lax.broadcasted_iota(jnp.int32, (tm, tm), 1),
                    1.0, 0.0).astype(MXU_DTYPE)
    incl = _mm(onehot.astype(MXU_DTYPE), tri)
    before = incl - onehot + carry_ref[:, 0:1]
    carry_ref[...] = carry_ref[...] + incl[:, tm - 1:tm]
    rank1 = jnp.sum(jnp.where(row_f == i1, before, 0.0), axis=0, keepdims=True)
    rank2 = jnp.sum(jnp.where(row_f == i2, before, 0.0), axis=0, keepdims=True)

    out_row = lax.broadcasted_iota(jnp.int32, ri_ref.shape, 0)
    ri = jnp.where(out_row == 0, i1 - N_GROUPS,
                   jnp.where(out_row == 1, i2 - N_GROUPS, jnp.where(out_row == 2, rank1, jnp.where(out_row == 3, rank2, 0.0))))
    ri_ref[...] = ri.astype(jnp.int32)
    rw_ref[...] = jnp.where(out_row == 0, w1, jnp.where(out_row == 1, w2, 0.0))
    cnt_ref[...] = carry_ref[...].astype(jnp.int32)


def _router(h, g, w_hi, w_lo, bias):
    n = h.shape[0]
    tm = min(ROW_TILE, n)
    full = lambda a: pl.BlockSpec(a.shape, lambda i: (0,) * a.ndim)
    return pl.pallas_call(
        _router_kernel,
        grid=(n // tm,),
        in_specs=[pl.BlockSpec((tm, D_MODEL), lambda i: (i, 0)), full(g), full(w_hi), full(w_lo), full(bias)],
        out_specs=[pl.BlockSpec((SUBLANES, tm), lambda i: (0, i)), pl.BlockSpec((SUBLANES, tm), lambda i: (0, i)),
                   pl.BlockSpec((ROUTER_ROWS, LANES), lambda i: (i, 0))],
        out_shape=[jax.ShapeDtypeStruct((SUBLANES, n), jnp.int32), jax.ShapeDtypeStruct((SUBLANES, n), F32),
                   jax.ShapeDtypeStruct((n // tm * ROUTER_ROWS, LANES), jnp.int32)],
        scratch_shapes=[pltpu.VMEM((ROUTER_ROWS, LANES), F32)],
        compiler_params=_cparams(1),
        name="router",
    )(h, g, w_hi, w_lo, bias)


def _row_copy(src_ref, src_row, dst_ref, dst_row, sem):
    src = src_ref.at[pl.ds(pl.multiple_of(src_row * ROW_CHUNKS, ROW_CHUNKS), ROW_CHUNKS)]
    dst = dst_ref.at[pl.ds(pl.multiple_of(dst_row * ROW_CHUNKS, ROW_CHUNKS), ROW_CHUNKS)]
    return pltpu.make_async_copy(src, dst, sem)


def _to_token_tiles(ref, x):
    rows = x.shape[0]
    for c in range(ROW_CHUNKS):
        ref[pl.ds(c, rows, stride=ROW_CHUNKS), :] = x[:, c * LANES:(c + 1) * LANES]


def _from_token_tiles(ref):
    rows = ref.shape[0] // ROW_CHUNKS
    return jnp.concatenate([ref[pl.ds(c, rows, stride=ROW_CHUNKS), :] for c in range(ROW_CHUNKS)], axis=1)


def _dispatch_kernel(pos_ref, fill_ref, h_ref, g_ref, xs_ref, xn_scr, zero_scr, sem, fill_sem):
    tm = h_ref.shape[0]

    @pl.when(pl.program_id(0) == 0)
    def _():
        zero_scr[...] = jnp.zeros_like(zero_scr)
        for e in range(fill_ref.shape[0]):
            first = pl.multiple_of(fill_ref[e] * ROW_CHUNKS, ROW_CHUNKS)
            fill = pltpu.make_async_copy(zero_scr, xs_ref.at[pl.ds(first, EXPERT_TILE * ROW_CHUNKS)], fill_sem)
            fill.start()
            fill.wait()

    base = pl.program_id(0) * (2 * tm)
    _to_token_tiles(xn_scr, _rms(h_ref[...], g_ref[...]))

    def issue(r, _):
        _row_copy(xn_scr, r, xs_ref, pos_ref[base + 2 * r], sem).start(priority=0)
        _row_copy(xn_scr, r, xs_ref, pos_ref[base + 2 * r + 1], sem).start(priority=1)
        return 0

    lax.fori_loop(0, tm, issue, 0, unroll=DMA_UNROLL)
    for _ in range(2):
        pltpu.make_async_copy(xn_scr, xs_ref.at[pl.ds(0, tm * ROW_CHUNKS)], sem).wait()


def _dispatch(pos, fill_start, h, g, p_rows):
    n = h.shape[0]
    tm = min(ROW_TILE, n)
    grid_spec = pltpu.PrefetchScalarGridSpec(
        num_scalar_prefetch=2,
        grid=(n // tm,),
        in_specs=[pl.BlockSpec((tm, D_MODEL), lambda i, pos, fill: (i, 0)),
                  pl.BlockSpec(g.shape, lambda i, pos, fill: (0, 0))],
        out_specs=pl.BlockSpec(memory_space=pl.ANY),
        scratch_shapes=[pltpu.VMEM((tm * ROW_CHUNKS, LANES), F32),
                        pltpu.VMEM((EXPERT_TILE * ROW_CHUNKS, LANES), F32),
                        pltpu.SemaphoreType.DMA(()), pltpu.SemaphoreType.DMA(())],
    )
    return pl.pallas_call(
        _dispatch_kernel,
        grid_spec=grid_spec,
        out_shape=jax.ShapeDtypeStruct(((p_rows + EXPERT_TILE) * ROW_CHUNKS, LANES), F32),
        compiler_params=_cparams(1),
        name="dispatch",
    )(pos, fill_start, h, g)


def _expert_kernel(te_ref, ta_ref, tx_ref, x_ref, wg_ref, wu_ref, wd_ref, y_ref):
    j = pl.program_id(0)

    @pl.when(ta_ref[j] == 1)
    def _():
        x = _from_token_tiles(x_ref).astype(MXU_DTYPE)
        hid = jax.nn.silu(_mm(x, wg_ref[0])) * _mm(x, wu_ref[0])
        _to_token_tiles(y_ref, _mm(hid.astype(MXU_DTYPE), wd_ref[0]))

    @pl.when(ta_ref[j] == 0)
    def _():
        y_ref[...] = jnp.zeros_like(y_ref)


def _experts(tile_expert, tile_active, tile_x, xs, wg, wu, wd, p_rows):
    grid_spec = pltpu.PrefetchScalarGridSpec(
        num_scalar_prefetch=3,
        grid=(p_rows // EXPERT_TILE,),
        in_specs=[pl.BlockSpec((EXPERT_TILE * ROW_CHUNKS, LANES), lambda j, te, ta, tx: (tx[j], 0)),
                  pl.BlockSpec((1, D_MODEL, D_EXPERT), lambda j, te, ta, tx: (te[j], 0, 0)),
                  pl.BlockSpec((1, D_MODEL, D_EXPERT), lambda j, te, ta, tx: (te[j], 0, 0)),
                  pl.BlockSpec((1, D_EXPERT, D_MODEL), lambda j, te, ta, tx: (te[j], 0, 0))],
        out_specs=pl.BlockSpec((EXPERT_TILE * ROW_CHUNKS, LANES), lambda j, te, ta, tx: (j, 0)),
    )
    return pl.pallas_call(
        _expert_kernel,
        grid_spec=grid_spec,
        out_shape=jax.ShapeDtypeStruct((p_rows * ROW_CHUNKS, LANES), F32),
        compiler_params=_cparams(1),
        name="experts",
    )(tile_expert, tile_active, tile_x, xs, wg, wu, wd)


def _combine_kernel(pos_ref, h_ref, rw_ref, g_ref, ys_ref, o_ref, y1_scr, y2_scr, sem, *, normalize):
    tm = h_ref.shape[0]
    base = pl.program_id(0) * (2 * tm)

    def issue(r, _):
        _row_copy(ys_ref, pos_ref[base + 2 * r], y1_scr, r, sem).start(priority=0)
        _row_copy(ys_ref, pos_ref[base + 2 * r + 1], y2_scr, r, sem).start(priority=1)
        return 0

    lax.fori_loop(0, tm, issue, 0, unroll=DMA_UNROLL)
    for buf in (y1_scr, y2_scr):
        pltpu.make_async_copy(ys_ref.at[pl.ds(0, tm * ROW_CHUNKS)], buf, sem).wait()
    rw = rw_ref[...]
    out = h_ref[...] + (rw[:, 0:1] * _from_token_tiles(y1_scr) + rw[:, 1:2] * _from_token_tiles(y2_scr))
    o_ref[...] = _rms(out, g_ref[...]) if normalize else out


def _combine(pos, h, rw, g_final, ys, normalize):
    n = h.shape[0]
    tm = min(ROW_TILE, n)
    grid_spec = pltpu.PrefetchScalarGridSpec(
        num_scalar_prefetch=1,
        grid=(n // tm,),
        in_specs=[pl.BlockSpec((tm, D_MODEL), lambda i, pos: (i, 0)),
                  pl.BlockSpec((tm, LANES), lambda i, pos: (i, 0)),
                  pl.BlockSpec(g_final.shape, lambda i, pos: (0, 0)),
                  pl.BlockSpec(memory_space=pl.ANY)],
        out_specs=pl.BlockSpec((tm, D_MODEL), lambda i, pos: (i, 0)),
        scratch_shapes=[pltpu.VMEM((tm * ROW_CHUNKS, LANES), F32), pltpu.VMEM((tm * ROW_CHUNKS, LANES), F32),
                        pltpu.SemaphoreType.DMA(())],
    )
    return pl.pallas_call(
        functools.partial(_combine_kernel, normalize=normalize),
        grid_spec=grid_spec,
        out_shape=jax.ShapeDtypeStruct((n, D_MODEL), F32),
        compiler_params=_cparams(1),
        name="combine",
    )(pos, h, rw, g_final, ys)


def _moe(h, g, w_group, b_group, w_expert, b_expert, w_gate, w_up, w_down, g_final, last):
    n = h.shape[0]
    w_r = jnp.zeros((ROUTER_ROWS, D_MODEL), F32).at[:N_GROUPS].set(w_group.T)
    w_r = w_r.at[N_GROUPS:N_GROUPS + N_EXPERTS].set(w_expert.T)
    b_r = jnp.zeros((ROUTER_ROWS,), F32).at[:N_GROUPS].set(b_group).at[N_GROUPS:N_GROUPS + N_EXPERTS].set(b_expert)
    b_r = jnp.broadcast_to(b_r[:, None], (ROUTER_ROWS, LANES))
    w_hi = w_r.astype(MXU_DTYPE)
    w_lo = (w_r - w_hi.astype(F32)).astype(MXU_DTYPE)
    ri, rw_t, cnt = _router(h, g, w_hi, w_lo, b_r)
    rw = jnp.pad(rw_t[0:2].T, ((0, 0), (0, LANES - 2)))

    counts = cnt[-ROUTER_ROWS:, 0][N_GROUPS:N_GROUPS + N_EXPERTS]
    padded = (counts + EXPERT_TILE - 1) // EXPERT_TILE * EXPERT_TILE
    ends = jnp.cumsum(padded)
    starts = ends - padded
    experts = jnp.arange(N_EXPERTS, dtype=jnp.int32)
    seg_start = jnp.sum(jnp.where(ri[0:2, :, None] == experts, starts, 0), axis=-1)
    pos = (seg_start + ri[2:4]).T.reshape(-1).astype(jnp.int32)
    p_rows = 2 * n + N_EXPERTS * EXPERT_TILE
    tile_start = jnp.arange(p_rows // EXPERT_TILE, dtype=jnp.int32) * EXPERT_TILE
    tile_expert = jnp.minimum(jnp.sum(tile_start[:, None] >= ends[None, :], axis=1), N_EXPERTS - 1).astype(jnp.int32)
    tile_active = (tile_start < ends[-1]).astype(jnp.int32)
    tile_x = jnp.minimum(tile_start, ends[-1] - EXPERT_TILE) // EXPERT_TILE

    tail = jnp.minimum(ends[-1] + jnp.arange(N_EXPERTS + 1, dtype=jnp.int32) * EXPERT_TILE, p_rows)
    xs = _dispatch(pos, jnp.concatenate([starts + counts, tail]).astype(jnp.int32), h, g, p_rows)
    ys = _experts(tile_expert, tile_active, tile_x.astype(jnp.int32), xs, w_gate.astype(MXU_DTYPE),
                  w_up.astype(MXU_DTYPE), w_down.astype(MXU_DTYPE), p_rows)
    return _combine(pos, h, rw, g_final, ys, normalize=last)


def _even_weights(w_in):
    gate_cols = w_in[:, C_GATE:C_GATE + 3 * NSA_HEADS].reshape(D_MODEL, NSA_KV_HEADS, HEADS_PER_GROUP, 3)
    gate_cols = jnp.pad(gate_cols, ((0, 0), (0, 0), (0, 0), (0, SUBLANES - 3)))
    gate_cols = gate_cols.reshape(D_MODEL, NSA_KV_HEADS, HEADS_PER_GROUP * SUBLANES)
    gate_cols = jnp.pad(gate_cols, ((0, 0), (0, 0), (0, LANES - HEADS_PER_GROUP * SUBLANES)))
    return jnp.concatenate([w_in[:, :C_GATE], gate_cols.reshape(D_MODEL, NSA_KV_HEADS * LANES)], axis=1)


def _block_diag_halves(w):
    blocks = w.shape[0] // 2
    out = jnp.zeros((2, blocks * w.shape[1], blocks * w.shape[2]), w.dtype)
    for j in range(2):
        for k in range(blocks):
            out = out.at[j, k * w.shape[1]:(k + 1) * w.shape[1], k * w.shape[2]:(k + 1) * w.shape[2]].set(
                w[j * blocks + k])
    return out


def _even_mixer(h, g, batch, cos_t, sin_t, w_in, w_out, conv_w, conv_b, w_r, b_r, w_i, b_i, lam,
                pos_k, w1_k, w2_k, pos_v, w1_v, w2_v):
    n = h.shape[0]
    t = n // batch
    xl, gg, qn, qr, kcv, ksr, vs, kwr, vw, gates = _even_proj(
        h, g, _even_weights(w_in).astype(MXU_DTYPE), cos_t, sin_t)

    y_lru = _lru(xl, gg, conv_w, conv_b[None, :], _block_diag_halves(w_r).astype(MXU_DTYPE),
                 _block_diag_halves(w_i).astype(MXU_DTYPE), b_r[None, :], b_i[None, :],
                 jax.nn.softplus(-lam)[None, :], batch)

    nchunk = t // CMP_STRIDE
    x16 = kcv.reshape(batch, nchunk, CMP_STRIDE, 2 * NSA_KV_HEADS, HEAD_DIM).transpose(0, 3, 1, 2, 4)
    x16 = x16.reshape(batch, 2 * NSA_KV_HEADS, nchunk, CMP_STRIDE * HEAD_DIM)
    half = CMP_STRIDE * HEAD_DIM
    pos = jnp.stack([pos_k.reshape(2, 1, half), pos_v.reshape(2, 1, half)])
    w1 = jnp.stack([w1_k.reshape(2, half, CMP_HIDDEN), w1_v.reshape(2, half, CMP_HIDDEN)]).astype(MXU_DTYPE)
    w2 = jnp.stack([w2_k, w2_v]).astype(MXU_DTYPE)
    kvc = _compress(x16, pos, w1, w2)

    vct = kvc[:, NSA_KV_HEADS:].transpose(0, 1, 3, 2)
    y_nsa = _nsa(qn, qr, kvc, vct, ksr, vs, kwr, vw, gates, batch)
    w_out = w_out.astype(MXU_DTYPE)
    return _out_proj(h, y_lru, y_nsa, w_out[:LRU_WIDTH], w_out[LRU_WIDTH:])


def _rope_tables(positions):
    inv = ROPE_THETA ** (-jnp.arange(0, 2 * ROT_HALF, 2, dtype=F32) / (2 * ROT_HALF))
    ang = positions.reshape(-1).astype(F32)[:, None] * inv
    cos, sin = jnp.cos(ang), jnp.sin(ang)
    rest = HEAD_DIM - 2 * ROT_HALF
    cos_h = jnp.concatenate([cos, cos, jnp.ones((cos.shape[0], rest), F32)], axis=1)
    sin_h = jnp.concatenate([-sin, sin, jnp.zeros((cos.shape[0], rest), F32)], axis=1)
    reps = LANES // HEAD_DIM
    return jnp.tile(cos_h, (1, reps)), jnp.tile(sin_h, (1, reps))


def kernel(x, mem, positions, norm_mix, norm_xattn, norm_mem, norm_ffn, norm_final, even_w_in, even_w_out, lru_conv_w, lru_conv_b, lru_w_r, lru_b_r, lru_w_i, lru_b_i, lru_lambda, nsa_cmp_pos_k, nsa_cmp_w1_k, nsa_cmp_w2_k, nsa_cmp_pos_v, nsa_cmp_w1_v, nsa_cmp_w2_v, odd_w_in, odd_conv_w, odd_w_out, xa_wq, xa_wk, xa_wv, xa_wo, moe_w_group, moe_b_group, moe_w_expert, moe_b_expert, moe_w_gate, moe_w_up, moe_w_down):
    batch, t, d = x.shape
    n = batch * t
    depth = norm_mix.shape[0]
    cos_t, sin_t = _rope_tables(positions)
    h = x.reshape(n, d)
    mem2 = mem.reshape(-1, d)
    for layer in range(depth):
        g_mix = norm_mix[layer][None, :]
        if layer % 2 == 0:
            e = layer // 2
            h = _even_mixer(h, g_mix, batch, cos_t, sin_t, even_w_in[e], even_w_out[e], lru_conv_w[e],
                            lru_conv_b[e], lru_w_r[e], lru_b_r[e], lru_w_i[e], lru_b_i[e], lru_lambda[e],
                            nsa_cmp_pos_k[e], nsa_cmp_w1_k[e], nsa_cmp_w2_k[e],
                            nsa_cmp_pos_v[e], nsa_cmp_w1_v[e], nsa_cmp_w2_v[e])
        else:
            o = layer // 2
            h = _short_conv(h, g_mix, odd_w_in[o].astype(MXU_DTYPE), odd_conv_w[o],
                            odd_w_out[o].astype(MXU_DTYPE), batch)
        wkv = jnp.concatenate([xa_wk[layer], xa_wv[layer]], axis=1).astype(MXU_DTYPE)
        kv = _mem_kv(mem2, norm_mem[layer][None, :], wkv)
        h = _xattn(h, norm_xattn[layer][None, :], xa_wq[layer].astype(MXU_DTYPE), kv,
                   xa_wo[layer].astype(MXU_DTYPE), batch)
        h = _moe(h, norm_ffn[layer][None, :], moe_w_group[layer], moe_b_group[layer], moe_w_expert[layer],
                 moe_b_expert[layer], moe_w_gate[layer], moe_w_up[layer], moe_w_down[layer],
                 norm_final[None, :], last=layer == depth - 1)
    return h.reshape(batch, t, d)
```
